```python
import jax
import jax.numpy as jnp
from jax import lax
import numpy as np

D_MODEL = 2048
BATCH = 2
SEQ = 4096
DEPTH = 1
DEC_BATCH = 32
DEC_SEQ = 4
PAST_LEN = 16384
PAGE_SIZE = 128

N_HEADS = 16
HEAD_DIM = 128
N_KV = 2
HPG = N_HEADS // N_KV
CMP_STRIDE = 16
CMP_LEN = 2 * CMP_STRIDE
CMP_HIDDEN = HEAD_DIM
SEL_BLOCK = 64
SEL_TOPK = 16
WINDOW = 512
BAND_BLOCK = 128
SEL_Q_BLOCK = 64
SCALE = HEAD_DIM ** -0.5
CONV_CH = D_MODEL // 2
CONV_WIDTH = 31
MEM_LEN = 256
MEM_HEADS = 4
MEM_HEAD_DIM = 128
MEM_SCALE = MEM_HEAD_DIM ** -0.5
N_EXPERTS = 32
TOP_K = 4
D_FF = D_MODEL
SWIGLU_LIMIT = 7.0
SWIGLU_ALPHA = 1.702
EXPERT_BLOCK = 128
EPS = 1e-6
Q_W = N_HEADS * HEAD_DIM
KV_W = 2 * N_KV * HEAD_DIM
GATE_W = 3 * N_HEADS
GLU_W = 2 * CONV_CH
MERGE_W = 2 * D_MODEL
IN_W = Q_W + 3 * KV_W + GATE_W + GLU_W + MERGE_W
IN_SPLITS = [Q_W, Q_W + KV_W, Q_W + 2 * KV_W, Q_W + 3 * KV_W, Q_W + 3 * KV_W + GATE_W, Q_W + 3 * KV_W + GATE_W + GLU_W]

kernel_name = 'nsa_conformer_moe_hybrid_step'


def rmsnorm(x, g):
    xf = x.astype(jnp.float32)
    y = xf * lax.rsqrt(jnp.mean(xf * xf, axis=-1, keepdims=True) + EPS)
    return (y * g.astype(jnp.float32)).astype(x.dtype)


def masked_softmax(s, mask):
    s = jnp.where(mask, s.astype(jnp.float32), -jnp.inf)
    m = jnp.max(s, axis=-1, keepdims=True)
    e = jnp.exp(s - jnp.where(jnp.isfinite(m), m, 0.0))
    return e / jnp.maximum(jnp.sum(e, axis=-1, keepdims=True), 1e-30)


def front(x, g_mix, w_in):
    n, t, _ = x.shape
    q, kvc, kvs, kvw, gates, glu_in, mlog = jnp.split(rmsnorm(x, g_mix) @ w_in, IN_SPLITS, axis=-1)
    kv_shape = (n, t, 2, N_KV, HEAD_DIM)
    return (q.reshape(n, t, N_KV, HPG, HEAD_DIM), kvc.reshape(kv_shape), kvs.reshape(kv_shape),
            kvw.reshape(kv_shape), jax.nn.sigmoid(gates).reshape(n, t, 3, N_KV, HPG), glu_in, mlog)


def compress(x, pe, w1, w2):
    n, length = x.shape[:2]
    ch = x.reshape(n, length // CMP_STRIDE, CMP_STRIDE, N_KV, HEAD_DIM)
    w_a, w_b = w1[:CMP_STRIDE], w1[CMP_STRIDE:]
    a = jnp.einsum('ncigd,ide->ncge', ch, w_a) + jnp.einsum('id,ide->e', pe[:CMP_STRIDE], w_a)
    b = jnp.einsum('ncigd,ide->ncge', ch, w_b) + jnp.einsum('id,ide->e', pe[CMP_STRIDE:], w_b)
    return jax.nn.gelu(a[:, :-1] + b[:, 1:]) @ w2


def compress_kv(kv, cmp_pe, cmp_w1, cmp_w2):
    return (compress(kv[:, :, 0], cmp_pe[0], cmp_w1[0], cmp_w2[0]),
            compress(kv[:, :, 1], cmp_pe[1], cmp_w1[1], cmp_w2[1]))


def cmp_branch(q, q_pos, kc, vc):
    nblk = kc.shape[1]
    s = jnp.einsum('ntghd,ncgd->nghtc', q, kc) * SCALE
    end = jnp.arange(nblk) * CMP_STRIDE + (CMP_LEN - 1)
    p = masked_softmax(s, end[None, :] <= q_pos[:, None])
    o = jnp.einsum('nghtc,ncgd->ntghd', p.astype(vc.dtype), vc)
    return o, jnp.sum(p, axis=2)


def select_blocks(p_grp, q_pos, n_sel):
    nblk = p_grp.shape[-1]
    c0 = jnp.arange(nblk)[:, None] * CMP_STRIDE
    s0 = jnp.arange(n_sel)[None, :] * SEL_BLOCK
    overlap = ((c0 < s0 + SEL_BLOCK) & (c0 + CMP_LEN > s0)).astype(jnp.float32)
    score = jnp.einsum('ngtc,cj->ngtj', p_grp, overlap)
    blk = jnp.arange(n_sel)[None, :]
    cur = (q_pos // SEL_BLOCK)[:, None]
    avail = blk <= cur
    forced = avail & ((blk == 0) | (blk == cur) | (blk == cur - 1))
    score = jnp.where(forced, jnp.inf, jnp.where(avail, score, -jnp.inf))
    top, idx = lax.top_k(score, min(SEL_TOPK, n_sel))
    return idx, top > -jnp.inf


def sel_core(q, q_pos, idx, valid, kg, vg):
    s = jnp.einsum('ntghd,ngtskd->nghtsk', q, kg) * SCALE
    shp = s.shape
    tok = idx[..., None] * SEL_BLOCK + jnp.arange(SEL_BLOCK)
    mask = valid[..., None] & (tok <= q_pos[:, None, None])
    p = masked_softmax(s.reshape(shp[:4] + (-1,)), mask.reshape(mask.shape[:3] + (-1,))[:, :, None])
    return jnp.einsum('nghtsk,ngtskd->ntghd', p.reshape(shp).astype(vg.dtype), vg)


def sel_prompt(q, q_pos, idx, valid, kvs):
    n, t = q.shape[:2]
    kb = kvs.reshape(n, t // SEL_BLOCK, SEL_BLOCK, 2, N_KV, HEAD_DIM)
    n_i = jnp.arange(n)[:, None, None, None]
    g_i = jnp.arange(N_KV)[None, :, None, None]
    nq = t // SEL_Q_BLOCK

    def step(args):
        qc, pc, ic, vc = args
        kvg = kb[n_i, ic, :, :, g_i]
        return sel_core(qc, pc, ic, vc, kvg[..., 0, :], kvg[..., 1, :])

    xs = (q.reshape(n, nq, SEL_Q_BLOCK, N_KV, HPG, HEAD_DIM).swapaxes(0, 1),
          q_pos.reshape(nq, SEL_Q_BLOCK),
          idx.reshape(n, N_KV, nq, SEL_Q_BLOCK, -1).transpose(2, 0, 1, 3, 4),
          valid.reshape(n, N_KV, nq, SEL_Q_BLOCK, -1).transpose(2, 0, 1, 3, 4))
    o = lax.map(step, xs)
    return o.swapaxes(0, 1).reshape(n, t, N_KV, HPG, HEAD_DIM)


def sel_sample(q, q_pos, idx, valid, pool, page_table, kv_new):
    n, tq = q.shape[:2]
    bpp = PAGE_SIZE // SEL_BLOCK
    n_past_blk = page_table.shape[1] * bpp
    n_new_blk = -(-tq // SEL_BLOCK)
    new_blocks = jnp.pad(kv_new, ((0, 0), (0, n_new_blk * SEL_BLOCK - tq), (0, 0), (0, 0), (0, 0))).reshape(
        n, n_new_blk, SEL_BLOCK, 2, N_KV, HEAD_DIM)
    n_i = jnp.arange(n)[:, None, None, None]
    g_i = jnp.arange(N_KV)[None, :, None, None]
    pidx = jnp.minimum(idx, n_past_blk - 1)
    page = page_table[n_i, pidx // bpp][..., None]
    row = (pidx % bpp)[..., None] * SEL_BLOCK + jnp.arange(SEL_BLOCK)
    from_pool = pool[page, row, :, g_i[..., None]]
    from_new = new_blocks[n_i, jnp.clip(idx - n_past_blk, 0, n_new_blk - 1), :, :, g_i]
    kvg = jnp.where((idx < n_past_blk)[..., None, None, None], from_pool, from_new)
    return sel_core(q, q_pos, idx, valid, kvg[..., 0, :], kvg[..., 1, :])


def window_core(q, q_pos, kv, k_pos):
    s = jnp.einsum('nbqghd,nbkgd->nbghqk', q, kv[:, :, :, 0]) * SCALE
    kp, qp = k_pos[:, None, :], q_pos[:, :, None]
    mask = (kp <= qp) & (kp > qp - WINDOW) & (kp >= 0)
    p = masked_softmax(s, mask[:, None, None])
    return jnp.einsum('nbghqk,nbkgd->nbqghd', p.astype(kv.dtype), kv[:, :, :, 1])


def window_prompt(q, kvw):
    n, t = q.shape[:2]
    nb = t // BAND_BLOCK
    nband = WINDOW // BAND_BLOCK
    kvp = jnp.pad(kvw, ((0, 0), (nband * BAND_BLOCK, 0), (0, 0), (0, 0), (0, 0))).reshape(
        n, nb + nband, BAND_BLOCK, 2, N_KV, HEAD_DIM)
    band = jnp.concatenate([kvp[:, i:i + nb] for i in range(nband + 1)], axis=2)
    q_pos = jnp.arange(t).reshape(nb, BAND_BLOCK)
    k_pos = (jnp.arange(nb)[:, None] - nband) * BAND_BLOCK + jnp.arange((nband + 1) * BAND_BLOCK)[None, :]
    o = window_core(q.reshape(n, nb, BAND_BLOCK, N_KV, HPG, HEAD_DIM), q_pos, band, k_pos)
    return o.reshape(n, t, N_KV, HPG, HEAD_DIM)


def window_sample(q, q_pos, buf, kv_new, past):
    wb, tq = buf.shape[1], q.shape[1]
    kv = jnp.concatenate([buf, kv_new], axis=1)
    k_pos = (past - wb + jnp.arange(wb + tq))[None]
    o = window_core(q[:, None], q_pos[None], kv[:, None], k_pos)[:, 0]
    return o, kv[:, -min(WINDOW, wb + tq):]


def glu(z):
    a, b = jnp.split(z, 2, axis=-1)
    return a * jax.nn.sigmoid(b)


def conv_module(up, w_dw, b_dw, ln_g, ln_b, w_pw):
    y = lax.conv_general_dilated(up, w_dw[:, None, :], window_strides=(1,), padding='VALID',
                                 dimension_numbers=('NWC', 'WIO', 'NWC'), feature_group_count=CONV_CH) + b_dw
    yf = y.astype(jnp.float32)
    yc = yf - jnp.mean(yf, axis=-1, keepdims=True)
    yn = yc * lax.rsqrt(jnp.mean(yc * yc, axis=-1, keepdims=True) + EPS) * ln_g + ln_b
    return jax.nn.silu(yn).astype(up.dtype) @ w_pw


def mixer_out(x, mlog, gates, o_cmp, o_sel, o_win, up, w_nsa_o, w_dw, b_dw, ln_conv_g, ln_conv_b, w_conv_o, w_out):
    n, t = x.shape[:2]
    g = gates[..., None]
    o = g[:, :, 0] * o_cmp + g[:, :, 1] * o_sel + g[:, :, 2] * o_win
    y_att = o.reshape(n, t, Q_W) @ w_nsa_o
    y_conv = conv_module(up, w_dw, b_dw, ln_conv_g, ln_conv_b, w_conv_o)
    g_att, g_conv = jnp.split(jax.nn.sigmoid(mlog), 2, axis=-1)
    return x + (g_att * y_att + g_conv * y_conv) @ w_out


def mem_kv(mem, g_mem, w_xkv):
    return (rmsnorm(mem, g_mem) @ w_xkv).reshape(mem.shape[0], MEM_LEN, 2, MEM_HEADS, MEM_HEAD_DIM)


def cross_attn(h, kv, w_xq, w_xo):
    n, t = h.shape[:2]
    q = (h @ w_xq).reshape(n, t, MEM_HEADS, MEM_HEAD_DIM)
    s = jnp.einsum('nthd,nmhd->nhtm', q, kv[:, :, 0]) * MEM_SCALE
    p = jax.nn.softmax(s.astype(jnp.float32), axis=-1)
    o = jnp.einsum('nhtm,nmhd->nthd', p.astype(kv.dtype), kv[:, :, 1])
    return o.reshape(n, t, MEM_HEADS * MEM_HEAD_DIM) @ w_xo


def moe(h, w_router, b_router, w_gu, b_gu, w_down, b_down):
    shp = h.shape
    t = h.reshape(-1, shp[-1])
    n = t.shape[0]
    logits = (t @ w_router).astype(jnp.float32) + b_router.astype(jnp.float32)
    top_v, top_e = lax.top_k(logits, TOP_K)
    gate = jax.nn.softmax(top_v, axis=-1)
    flat_e = top_e.reshape(-1)
    order = jnp.argsort(flat_e)
    e_sorted = flat_e[order]
    counts = jnp.bincount(flat_e, length=N_EXPERTS)
    padded = (counts + EXPERT_BLOCK - 1) // EXPERT_BLOCK * EXPERT_BLOCK
    pad_end = jnp.cumsum(padded)
    dest = (pad_end - padded)[e_sorted] + jnp.arange(n * TOP_K) - (jnp.cumsum(counts) - counts)[e_sorted]
    n_blk = -(-(n * TOP_K) // EXPERT_BLOCK) + N_EXPERTS
    slots = n_blk * EXPERT_BLOCK
    slot_tok = jnp.full((slots,), n, jnp.int32).at[dest].set((order // TOP_K).astype(jnp.int32))
    slot_gate = jnp.zeros((slots,), jnp.float32).at[dest].set(gate.reshape(-1)[order])
    blk_e = jnp.minimum(jnp.searchsorted(pad_end, jnp.arange(n_blk) * EXPERT_BLOCK, side='right'), N_EXPERTS - 1)
    t_pad = jnp.concatenate([t, jnp.zeros((1, shp[-1]), t.dtype)], axis=0)

    def expert_block(args):
        tok, g, e = args
        z = (t_pad[tok] @ w_gu[e] + b_gu[e]).astype(jnp.float32)
        a, u = jnp.split(z, 2, axis=-1)
        a = jnp.minimum(a, SWIGLU_LIMIT)
        u = jnp.clip(u, -SWIGLU_LIMIT, SWIGLU_LIMIT)
        act = a * jax.nn.sigmoid(SWIGLU_ALPHA * a) * (u + 1.0)
        out = act.astype(t.dtype) @ w_down[e] + b_down[e]
        return out.astype(jnp.float32) * g[:, None]

    out = lax.map(expert_block, (slot_tok.reshape(n_blk, EXPERT_BLOCK), slot_gate.reshape(n_blk, EXPERT_BLOCK), blk_e))
    y = jnp.zeros((n + 1, shp[-1]), jnp.float32).at[slot_tok].add(out.reshape(slots, shp[-1]))
    return y[:n].astype(h.dtype).reshape(shp)


def post_mixer(x, mkv, g_xattn, w_xq, w_xo, g_moe, w_router, b_router, w_gu, b_gu, w_down, b_down):
    x = x + cross_attn(rmsnorm(x, g_xattn), mkv, w_xq, w_xo)
    return x + moe(rmsnorm(x, g_moe), w_router, b_router, w_gu, b_gu, w_down, b_down)


def setup_inputs(seed: int = 0) -> dict:
    key = jax.random.key(seed)
    ks = jax.random.split(key, 40)

    def nrm(i, shape, scale):
        return jax.random.normal(ks[i], shape, jnp.float32) * scale

    n_pages = PAST_LEN // PAGE_SIZE
    n_used = DEC_BATCH * n_pages
    n_pool = n_used + max(1, n_used // 4)
    win_buf = min(WINDOW, PAST_LEN)
    mem_w = MEM_HEADS * MEM_HEAD_DIM
    return {
        'x_prompt': nrm(0, (BATCH, SEQ, D_MODEL), 1.0),
        'x_sample': nrm(1, (DEC_BATCH, DEC_SEQ, D_MODEL), 1.0),
        'cache_kv_cmp': nrm(2, (DEPTH, n_pool, PAGE_SIZE, 2, N_KV, HEAD_DIM), 1.0),
        'cache_kv_sel': nrm(3, (DEPTH, n_pool, PAGE_SIZE, 2, N_KV, HEAD_DIM), 1.0),
        'cache_kv_win': nrm(4, (DEPTH, DEC_BATCH, win_buf, 2, N_KV, HEAD_DIM), 1.0),
        'cache_conv': nrm(5, (DEPTH, DEC_BATCH, CONV_WIDTH - 1, CONV_CH), 0.5),
        'cache_mem_kv': nrm(6, (DEPTH, DEC_BATCH, MEM_LEN, 2, MEM_HEADS, MEM_HEAD_DIM), 1.0),
        'page_table': jax.random.permutation(ks[7], n_pool)[:n_used].reshape(DEC_BATCH, n_pages).astype(jnp.int32),
        'mem_prompt': nrm(8, (BATCH, MEM_LEN, D_MODEL), 1.0),
        'g_mix': 1.0 + nrm(9, (DEPTH, D_MODEL), 0.02),
        'w_in': nrm(10, (DEPTH, D_MODEL, IN_W), D_MODEL ** -0.5),
        'cmp_pe': nrm(11, (DEPTH, 2, CMP_LEN, HEAD_DIM), 0.1),
        'cmp_w1': nrm(12, (DEPTH, 2, CMP_LEN, HEAD_DIM, CMP_HIDDEN), (CMP_LEN * HEAD_DIM) ** -0.5),
        'cmp_w2': nrm(13, (DEPTH, 2, CMP_HIDDEN, HEAD_DIM), 1.5 * CMP_HIDDEN ** -0.5),
        'w_nsa_o': nrm(14, (DEPTH, Q_W, D_MODEL), Q_W ** -0.5),
        'w_dw': nrm(15, (DEPTH, CONV_WIDTH, CONV_CH), CONV_WIDTH ** -0.5),
        'b_dw': nrm(16, (DEPTH, CONV_CH), 0.02),
        'ln_conv_g': 1.0 + nrm(17, (DEPTH, CONV_CH), 0.02),
        'ln_conv_b': nrm(18, (DEPTH, CONV_CH), 0.02),
        'w_conv_o': nrm(19, (DEPTH, CONV_CH, D_MODEL), CONV_CH ** -0.5),
        'w_out': nrm(20, (DEPTH, D_MODEL, D_MODEL), D_MODEL ** -0.5),
        'g_xattn': 1.0 + nrm(21, (DEPTH, D_MODEL), 0.02),
        'g_mem': 1.0 + nrm(22, (DEPTH, D_MODEL), 0.02),
        'w_xq': nrm(23, (DEPTH, D_MODEL, mem_w), D_MODEL ** -0.5),
        'w_xkv': nrm(24, (DEPTH, D_MODEL, 2 * mem_w), D_MODEL ** -0.5),
        'w_xo': nrm(25, (DEPTH, mem_w, D_MODEL), mem_w ** -0.5),
        'g_moe': 1.0 + nrm(26, (DEPTH, D_MODEL), 0.02),
        'w_router': nrm(27, (DEPTH, D_MODEL, N_EXPERTS), D_MODEL ** -0.5),
        'b_router': nrm(28, (DEPTH, N_EXPERTS), 0.01),
        'w_gu': nrm(29, (DEPTH, N_EXPERTS, D_MODEL, 2 * D_FF), D_MODEL ** -0.5),
        'b_gu': nrm(30, (DEPTH, N_EXPERTS, 2 * D_FF), 0.02),
        'w_down': nrm(31, (DEPTH, N_EXPERTS, D_FF, D_MODEL), D_FF ** -0.5),
        'b_down': nrm(32, (DEPTH, N_EXPERTS, D_MODEL), 0.02),
        'g_final': 1.0 + nrm(33, (D_MODEL,), 0.02),
    }


def reference(x_prompt, x_sample, cache_kv_cmp, cache_kv_sel, cache_kv_win, cache_conv, cache_mem_kv, page_table,
              mem_prompt, g_mix, w_in, cmp_pe, cmp_w1, cmp_w2, w_nsa_o, w_dw, b_dw, ln_conv_g, ln_conv_b, w_conv_o,
              w_out, g_xattn, g_mem, w_xq, w_xkv, w_xo, g_moe, w_router, b_router, w_gu, b_gu, w_down, b_down, g_final):
    t = x_prompt.shape[1]
    nd, tq = x_sample.shape[:2]
    past = page_table.shape[1] * PAGE_SIZE
    pos_p = jnp.arange(t)
    pos_s = past + jnp.arange(tq)
    xp, xs = x_prompt, x_sample
    p_cmp, p_sel, p_win, p_conv, p_mem = [], [], [], [], []
    s_cmp, s_sel, s_win, s_conv = [], [], [], []
    for l in range(DEPTH):
        q, kvc, kvs, kvw, gates, glu_in, mlog = front(xp, g_mix[l], w_in[l])
        kc, vc = compress_kv(kvc, cmp_pe[l], cmp_w1[l], cmp_w2[l])
        o_cmp, p_grp = cmp_branch(q, pos_p, kc, vc)
        idx, valid = select_blocks(p_grp, pos_p, t // SEL_BLOCK)
        o_sel = sel_prompt(q, pos_p, idx, valid, kvs)
        o_win = window_prompt(q, kvw)
        up = jnp.pad(glu(glu_in), ((0, 0), (CONV_WIDTH - 1, 0), (0, 0)))
        xp = mixer_out(xp, mlog, gates, o_cmp, o_sel, o_win, up, w_nsa_o[l], w_dw[l], b_dw[l], ln_conv_g[l],
                       ln_conv_b[l], w_conv_o[l], w_out[l])
        mkv = mem_kv(mem_prompt, g_mem[l], w_xkv[l])
        xp = post_mixer(xp, mkv, g_xattn[l], w_xq[l], w_xo[l], g_moe[l], w_router[l], b_router[l], w_gu[l],
                        b_gu[l], w_down[l], b_down[l])
        p_cmp.append(kvc)
        p_sel.append(kvs)
        p_win.append(kvw[:, t - min(WINDOW, t):])
        p_conv.append(up[:, t:])
        p_mem.append(mkv)
        q, kvc, kvs, kvw, gates, glu_in, mlog = front(xs, g_mix[l], w_in[l])
        total = ((past + tq) // CMP_STRIDE) * CMP_STRIDE
        hist = jnp.concatenate([cache_kv_cmp[l, page_table].reshape(nd, past, 2, N_KV, HEAD_DIM), kvc], axis=1)[:, :total]
        kc, vc = compress_kv(hist, cmp_pe[l], cmp_w1[l], cmp_w2[l])
        o_cmp, p_grp = cmp_branch(q, pos_s, kc, vc)
        idx, valid = select_blocks(p_grp, pos_s, -(-(past + tq) // SEL_BLOCK))
        o_sel = sel_sample(q, pos_s, idx, valid, cache_kv_sel[l], page_table, kvs)
        o_win, win_new = window_sample(q, pos_s, cache_kv_win[l], kvw, past)
        up = jnp.concatenate([cache_conv[l], glu(glu_in)], axis=1)
        xs = mixer_out(xs, mlog, gates, o_cmp, o_sel, o_win, up, w_nsa_o[l], w_dw[l], b_dw[l], ln_conv_g[l],
                       ln_conv_b[l], w_conv_o[l], w_out[l])
        xs = post_mixer(xs, cache_mem_kv[l], g_xattn[l], w_xq[l], w_xo[l], g_moe[l], w_router[l], b_router[l],
                        w_gu[l], b_gu[l], w_down[l], b_down[l])
        s_cmp.append(kvc)
        s_sel.append(kvs)
        s_win.append(win_new)
        s_conv.append(up[:, tq:])
    y_prompt = rmsnorm(xp, g_final)
    y_sample = rmsnorm(xs, g_final)
    p_kv_cmp = jnp.stack(p_cmp)
    p_kv_sel = jnp.stack(p_sel)
    p_kv_win = jnp.stack(p_win)
    p_conv_buf = jnp.stack(p_conv)
    p_mem_kv = jnp.stack(p_mem)
    s_kv_cmp = jnp.stack(s_cmp)
    s_kv_sel = jnp.stack(s_sel)
    s_kv_win = jnp.stack(s_win)
    s_conv_buf = jnp.stack(s_conv)
    return (y_prompt, y_sample, p_kv_cmp, p_kv_sel, p_kv_win, p_conv_buf, p_mem_kv, s_kv_cmp, s_kv_sel, s_kv_win, s_conv_buf)
```

```python
import functools

import jax
import jax.numpy as jnp
from jax import lax
from jax.experimental import pallas as pl
from jax.experimental.pallas import tpu as pltpu

D_MODEL = 2048
PAGE_SIZE = 128

N_HEADS = 16
HEAD_DIM = 128
N_KV = 2
HPG = N_HEADS // N_KV
CMP_STRIDE = 16
CMP_LEN = 2 * CMP_STRIDE
CMP_HALF = CMP_STRIDE * HEAD_DIM
SEL_BLOCK = 64
SEL_TOPK = 16
WINDOW = 512
SCALE = HEAD_DIM ** -0.5
CONV_CH = D_MODEL // 2
CONV_WIDTH = 31
MEM_LEN = 256
MEM_HEADS = 4
MEM_HEAD_DIM = 128
MEM_W = MEM_HEADS * MEM_HEAD_DIM
MEM_SCALE = MEM_HEAD_DIM ** -0.5
N_EXPERTS = 32
TOP_K = 4
SWIGLU_LIMIT = 7.0
SWIGLU_ALPHA = 1.702
EPS = 1e-6
Q_W = N_HEADS * HEAD_DIM
KV_W = 2 * N_KV * HEAD_DIM
GATE_W = 3 * N_HEADS
GLU_W = 2 * CONV_CH
MERGE_W = 2 * D_MODEL
IN_SPLITS = [Q_W, Q_W + KV_W, Q_W + 2 * KV_W, Q_W + 3 * KV_W, Q_W + 3 * KV_W + GATE_W, Q_W + 3 * KV_W + GATE_W + GLU_W]

V7X_VMEM_BYTES = 64 * 1024 * 1024
VMEM_LIMIT_BYTES = V7X_VMEM_BYTES - 8 * 1024 * 1024
LANES = 128

MOE_BLOCK_ROWS = 1024
MOE_SUB_ROWS = 256
MOE_FF_TILE = 256
NSA_Q_TILE = 128
NSA_KV_TILE = 512
CONV_CTX = 32
CONV_ROWS = 64
CHUNKS_PER_PAGE = PAGE_SIZE // CMP_STRIDE
CMP_PAGES = 32
DEC_PAD = 8

NEG = -1e30
_NT = (((1,), (1,)), ((), ()))


def _params(*semantics):
    return pltpu.CompilerParams(dimension_semantics=semantics, vmem_limit_bytes=VMEM_LIMIT_BYTES)


def _row_block(rows, target):
    if rows <= target:
        return rows
    for b in range(target, 15, -1):
        if rows % b == 0 and b % 16 == 0:
            return b
    return rows


def _linear_kernel(*refs, n_w, norm, n_tile, n_row, epilogue):
    x_ref = refs[0]
    k = 1
    g_ref = refs[k] if norm else None
    k += int(norm)
    w_refs = refs[k:k + n_w]
    k += n_w
    tile_refs = refs[k:k + n_tile]
    k += n_tile
    row_refs = refs[k:k + n_row]
    k += n_row
    o_ref, h_ref = refs[k], refs[k + 1]

    @pl.when(pl.program_id(1) == 0)
    def _():
        x = x_ref[...].astype(jnp.float32)
        if norm:
            x = x * lax.rsqrt(jnp.mean(x * x, axis=-1, keepdims=True) + EPS) * g_ref[...]
        h_ref[...] = x.astype(jnp.bfloat16)

    h = h_ref[...]
    zs = [jnp.dot(h, w[...], preferred_element_type=jnp.float32) for w in w_refs]
    o_ref[...] = epilogue(zs, [t[...] for t in tile_refs], [r[...] for r in row_refs]).astype(o_ref.dtype)


def fused_linear(x, ws, *, gain=None, tiles=(), rows=(), epilogue=None, out_dtype=jnp.float32,
                 block_m=1024, block_n=512, name="fused_linear"):
    m, k = x.shape
    n = ws[0].shape[1]
    bm = _row_block(m, block_m)
    bn = min(block_n, n)
    assert m % bm == 0 and n % bn == 0 and all(w.shape == (k, n) for w in ws)
    if epilogue is None:
        epilogue = _first_epilogue
    norm = gain is not None
    in_specs = [pl.BlockSpec((bm, k), lambda i, j: (i, 0))]
    args = [x]
    if norm:
        in_specs.append(pl.BlockSpec((1, k), lambda i, j: (0, 0)))
        args.append(gain.reshape(1, k).astype(jnp.float32))
    in_specs += [pl.BlockSpec((k, bn), lambda i, j: (0, j)) for _ in ws]
    args += list(ws)
    in_specs += [pl.BlockSpec((bm, bn), lambda i, j: (i, j)) for _ in tiles]
    args += list(tiles)
    in_specs += [pl.BlockSpec((1, bn), lambda i, j: (0, j)) for _ in rows]
    args += list(rows)
    return pl.pallas_call(
        functools.partial(_linear_kernel, n_w=len(ws), norm=norm, n_tile=len(tiles), n_row=len(rows),
                          epilogue=epilogue),
        grid=(m // bm, n // bn),
        in_specs=in_specs,
        out_specs=pl.BlockSpec((bm, bn), lambda i, j: (i, j)),
        out_shape=jax.ShapeDtypeStruct((m, n), out_dtype),
        scratch_shapes=[pltpu.VMEM((bm, k), jnp.bfloat16)],
        compiler_params=_params("parallel", "arbitrary"),
        name=name,
    )(*args)


def _first_epilogue(zs, ts, rs):
    return zs[0]


def _glu_epilogue(zs, ts, rs):
    return zs[0] * jax.nn.sigmoid(zs[1])


def _sigmoid_epilogue(zs, ts, rs):
    return jax.nn.sigmoid(zs[0])


def _residual_epilogue(zs, ts, rs):
    return ts[0] + zs[0]


def _merge_kernel(o_ref, a_ref, wo_ref, wc_ref, ga_ref, gc_ref, out_ref):
    y_att = jnp.dot(o_ref[...], wo_ref[...], preferred_element_type=jnp.float32)
    y_conv = jnp.dot(a_ref[...], wc_ref[...], preferred_element_type=jnp.float32)
    out_ref[...] = (ga_ref[...] * y_att + gc_ref[...] * y_conv).astype(out_ref.dtype)


def merge_branches(o, act, w_nsa_o, w_conv_o, merge_gates, *, block_m=1024, block_n=512):
    m = o.shape[0]
    d = w_nsa_o.shape[1]
    bm = _row_block(m, block_m)
    bn = min(block_n, d)
    nj = d // bn
    return pl.pallas_call(
        _merge_kernel,
        grid=(m // bm, nj),
        in_specs=[pl.BlockSpec((bm, o.shape[1]), lambda i, j: (i, 0)),
                  pl.BlockSpec((bm, act.shape[1]), lambda i, j: (i, 0)),
                  pl.BlockSpec((w_nsa_o.shape[0], bn), lambda i, j: (0, j)),
                  pl.BlockSpec((w_conv_o.shape[0], bn), lambda i, j: (0, j)),
                  pl.BlockSpec((bm, bn), lambda i, j: (i, j)),
                  pl.BlockSpec((bm, bn), lambda i, j: (i, j + nj))],
        out_specs=pl.BlockSpec((bm, bn), lambda i, j: (i, j)),
        out_shape=jax.ShapeDtypeStruct((m, d), jnp.bfloat16),
        compiler_params=_params("parallel", "arbitrary"),
        name="merge_branches",
    )(o, act, w_nsa_o, w_conv_o, merge_gates, merge_gates)


def _compress_kernel(x_ref, w1_ref, pe_ref, w2_ref, o_ref):
    c = x_ref.shape[1]
    for j in range(2):
        w_a = w1_ref[j, :CMP_HALF, :]
        w_b = w1_ref[j, CMP_HALF:, :]
        pe = pe_ref[j]
        bias = (jnp.dot(pe[:, :CMP_HALF], w_a, preferred_element_type=jnp.float32)
                + jnp.dot(pe[:, CMP_HALF:], w_b, preferred_element_type=jnp.float32))[0:1]
        xs = []
        for g in range(N_KV):
            base = (j * N_KV + g) * HEAD_DIM
            xs.append(jnp.concatenate(
                [x_ref[0, :, i * KV_W + base:i * KV_W + base + HEAD_DIM] for i in range(CMP_STRIDE)], axis=1))
        x = jnp.concatenate(xs, axis=0).astype(jnp.bfloat16)
        a = jnp.dot(x, w_a, preferred_element_type=jnp.float32)
        b = jnp.dot(x, w_b, preferred_element_type=jnp.float32)
        for g in range(N_KV):
            b_next = pltpu.roll(b[g * c:(g + 1) * c], c - 1, axis=0)
            h = jax.nn.gelu(a[g * c:(g + 1) * c] + b_next + bias)
            o_ref[0, j, g] = jnp.dot(h.astype(jnp.bfloat16), w2_ref[j],
                                     preferred_element_type=jnp.float32).astype(o_ref.dtype)


def _compress_weights(cmp_pe, cmp_w1, cmp_w2):
    bf = jnp.bfloat16
    w1 = cmp_w1.reshape(2, CMP_LEN * HEAD_DIM, HEAD_DIM).astype(bf)
    pe = jnp.broadcast_to(cmp_pe.reshape(2, 1, CMP_LEN * HEAD_DIM), (2, 8, CMP_LEN * HEAD_DIM)).astype(bf)
    return w1, pe, cmp_w2.astype(bf)


def compress_prompt(kvc, cmp_pe, cmp_w1, cmp_w2):
    n, t, _ = kvc.shape
    c = t // CMP_STRIDE
    w1, pe, w2 = _compress_weights(cmp_pe, cmp_w1, cmp_w2)
    return pl.pallas_call(
        _compress_kernel,
        grid=(n,),
        in_specs=[pl.BlockSpec((1, c, CMP_STRIDE * KV_W), lambda b: (b, 0, 0)),
                  pl.BlockSpec(w1.shape, lambda b: (0, 0, 0)),
                  pl.BlockSpec(pe.shape, lambda b: (0, 0, 0)),
                  pl.BlockSpec(w2.shape, lambda b: (0, 0, 0))],
        out_specs=pl.BlockSpec((1, 2, N_KV, c, HEAD_DIM), lambda b: (b, 0, 0, 0, 0)),
        out_shape=jax.ShapeDtypeStruct((n, 2, N_KV, c, HEAD_DIM), jnp.bfloat16),
        compiler_params=_params("parallel"),
        name="compress_prompt",
    )(kvc.reshape(n, c, CMP_STRIDE * KV_W), w1, pe, w2)


def _softmax_rows(s, mask):
    sm = jnp.where(mask, s, NEG)
    e = jnp.where(mask, jnp.exp(sm - jnp.max(sm, axis=-1, keepdims=True)), 0.0)
    return e / jnp.maximum(jnp.sum(e, axis=-1, keepdims=True), 1e-30)


def _select_blocks_t(score_t, qpos_l):
    blk = lax.broadcasted_iota(jnp.int32, score_t.shape, 0)
    cur = lax.shift_right_logical(qpos_l, SEL_BLOCK.bit_length() - 1)
    avail = blk <= cur
    forced = avail & ((blk == 0) | (blk == cur) | (blk == cur - 1))
    work = jnp.where(forced, jnp.inf, jnp.where(avail, score_t, -jnp.inf))
    blk_f = blk.astype(jnp.float32)
    sel = jnp.zeros(score_t.shape, jnp.float32)
    for _ in range(SEL_TOPK):
        m = jnp.max(work, axis=0, keepdims=True)
        first = jnp.min(jnp.where(work == m, blk_f, float(score_t.shape[0])), axis=0, keepdims=True)
        hit = blk_f == first
        sel = jnp.where(hit & (m > -jnp.inf), 1.0, sel)
        work = jnp.where(hit, -jnp.inf, work)
    return sel


def _nsa_prompt_kernel(q_ref, gate_ref, ckv_ref, ks_ref, kw_ref, o_ref, m_ref, l_ref, acc_ref, *, tq, tk):
    b = pl.program_id(1)
    q0 = b * tq
    qpos = q0 + lax.broadcasted_iota(jnp.int32, (tq, 1), 0)
    qpos_l = q0 + lax.broadcasted_iota(jnp.int32, (1, tq), 1)
    n_cmp = ckv_ref.shape[3]
    gates = gate_ref[0]
    blocks_per_tile = tk // SEL_BLOCK

    for g in range(N_KV):
        kcol = slice(g * HEAD_DIM, (g + 1) * HEAD_DIM)
        vcol = slice((N_KV + g) * HEAD_DIM, (N_KV + g + 1) * HEAD_DIM)
        qg = jnp.concatenate([q_ref[0, :, (g * HPG + hh) * HEAD_DIM:(g * HPG + hh + 1) * HEAD_DIM]
                              for hh in range(HPG)], axis=0)

        s = lax.dot_general(qg, ckv_ref[0, 0, g], _NT, preferred_element_type=jnp.float32) * SCALE
        cend = lax.broadcasted_iota(jnp.int32, (1, n_cmp), 1) * CMP_STRIDE + (CMP_LEN - 1)
        vis = cend <= qpos
        ps = [_softmax_rows(s[hh * tq:(hh + 1) * tq], vis) for hh in range(HPG)]
        o_cmp = jnp.dot(jnp.concatenate(ps, axis=0).astype(jnp.bfloat16), ckv_ref[0, 1, g],
                        preferred_element_type=jnp.float32)
        p_grp = ps[0]
        for p in ps[1:]:
            p_grp = p_grp + p

        c0 = lax.broadcasted_iota(jnp.int32, (LANES, n_cmp), 1) * CMP_STRIDE
        s0 = lax.broadcasted_iota(jnp.int32, (LANES, n_cmp), 0) * SEL_BLOCK
        overlap_t = ((c0 < s0 + SEL_BLOCK) & (c0 + CMP_LEN > s0)).astype(jnp.bfloat16)
        score_t = lax.dot_general(overlap_t, p_grp.astype(jnp.bfloat16), _NT, preferred_element_type=jnp.float32)
        sel = _select_blocks_t(score_t, qpos_l).T.astype(jnp.bfloat16)

        m_ref[...] = jnp.full(m_ref.shape, NEG, jnp.float32)
        l_ref[...] = jnp.zeros(l_ref.shape, jnp.float32)
        acc_ref[...] = jnp.zeros(acc_ref.shape, jnp.float32)

        def sel_tile(kt, causal):
            k0 = pl.multiple_of(kt * tk, tk)
            s = lax.dot_general(qg, ks_ref[0, pl.ds(k0, tk), kcol], _NT, preferred_element_type=jnp.float32) * SCALE
            jrow = lax.broadcasted_iota(jnp.int32, (LANES, tk), 0)
            jcol = lax.shift_right_logical(lax.broadcasted_iota(jnp.int32, (LANES, tk), 1),
                                           SEL_BLOCK.bit_length() - 1)
            expand = (jrow == kt * blocks_per_tile + jcol).astype(jnp.bfloat16)
            mask = jnp.dot(sel, expand, preferred_element_type=jnp.float32) > 0.5
            if causal:
                mask = mask & (k0 + lax.broadcasted_iota(jnp.int32, (1, tk), 1) <= qpos)
            ps = []
            for hh in range(HPG):
                rows = slice(hh * tq, (hh + 1) * tq)
                sh = jnp.where(mask, s[rows], NEG)
                m_old = m_ref[rows]
                m_new = jnp.maximum(m_old, jnp.max(sh, axis=-1, keepdims=True))
                p = jnp.where(mask, jnp.exp(sh - m_new), 0.0)
                alpha = jnp.exp(m_old - m_new)
                l_ref[rows] = alpha * l_ref[rows] + jnp.sum(p, axis=-1, keepdims=True)
                acc_ref[rows] = alpha * acc_ref[rows]
                m_ref[rows] = m_new
                ps.append(p.astype(jnp.bfloat16))
            acc_ref[...] += jnp.dot(jnp.concatenate(ps, axis=0), ks_ref[0, pl.ds(k0, tk), vcol],
                                    preferred_element_type=jnp.float32)

        n_full = q0 // tk

        def full_tile(kt, carry):
            sel_tile(kt, False)
            return carry

        lax.fori_loop(0, n_full, full_tile, 0)
        sel_tile(n_full, True)
        o_sel = acc_ref[...] / l_ref[...]

        w0 = pl.multiple_of(jnp.maximum(q0 - WINDOW, 0), tq)
        s = lax.dot_general(qg, kw_ref[0, pl.ds(w0, WINDOW + tq), kcol], _NT,
                            preferred_element_type=jnp.float32) * SCALE
        kpos = w0 + lax.broadcasted_iota(jnp.int32, (1, WINDOW + tq), 1)
        mask = (kpos <= qpos) & (kpos > qpos - WINDOW)
        ps = [_softmax_rows(s[hh * tq:(hh + 1) * tq], mask).astype(jnp.bfloat16) for hh in range(HPG)]
        o_win = jnp.dot(jnp.concatenate(ps, axis=0), kw_ref[0, pl.ds(w0, WINDOW + tq), vcol],
                        preferred_element_type=jnp.float32)

        for hh in range(HPG):
            head = g * HPG + hh
            rows = slice(hh * tq, (hh + 1) * tq)
            o = (gates[:, head:head + 1] * o_cmp[rows]
                 + gates[:, N_HEADS + head:N_HEADS + head + 1] * o_sel[rows]
                 + gates[:, 2 * N_HEADS + head:2 * N_HEADS + head + 1] * o_win[rows])
            o_ref[0, :, head * HEAD_DIM:(head + 1) * HEAD_DIM] = o.astype(o_ref.dtype)


def nsa_prompt(q, gates, ckv, kvs, kvw, *, tq=NSA_Q_TILE, tk=NSA_KV_TILE):
    n, t, _ = q.shape
    assert t % tk == 0 and tk % tq == 0 and t >= WINDOW + tq and t // SEL_BLOCK <= LANES
    rows = HPG * tq
    return pl.pallas_call(
        functools.partial(_nsa_prompt_kernel, tq=tq, tk=tk),
        grid=(n, t // tq),
        in_specs=[pl.BlockSpec((1, tq, Q_W), lambda b, i: (b, i, 0)),
                  pl.BlockSpec((1, tq, LANES), lambda b, i: (b, i, 0)),
                  pl.BlockSpec((1,) + ckv.shape[1:], lambda b, i: (b, 0, 0, 0, 0)),
                  pl.BlockSpec((1, t, KV_W), lambda b, i: (b, 0, 0)),
                  pl.BlockSpec((1, t, KV_W), lambda b, i: (b, 0, 0))],
        out_specs=pl.BlockSpec((1, tq, Q_W), lambda b, i: (b, i, 0)),
        out_shape=jax.ShapeDtypeStruct((n, t, Q_W), jnp.bfloat16),
        scratch_shapes=[pltpu.VMEM((rows, 1), jnp.float32), pltpu.VMEM((rows, 1), jnp.float32),
                        pltpu.VMEM((rows, HEAD_DIM), jnp.float32)],
        compiler_params=_params("parallel", "arbitrary"),
        name="nsa_prompt",
    )(q, gates, ckv, kvs, kvw)


def _conv_kernel(a_ref, b_ref, w_ref, bias_ref, g_ref, beta_ref, o_ref, win_ref, y_ref, *, tt):
    win_ref[0:tt, :] = a_ref[0]
    win_ref[tt:tt + CONV_CTX, :] = b_ref[0]
    first = CONV_CTX - (CONV_WIDTH - 1)
    rc = min(CONV_ROWS, tt)
    for cc in range(CONV_CH // LANES):
        cols = slice(cc * LANES, (cc + 1) * LANES)
        for r0 in range(0, tt, rc):
            acc = jnp.broadcast_to(bias_ref[:, cols], (rc, LANES))
            for k in range(CONV_WIDTH):
                acc = acc + w_ref[k:k + 1, cols] * win_ref[r0 + first + k:r0 + first + k + rc, cols]
            y_ref[r0:r0 + rc, cols] = acc
    y = y_ref[...]
    yc = y - jnp.mean(y, axis=-1, keepdims=True)
    yn = yc * lax.rsqrt(jnp.mean(yc * yc, axis=-1, keepdims=True) + EPS) * g_ref[...] + beta_ref[...]
    o_ref[0] = (yn * jax.nn.sigmoid(yn)).astype(o_ref.dtype)


def conv_act(ctx, up, w_dw, b_dw, ln_g, ln_b, *, block_t=128):
    n, t, ch = up.shape
    tp = -(-t // CONV_CTX) * CONV_CTX
    tt = _row_block(tp, block_t)
    padded = jnp.concatenate([jnp.zeros((n, CONV_CTX - ctx.shape[1], ch), up.dtype), ctx, up,
                              jnp.zeros((n, tp - t, ch), up.dtype)], axis=1)
    w = jnp.zeros((CONV_CTX, ch), jnp.float32).at[:CONV_WIDTH].set(w_dw)
    step = tt // CONV_CTX
    row = lambda v: v.reshape(1, ch).astype(jnp.float32)
    out = pl.pallas_call(
        functools.partial(_conv_kernel, tt=tt),
        grid=(n, tp // tt),
        in_specs=[pl.BlockSpec((1, tt, ch), lambda b, i: (b, i, 0)),
                  pl.BlockSpec((1, CONV_CTX, ch), lambda b, i: (b, (i + 1) * step, 0)),
                  pl.BlockSpec((CONV_CTX, ch), lambda b, i: (0, 0)),
                  pl.BlockSpec((1, ch), lambda b, i: (0, 0)),
                  pl.BlockSpec((1, ch), lambda b, i: (0, 0)),
                  pl.BlockSpec((1, ch), lambda b, i: (0, 0))],
        out_specs=pl.BlockSpec((1, tt, ch), lambda b, i: (b, i, 0)),
        out_shape=jax.ShapeDtypeStruct((n, tp, ch), jnp.bfloat16),
        scratch_shapes=[pltpu.VMEM((tt + CONV_CTX, ch), jnp.float32), pltpu.VMEM((tt, ch), jnp.float32)],
        compiler_params=_params("parallel", "arbitrary"),
        name="conv_act",
    )(padded, padded, w, row(b_dw), row(ln_g), row(ln_b))
    return out[:, :t]


def _cross_attn_kernel(q_ref, kv_ref, o_ref):
    outs = []
    for h in range(MEM_HEADS):
        lo = h * MEM_HEAD_DIM
        q = q_ref[0, :, lo:lo + MEM_HEAD_DIM]
        k = kv_ref[0, :, lo:lo + MEM_HEAD_DIM].astype(jnp.bfloat16)
        v = kv_ref[0, :, MEM_W + lo:MEM_W + lo + MEM_HEAD_DIM].astype(jnp.bfloat16)
        s = lax.dot_general(q, k, _NT, preferred_element_type=jnp.float32) * MEM_SCALE
        e = jnp.exp(s - jnp.max(s, axis=-1, keepdims=True))
        p = e / jnp.sum(e, axis=-1, keepdims=True)
        outs.append(jnp.dot(p.astype(jnp.bfloat16), v, preferred_element_type=jnp.float32))
    o_ref[0] = jnp.concatenate(outs, axis=-1).astype(o_ref.dtype)


def cross_attn_core(q, kv, *, block_q=512):
    n, t, _ = q.shape
    bq = _row_block(t, block_q)
    return pl.pallas_call(
        _cross_attn_kernel,
        grid=(n, t // bq),
        in_specs=[pl.BlockSpec((1, bq, MEM_W), lambda b, i: (b, i, 0)),
                  pl.BlockSpec((1, MEM_LEN, 2 * MEM_W), lambda b, i: (b, 0, 0))],
        out_specs=pl.BlockSpec((1, bq, MEM_W), lambda b, i: (b, i, 0)),
        out_shape=jax.ShapeDtypeStruct((n, t, MEM_W), jnp.bfloat16),
        compiler_params=_params("parallel", "arbitrary"),
        name="cross_attn_core",
    )(q, kv)


def _router_kernel(x_ref, g_ref, w_ref, b_ref, h_ref, e_ref, p_ref):
    x = x_ref[...]
    h = (x * lax.rsqrt(jnp.mean(x * x, axis=-1, keepdims=True) + EPS) * g_ref[...]).astype(jnp.bfloat16)
    h_ref[...] = h
    logits = jnp.dot(h, w_ref[...], preferred_element_type=jnp.float32) + b_ref[...]
    lt = logits.T[:N_EXPERTS]
    ids = lax.broadcasted_iota(jnp.int32, lt.shape, 0).astype(jnp.float32)
    vals, idxs = [], []
    for _ in range(TOP_K):
        m = jnp.max(lt, axis=0, keepdims=True)
        idx = jnp.min(jnp.where(lt == m, ids, float(N_EXPERTS)), axis=0, keepdims=True)
        vals.append(m)
        idxs.append(idx)
        lt = jnp.where(ids == idx, -jnp.inf, lt)
    top_v = jnp.concatenate(vals, axis=0)
    e = jnp.exp(top_v - vals[0])
    e_ref[...] = jnp.concatenate(idxs, axis=0).astype(jnp.int32)
    p_ref[...] = e / jnp.sum(e, axis=0, keepdims=True)


def moe_router(x, gain, w_router, b_router, *, block_m=512):
    m, d = x.shape
    bm = _row_block(m, block_m)
    w = jnp.zeros((d, LANES), jnp.bfloat16).at[:, :N_EXPERTS].set(w_router.astype(jnp.bfloat16))
    b = jnp.zeros((1, LANES), jnp.float32).at[0, :N_EXPERTS].set(b_router.astype(jnp.float32))
    return pl.pallas_call(
        _router_kernel,
        grid=(m // bm,),
        in_specs=[pl.BlockSpec((bm, d), lambda i: (i, 0)),
                  pl.BlockSpec((1, d), lambda i: (0, 0)),
                  pl.BlockSpec((d, LANES), lambda i: (0, 0)),
                  pl.BlockSpec((1, LANES), lambda i: (0, 0))],
        out_specs=[pl.BlockSpec((bm, d), lambda i: (i, 0)),
                   pl.BlockSpec((TOP_K, bm), lambda i: (0, i)),
                   pl.BlockSpec((TOP_K, bm), lambda i: (0, i))],
        out_shape=[jax.ShapeDtypeStruct((m, d), jnp.bfloat16),
                   jax.ShapeDtypeStruct((TOP_K, m), jnp.int32),
                   jax.ShapeDtypeStruct((TOP_K, m), jnp.float32)],
        compiler_params=_params("parallel"),
        name="moe_router",
    )(x, gain.reshape(1, d).astype(jnp.float32), w, b)


def _expert_kernel(blk_e_ref, blk_rows_ref, x_ref, gate_ref, wg_ref, wu_ref, bg_ref, bu_ref, wd_ref, bd_ref,
                   o_ref, wg_s, wu_s, wd_s):
    i, j = pl.program_id(0), pl.program_id(1)
    rows = blk_rows_ref[i]

    @pl.when(j == 0)
    def _():
        o_ref[...] = jnp.zeros_like(o_ref)

    @pl.when(rows > 0)
    def _():
        wg_s[...] = wg_ref[0].astype(jnp.bfloat16)
        wu_s[...] = wu_ref[0].astype(jnp.bfloat16)
        wd_s[...] = wd_ref[0].astype(jnp.bfloat16)

    for s in range(MOE_BLOCK_ROWS // MOE_SUB_ROWS):
        lo = s * MOE_SUB_ROWS

        @pl.when(lo < rows)
        def _():
            x = x_ref[lo:lo + MOE_SUB_ROWS, :]
            a = jnp.dot(x, wg_s[...], preferred_element_type=jnp.float32) + bg_ref[0]
            u = jnp.dot(x, wu_s[...], preferred_element_type=jnp.float32) + bu_ref[0]
            a = jnp.minimum(a, SWIGLU_LIMIT)
            u = jnp.clip(u, -SWIGLU_LIMIT, SWIGLU_LIMIT)
            act = a * jax.nn.sigmoid(SWIGLU_ALPHA * a) * (u + 1.0)
            o_ref[lo:lo + MOE_SUB_ROWS, :] += jnp.dot(act.astype(jnp.bfloat16), wd_s[...],
                                                      preferred_element_type=jnp.float32)

    @pl.when(j == pl.num_programs(1) - 1)
    def _():
        o_ref[...] = (o_ref[...] + bd_ref[0]) * gate_ref[...]


def moe_experts(xs, slot_gate, blk_e, blk_rows, w_gu, b_gu, w_down, b_down):
    s_total, d = xs.shape
    n_e, _, two_ff = w_gu.shape
    ff = two_ff // 2
    nb = s_total // MOE_BLOCK_ROWS
    nf = ff // MOE_FF_TILE

    def col(i, j, rows_ref):
        return jnp.where(rows_ref[i] > 0, j, nf - 1)

    grid_spec = pltpu.PrefetchScalarGridSpec(
        num_scalar_prefetch=2,
        grid=(nb, nf),
        in_specs=[
            pl.BlockSpec((MOE_BLOCK_ROWS, d), lambda i, j, e, r: (i, 0)),
            pl.BlockSpec((MOE_BLOCK_ROWS, 1), lambda i, j, e, r: (i, 0)),
            pl.BlockSpec((1, d, MOE_FF_TILE), lambda i, j, e, r: (e[i], 0, col(i, j, r))),
            pl.BlockSpec((1, d, MOE_FF_TILE), lambda i, j, e, r: (e[i], 0, nf + col(i, j, r))),
            pl.BlockSpec((1, 1, MOE_FF_TILE), lambda i, j, e, r: (e[i], 0, col(i, j, r))),
            pl.BlockSpec((1, 1, MOE_FF_TILE), lambda i, j, e, r: (e[i], 0, nf + col(i, j, r))),
            pl.BlockSpec((1, MOE_FF_TILE, d), lambda i, j, e, r: (e[i], col(i, j, r), 0)),
            pl.BlockSpec((1, 1, d), lambda i, j, e, r: (e[i], 0, 0)),
        ],
        out_specs=pl.BlockSpec((MOE_BLOCK_ROWS, d), lambda i, j, e, r: (i, 0)),
        scratch_shapes=[pltpu.VMEM((d, MOE_FF_TILE), jnp.bfloat16),
                        pltpu.VMEM((d, MOE_FF_TILE), jnp.bfloat16),
                        pltpu.VMEM((MOE_FF_TILE, d), jnp.bfloat16)],
    )
    b_gu3 = b_gu.reshape(n_e, 1, two_ff)
    return pl.pallas_call(
        _expert_kernel,
        grid_spec=grid_spec,
        out_shape=jax.ShapeDtypeStruct((s_total, d), jnp.float32),
        compiler_params=_params("arbitrary", "arbitrary"),
        name="moe_experts",
    )(blk_e, blk_rows, xs, slot_gate, w_gu, w_gu, b_gu3, b_gu3, w_down, b_down.reshape(n_e, 1, d))


def moe_layer(x_groups, gain, w_router, b_router, w_gu, b_gu, w_down, b_down):
    routed = [moe_router(x, gain, w_router, b_router) for x in x_groups]
    h = jnp.concatenate([r[0] for r in routed], axis=0)
    top_e = jnp.concatenate([r[1] for r in routed], axis=1)
    top_p = jnp.concatenate([r[2] for r in routed], axis=1)
    t = h.shape[0]
    bm = MOE_BLOCK_ROWS
    n_blocks = -(-(t * TOP_K) // bm) + N_EXPERTS
    onehot = (top_e[:, :, None] == jnp.arange(N_EXPERTS)[None, None, :]).astype(jnp.int32)
    per_tok = jnp.sum(onehot, axis=0)
    before = jnp.cumsum(per_tok, axis=0) - per_tok
    counts = jnp.sum(per_tok, axis=0)
    blocks_e = (counts + bm - 1) // bm
    blk_end = jnp.cumsum(blocks_e)
    blk_start = blk_end - blocks_e
    rank = jnp.sum(onehot * before[None], axis=-1)
    slot = blk_start[top_e] * bm + rank
    tok_ids = jnp.broadcast_to(jnp.arange(t, dtype=jnp.int32)[None], slot.shape)
    slot_tok = jnp.zeros((n_blocks * bm,), jnp.int32).at[slot.reshape(-1)].set(tok_ids.reshape(-1))
    slot_gate = jnp.zeros((n_blocks * bm,), jnp.float32).at[slot.reshape(-1)].set(top_p.reshape(-1))
    blk = jnp.arange(n_blocks)
    blk_e = jnp.minimum(jnp.searchsorted(blk_end, blk, side='right'), N_EXPERTS - 1).astype(jnp.int32)
    used = blk < blk_end[-1]
    last_e = blk_e[jnp.maximum(blk_end[-1] - 1, 0)]
    blk_e = jnp.where(used, blk_e, last_e).astype(jnp.int32)
    blk_rows = jnp.where(used, jnp.clip(counts[blk_e] - (blk - blk_start[blk_e]) * bm, 0, bm), 0).astype(jnp.int32)
    xs = h[slot_tok]
    outs = moe_experts(xs, slot_gate[:, None], blk_e, blk_rows, w_gu, b_gu, w_down, b_down)
    picked = outs[slot.T.reshape(-1)].reshape(t, TOP_K, -1)
    return jnp.sum(picked, axis=1)


def _final_kernel(x_ref, y_ref, g_ref, o_ref):
    x = x_ref[...] + y_ref[...]
    o_ref[...] = x * lax.rsqrt(jnp.mean(x * x, axis=-1, keepdims=True) + EPS) * g_ref[...]


def final_norm(x, moe_out, row_offset, gain, *, block_m=128):
    m, d = x.shape
    bm = _row_block(m, block_m)
    assert row_offset % bm == 0
    off = row_offset // bm
    return pl.pallas_call(
        _final_kernel,
        grid=(m // bm,),
        in_specs=[pl.BlockSpec((bm, d), lambda i: (i, 0)),
                  pl.BlockSpec((bm, d), lambda i: (i + off, 0)),
                  pl.BlockSpec((1, d), lambda i: (0, 0))],
        out_specs=pl.BlockSpec((bm, d), lambda i: (i, 0)),
        out_shape=jax.ShapeDtypeStruct((m, d), jnp.float32),
        compiler_params=_params("parallel"),
        name="final_norm",
    )(x, moe_out, gain.reshape(1, d).astype(jnp.float32))


def _hist_compress_kernel(pt_ref, *refs):
    pages, (w1_ref, pe_ref, w2_ref, o_ref) = refs[:CMP_PAGES + 1], refs[CMP_PAGES + 1:]
    c = CMP_PAGES * CHUNKS_PER_PAGE
    last = lax.broadcasted_iota(jnp.int32, (c, 1), 0) == c - 1

    def rows(page, base):
        return jnp.concatenate([page[0, :, i * KV_W + base:i * KV_W + base + HEAD_DIM] for i in range(CMP_STRIDE)],
                               axis=1)

    for j in range(2):
        w_a = w1_ref[j, :CMP_HALF, :]
        w_b = w1_ref[j, CMP_HALF:, :]
        pe = pe_ref[j]
        bias = (jnp.dot(pe[:, :CMP_HALF], w_a, preferred_element_type=jnp.float32)
                + jnp.dot(pe[:, CMP_HALF:], w_b, preferred_element_type=jnp.float32))[0:1]
        bases = [(j * N_KV + g) * HEAD_DIM for g in range(N_KV)]
        x = jnp.concatenate([rows(p, base) for base in bases for p in pages[:CMP_PAGES]],
                            axis=0).astype(jnp.bfloat16)
        x_next = jnp.concatenate([rows(pages[CMP_PAGES], base) for base in bases], axis=0).astype(jnp.bfloat16)
        a = jnp.dot(x, w_a, preferred_element_type=jnp.float32)
        b = jnp.dot(x, w_b, preferred_element_type=jnp.float32)
        b_tail = jnp.dot(x_next, w_b, preferred_element_type=jnp.float32)
        for g in range(N_KV):
            b_next = jnp.where(last, b_tail[g * CHUNKS_PER_PAGE:g * CHUNKS_PER_PAGE + 1],
                               pltpu.roll(b[g * c:(g + 1) * c], c - 1, axis=0))
            h = jax.nn.gelu(a[g * c:(g + 1) * c] + b_next + bias)
            o_ref[0, j, g] = jnp.dot(h.astype(jnp.bfloat16), w2_ref[j],
                                     preferred_element_type=jnp.float32).astype(o_ref.dtype)


def compress_history(pool, page_table, cmp_pe, cmp_w1, cmp_w2):
    n, n_pages = page_table.shape
    assert n_pages % CMP_PAGES == 0
    c = n_pages * CHUNKS_PER_PAGE
    w1, pe, w2 = _compress_weights(cmp_pe, cmp_w1, cmp_w2)
    chunks = pool.reshape(pool.shape[0], CHUNKS_PER_PAGE, CMP_STRIDE * KV_W)

    def page_spec(k):
        return pl.BlockSpec((1, CHUNKS_PER_PAGE, CMP_STRIDE * KV_W),
                            lambda b, s, pt: (pt[b * n_pages + jnp.minimum(s * CMP_PAGES + k, n_pages - 1)], 0, 0))

    grid_spec = pltpu.PrefetchScalarGridSpec(
        num_scalar_prefetch=1,
        grid=(n, n_pages // CMP_PAGES),
        in_specs=[page_spec(k) for k in range(CMP_PAGES + 1)] + [
            pl.BlockSpec(w1.shape, lambda b, s, pt: (0, 0, 0)),
            pl.BlockSpec(pe.shape, lambda b, s, pt: (0, 0, 0)),
            pl.BlockSpec(w2.shape, lambda b, s, pt: (0, 0, 0))],
        out_specs=pl.BlockSpec((1, 2, N_KV, CMP_PAGES * CHUNKS_PER_PAGE, HEAD_DIM), lambda b, s, pt: (b, 0, 0, s, 0)),
    )
    return pl.pallas_call(
        _hist_compress_kernel,
        grid_spec=grid_spec,
        out_shape=jax.ShapeDtypeStruct((n, 2, N_KV, c, HEAD_DIM), jnp.bfloat16),
        compiler_params=_params("parallel", "arbitrary"),
        name="compress_history",
    )(page_table.reshape(-1), *([chunks] * (CMP_PAGES + 1)), w1, pe, w2)


def _nsa_sample_dense_kernel(q_ref, ckv_ref, kw_ref, ocmp_ref, owin_ref, idx_ref, val_ref, *, past, n_cmp, n_sel):
    rows = HPG * DEC_PAD
    tok = lax.bitwise_and(lax.broadcasted_iota(jnp.int32, (rows, 1), 0), DEC_PAD - 1)
    qpos = past + tok
    qpos_l = past + lax.broadcasted_iota(jnp.int32, (1, LANES), 1)
    c_all = ckv_ref.shape[3]
    n_sel_pad = -(-n_sel // 8) * 8
    wlen = kw_ref.shape[1]
    for g in range(N_KV):
        qg = q_ref[0, g]
        s = lax.dot_general(qg, ckv_ref[0, 0, g], _NT, preferred_element_type=jnp.float32) * SCALE
        cidx = lax.broadcasted_iota(jnp.int32, (1, c_all), 1)
        vis = (cidx * CMP_STRIDE + (CMP_LEN - 1) <= qpos) & (cidx < n_cmp)
        p = _softmax_rows(s, vis)
        ocmp_ref[0, g] = jnp.dot(p.astype(jnp.bfloat16), ckv_ref[0, 1, g], preferred_element_type=jnp.float32)
        p_grp = p[0:DEC_PAD]
        for hh in range(1, HPG):
            p_grp = p_grp + p[hh * DEC_PAD:(hh + 1) * DEC_PAD]
        p_grp = jnp.concatenate([p_grp, jnp.zeros((LANES - DEC_PAD, c_all), jnp.float32)], axis=0)
        c0 = lax.broadcasted_iota(jnp.int32, (n_sel_pad, c_all), 1) * CMP_STRIDE
        s0 = lax.broadcasted_iota(jnp.int32, (n_sel_pad, c_all), 0) * SEL_BLOCK
        overlap_t = ((c0 < s0 + SEL_BLOCK) & (c0 + CMP_LEN > s0)).astype(jnp.bfloat16)
        score_t = lax.dot_general(overlap_t, p_grp.astype(jnp.bfloat16), _NT, preferred_element_type=jnp.float32)
        blk = lax.broadcasted_iota(jnp.int32, score_t.shape, 0)
        cur = lax.shift_right_logical(qpos_l, SEL_BLOCK.bit_length() - 1)
        avail = (blk <= cur) & (blk < n_sel)
        forced = avail & ((blk == 0) | (blk == cur) | (blk == cur - 1))
        work = jnp.where(forced, jnp.inf, jnp.where(avail, score_t, -jnp.inf))
        blk_f = blk.astype(jnp.float32)
        idxs, vals = [], []
        for _ in range(SEL_TOPK):
            m = jnp.max(work, axis=0, keepdims=True)
            first = jnp.min(jnp.where(work == m, blk_f, float(n_sel_pad)), axis=0, keepdims=True)
            idxs.append(first)
            vals.append(jnp.where(m > -jnp.inf, 1.0, 0.0))
            work = jnp.where(blk_f == first, -jnp.inf, work)
        idx_ref[0, g] = jnp.concatenate(idxs, axis=0).astype(jnp.int32)
        val_ref[0, g] = jnp.concatenate(vals, axis=0).astype(jnp.int32)

        kcol = slice(g * HEAD_DIM, (g + 1) * HEAD_DIM)
        vcol = slice((N_KV + g) * HEAD_DIM, (N_KV + g + 1) * HEAD_DIM)
        s = lax.dot_general(qg, kw_ref[0, :, kcol].astype(jnp.bfloat16), _NT,
                            preferred_element_type=jnp.float32) * SCALE
        kpos = past - WINDOW + lax.broadcasted_iota(jnp.int32, (1, wlen), 1)
        p = _softmax_rows(s, (kpos <= qpos) & (kpos > qpos - WINDOW))
        owin_ref[0, g] = jnp.dot(p.astype(jnp.bfloat16), kw_ref[0, :, vcol].astype(jnp.bfloat16),
                                 preferred_element_type=jnp.float32)


def nsa_sample_dense(qd, ckv, kw, *, past, n_cmp, n_sel):
    n = qd.shape[0]
    rows = HPG * DEC_PAD
    o_shape = jax.ShapeDtypeStruct((n, N_KV, rows, HEAD_DIM), jnp.float32)
    i_shape = jax.ShapeDtypeStruct((n, N_KV, SEL_TOPK, LANES), jnp.int32)
    o_spec = pl.BlockSpec((1, N_KV, rows, HEAD_DIM), lambda b: (b, 0, 0, 0))
    i_spec = pl.BlockSpec((1, N_KV, SEL_TOPK, LANES), lambda b: (b, 0, 0, 0))
    return pl.pallas_call(
        functools.partial(_nsa_sample_dense_kernel, past=past, n_cmp=n_cmp, n_sel=n_sel),
        grid=(n,),
        in_specs=[pl.BlockSpec((1, N_KV, rows, HEAD_DIM), lambda b: (b, 0, 0, 0)),
                  pl.BlockSpec((1,) + ckv.shape[1:], lambda b: (b, 0, 0, 0, 0)),
                  pl.BlockSpec((1,) + kw.shape[1:], lambda b: (b, 0, 0))],
        out_specs=[o_spec, o_spec, i_spec, i_spec],
        out_shape=[o_shape, o_shape, i_shape, i_shape],
        compiler_params=_params("parallel"),
        name="nsa_sample_dense",
    )(qd, ckv, kw)


def _nsa_sample_sel_kernel(idx_ref, val_ref, pt_ref, q_ref, new_ref, *refs, tq, past, n_past_blk):
    blocks, o_ref = refs[:N_KV * SEL_TOPK], refs[N_KV * SEL_TOPK]
    b, t = pl.program_id(0), pl.program_id(1)
    qpos = past + t
    lane = lax.broadcasted_iota(jnp.int32, (1, SEL_TOPK * SEL_BLOCK), 1)
    slot_of_lane = lax.shift_right_logical(lane, SEL_BLOCK.bit_length() - 1)
    for g in range(N_KV):
        base = ((b * tq + t) * N_KV + g) * SEL_TOPK
        kcol = slice(g * HEAD_DIM, (g + 1) * HEAD_DIM)
        vcol = slice((N_KV + g) * HEAD_DIM, (N_KV + g + 1) * HEAD_DIM)
        ks, vs = [], []
        tokpos = jnp.zeros(lane.shape, jnp.int32)
        ok = jnp.zeros(lane.shape, jnp.int32)
        for s in range(SEL_TOPK):
            blk = idx_ref[base + s]
            from_pool = blk < n_past_blk
            kv = jnp.where(from_pool, blocks[g * SEL_TOPK + s][0], new_ref[0])
            ks.append(kv[:, kcol])
            vs.append(kv[:, vcol])
            here = slot_of_lane == s
            tokpos = jnp.where(here, blk * SEL_BLOCK, tokpos)
            ok = jnp.where(here, val_ref[base + s], ok)
        tokpos = tokpos + lax.bitwise_and(lane, SEL_BLOCK - 1)
        mask = (ok > 0) & (tokpos <= qpos)
        k = jnp.concatenate(ks, axis=0).astype(jnp.bfloat16)
        v = jnp.concatenate(vs, axis=0).astype(jnp.bfloat16)
        s = lax.dot_general(q_ref[0, 0, g], k, _NT, preferred_element_type=jnp.float32) * SCALE
        p = _softmax_rows(s, mask)
        o_ref[0, 0, g] = jnp.dot(p.astype(jnp.bfloat16), v, preferred_element_type=jnp.float32)


def nsa_sample_sel(qs, idx, valid, page_table, pool, new_blocks, *, past):
    n, tq = qs.shape[:2]
    n_pages = page_table.shape[1]
    bpp = PAGE_SIZE // SEL_BLOCK
    n_past_blk = n_pages * bpp
    halves = pool.reshape(pool.shape[0] * bpp, SEL_BLOCK, KV_W)

    def block_spec(g, s):
        def index(b, t, idx_ref, val_ref, pt_ref):
            blk = jnp.minimum(idx_ref[((b * tq + t) * N_KV + g) * SEL_TOPK + s], n_past_blk - 1)
            return (pt_ref[b * n_pages + blk // bpp] * bpp + blk % bpp, 0, 0)
        return pl.BlockSpec((1, SEL_BLOCK, KV_W), index)

    grid_spec = pltpu.PrefetchScalarGridSpec(
        num_scalar_prefetch=3,
        grid=(n, tq),
        in_specs=[pl.BlockSpec((1, 1, N_KV, 16, HEAD_DIM), lambda b, t, *_: (b, t, 0, 0, 0)),
                  pl.BlockSpec((1, SEL_BLOCK, KV_W), lambda b, t, *_: (b, 0, 0))]
                 + [block_spec(g, s) for g in range(N_KV) for s in range(SEL_TOPK)],
        out_specs=pl.BlockSpec((1, 1, N_KV, 16, HEAD_DIM), lambda b, t, *_: (b, t, 0, 0, 0)),
    )
    return pl.pallas_call(
        functools.partial(_nsa_sample_sel_kernel, tq=tq, past=past, n_past_blk=n_past_blk),
        grid_spec=grid_spec,
        out_shape=jax.ShapeDtypeStruct((n, tq, N_KV, 16, HEAD_DIM), jnp.float32),
        compiler_params=_params("parallel", "arbitrary"),
        name="nsa_sample_sel",
    )(idx.reshape(-1), valid.reshape(-1), page_table.reshape(-1), qs, new_blocks,
      *([halves] * (N_KV * SEL_TOPK)))


def nsa_sample(q, gates, kvs_new, kvw_new, pool_cmp, pool_sel, win_cache, page_table, cmp_pe, cmp_w1, cmp_w2):
    n, tq, _ = q.shape
    past = page_table.shape[1] * PAGE_SIZE
    assert tq <= DEC_PAD and tq <= SEL_BLOCK and win_cache.shape[1] == WINDOW
    assert (past + tq) // CMP_STRIDE * CMP_STRIDE == past
    n_cmp = past // CMP_STRIDE - 1
    n_sel = -(-(past + tq) // SEL_BLOCK)
    ckv = compress_history(pool_cmp, page_table, cmp_pe, cmp_w1, cmp_w2)
    q5 = q.reshape(n, tq, N_KV, HPG, HEAD_DIM)
    qd = jnp.pad(q5.transpose(0, 2, 3, 1, 4), ((0, 0), (0, 0), (0, 0), (0, DEC_PAD - tq), (0, 0)))
    qd = qd.reshape(n, N_KV, HPG * DEC_PAD, HEAD_DIM)
    kw_all = jnp.concatenate([win_cache, kvw_new], axis=1)
    kw = jnp.pad(kw_all, ((0, 0), (0, -(-(WINDOW + DEC_PAD) // LANES) * LANES - WINDOW - tq), (0, 0)))
    o_cmp, o_win, idx, valid = nsa_sample_dense(qd, ckv, kw, past=past, n_cmp=n_cmp, n_sel=n_sel)
    idx = idx[..., :tq].transpose(0, 3, 1, 2)
    valid = valid[..., :tq].transpose(0, 3, 1, 2)
    qs = jnp.pad(q5, ((0, 0), (0, 0), (0, 0), (0, 16 - HPG), (0, 0)))
    new_blocks = jnp.pad(kvs_new, ((0, 0), (0, SEL_BLOCK - tq), (0, 0)))
    o_sel = nsa_sample_sel(qs, idx, valid, page_table, pool_sel, new_blocks, past=past)[:, :, :, :HPG]
    unpack = lambda o: o.reshape(n, N_KV, HPG, DEC_PAD, HEAD_DIM)[:, :, :, :tq].transpose(0, 3, 1, 2, 4)
    g = gates[..., :GATE_W].reshape(n, tq, 3, N_KV, HPG)[..., None]
    o = g[:, :, 0] * unpack(o_cmp) + g[:, :, 1] * o_sel + g[:, :, 2] * unpack(o_win)
    return o.reshape(n, tq, Q_W), kw_all


def prepare_weights(w_in, w_nsa_o, w_conv_o, w_out, w_xq, w_xkv, w_xo):
    bf = jnp.bfloat16
    w = w_in.astype(bf)
    d = w.shape[0]
    s = IN_SPLITS
    return dict(
        q=w[:, :s[0]], kv=w[:, s[0]:s[3]],
        gates=jnp.zeros((d, LANES), bf).at[:, :GATE_W].set(w[:, s[3]:s[4]]),
        glu_a=w[:, s[4]:s[4] + CONV_CH], glu_b=w[:, s[4] + CONV_CH:s[5]], merge=w[:, s[5]:],
        nsa_o=w_nsa_o.astype(bf), conv_o=w_conv_o.astype(bf), out=w_out.astype(bf),
        xq=w_xq.astype(bf), xkv=w_xkv.astype(bf), xo=w_xo.astype(bf))


def front(x2, g_mix, wts):
    q = fused_linear(x2, [wts['q']], gain=g_mix, out_dtype=jnp.bfloat16, name="front_q")
    kv = fused_linear(x2, [wts['kv']], gain=g_mix, name="front_kv")
    gates = fused_linear(x2, [wts['gates']], gain=g_mix, epilogue=_sigmoid_epilogue, name="front_gates")
    up = fused_linear(x2, [wts['glu_a'], wts['glu_b']], gain=g_mix, epilogue=_glu_epilogue, name="front_glu")
    mg = fused_linear(x2, [wts['merge']], gain=g_mix, epilogue=_sigmoid_epilogue, name="front_merge")
    return q, kv, gates, up, mg


def mixer_tail(x2, o, act, mg, wts):
    merged = merge_branches(o, act, wts['nsa_o'], wts['conv_o'], mg)
    return fused_linear(merged, [wts['out']], tiles=[x2], epilogue=_residual_epilogue, name="mixer_out_proj")


def cross_attn_block(x2, n, mkv, g_xattn, wts):
    m = x2.shape[0]
    qx = fused_linear(x2, [wts['xq']], gain=g_xattn, out_dtype=jnp.bfloat16, name="xattn_q")
    oc = cross_attn_core(qx.reshape(n, m // n, MEM_W), mkv)
    return fused_linear(oc.reshape(m, MEM_W), [wts['xo']], tiles=[x2], epilogue=_residual_epilogue, name="xattn_o")


def kernel(x_prompt, x_sample, cache_kv_cmp, cache_kv_sel, cache_kv_win, cache_conv, cache_mem_kv, page_table,
           mem_prompt, g_mix, w_in, cmp_pe, cmp_w1, cmp_w2, w_nsa_o, w_dw, b_dw, ln_conv_g, ln_conv_b, w_conv_o,
           w_out, g_xattn, g_mem, w_xq, w_xkv, w_xo, g_moe, w_router, b_router, w_gu, b_gu, w_down, b_down, g_final):
    nb, t, d = x_prompt.shape
    nd, tq = x_sample.shape[:2]
    l = 0
    bf = jnp.bfloat16
    wts = prepare_weights(w_in[l], w_nsa_o[l], w_conv_o[l], w_out[l], w_xq[l], w_xkv[l], w_xo[l])
    kv_shape = lambda n_, t_: (n_, t_, 2, N_KV, HEAD_DIM)

    xp2 = x_prompt.reshape(nb * t, d)
    q, kv, gates, up_new, mg = front(xp2, g_mix[l], wts)
    kvc, kvs, kvw = (kv[:, i * KV_W:(i + 1) * KV_W].reshape(nb, t, KV_W) for i in range(3))
    ckv = compress_prompt(kvc, cmp_pe[l], cmp_w1[l], cmp_w2[l])
    o = nsa_prompt(q.reshape(nb, t, Q_W), gates.reshape(nb, t, LANES), ckv, kvs.astype(bf), kvw.astype(bf))
    up_new = up_new.reshape(nb, t, CONV_CH)
    act = conv_act(jnp.zeros((nb, CONV_WIDTH - 1, CONV_CH), jnp.float32), up_new, w_dw[l], b_dw[l],
                   ln_conv_g[l], ln_conv_b[l])
    xp = mixer_tail(xp2, o.reshape(nb * t, Q_W), act.reshape(nb * t, CONV_CH), mg, wts)
    mkv = fused_linear(mem_prompt.reshape(nb * MEM_LEN, d), [wts['xkv']], gain=g_mem[l], name="mem_kv")
    xp = cross_attn_block(xp, nb, mkv.reshape(nb, MEM_LEN, 2 * MEM_W), g_xattn[l], wts)
    p_kv_cmp, p_kv_sel = kvc.reshape(kv_shape(nb, t))[None], kvs.reshape(kv_shape(nb, t))[None]
    p_kv_win = kvw.reshape(kv_shape(nb, t))[:, t - min(WINDOW, t):][None]
    p_conv = jnp.pad(up_new, ((0, 0), (CONV_WIDTH - 1, 0), (0, 0)))[:, t:][None]
    p_mem = mkv.reshape(nb, MEM_LEN, 2, MEM_HEADS, MEM_HEAD_DIM)[None]

    xs2 = x_sample.reshape(nd * tq, d)
    q, kv, gates, up_new, mg = front(xs2, g_mix[l], wts)
    kvc, kvs, kvw = (kv[:, i * KV_W:(i + 1) * KV_W].reshape(nd, tq, KV_W) for i in range(3))
    pool_shape = (-1, PAGE_SIZE, KV_W)
    o, win_all = nsa_sample(q.reshape(nd, tq, Q_W), gates.reshape(nd, tq, LANES), kvs, kvw,
                            cache_kv_cmp[l].reshape(pool_shape), cache_kv_sel[l].reshape(pool_shape),
                            cache_kv_win[l].reshape(nd, -1, KV_W), page_table, cmp_pe[l], cmp_w1[l], cmp_w2[l])
    win_new = win_all[:, -min(WINDOW, win_all.shape[1]):].reshape(kv_shape(nd, min(WINDOW, win_all.shape[1])))
    kvc, kvs = kvc.reshape(kv_shape(nd, tq)), kvs.reshape(kv_shape(nd, tq))
    up_new = up_new.reshape(nd, tq, CONV_CH)
    act = conv_act(cache_conv[l], up_new, w_dw[l], b_dw[l], ln_conv_g[l], ln_conv_b[l])
    xs = mixer_tail(xs2, o.reshape(nd * tq, Q_W).astype(bf), act.reshape(nd * tq, CONV_CH), mg, wts)
    xs = cross_attn_block(xs, nd, cache_mem_kv[l].reshape(nd, MEM_LEN, 2 * MEM_W), g_xattn[l], wts)
    s_conv = jnp.concatenate([cache_conv[l], up_new], axis=1)[:, tq:][None]

    moe_out = moe_layer([xp, xs], g_moe[l], w_router[l], b_router[l], w_gu[l], b_gu[l], w_down[l], b_down[l])
    y_prompt = final_norm(xp, moe_out, 0, g_final).reshape(nb, t, d)
    y_sample = final_norm(xs, moe_out, nb * t, g_final).reshape(nd, tq, d)
    return (y_prompt, y_sample, p_kv_cmp, p_kv_sel, p_kv_win, p_conv, p_mem, kvc[None], kvs[None], win_new[None],
            s_conv)
```

```python
import functools

import jax
import jax.numpy as jnp
from jax import lax
from jax.experimental import pallas as pl
from jax.experimental.pallas import tpu as pltpu

D_MODEL = 2048
PAGE_SIZE = 128

N_HEADS = 16
HEAD_DIM = 128
N_KV = 2
HPG = N_HEADS // N_KV
CMP_STRIDE = 16
CMP_LEN = 2 * CMP_STRIDE
CMP_HALF = CMP_STRIDE * HEAD_DIM
SEL_BLOCK = 64
SEL_TOPK = 16
WINDOW = 512
SCALE = HEAD_DIM ** -0.5
CONV_CH = D_MODEL // 2
CONV_WIDTH = 31
MEM_LEN = 256
MEM_HEADS = 4
MEM_HEAD_DIM = 128
MEM_W = MEM_HEADS * MEM_HEAD_DIM
MEM_SCALE = MEM_HEAD_DIM ** -0.5
N_EXPERTS = 32
TOP_K = 4
SWIGLU_LIMIT = 7.0
SWIGLU_ALPHA = 1.702
EPS = 1e-6
Q_W = N_HEADS * HEAD_DIM
KV_W = 2 * N_KV * HEAD_DIM
KV_ROWS = KV_W // HEAD_DIM
GATE_W = 3 * N_HEADS
GLU_W = 2 * CONV_CH
MERGE_W = 2 * D_MODEL
IN_SPLITS = [Q_W, Q_W + KV_W, Q_W + 2 * KV_W, Q_W + 3 * KV_W, Q_W + 3 * KV_W + GATE_W, Q_W + 3 * KV_W + GATE_W + GLU_W]

V7X_VMEM_BYTES = 64 * 1024 * 1024
VMEM_LIMIT_BYTES = V7X_VMEM_BYTES - 8 * 1024 * 1024
LANES = 128

MOE_BLOCK_ROWS = 1024
MOE_SUB_ROWS = 256
MOE_FF_TILE = 256
NSA_Q_TILE = 128
NSA_KV_TILE = 512
CONV_CTX = 32
CONV_ROWS = 64
CHUNKS_PER_PAGE = PAGE_SIZE // CMP_STRIDE
CMP_PAGES = 32
DEC_PAD = 8

NEG = -1e30
_NT = (((1,), (1,)), ((), ()))


def _params(*semantics):
    return pltpu.CompilerParams(dimension_semantics=semantics, vmem_limit_bytes=VMEM_LIMIT_BYTES)


def _row_block(rows, target):
    if rows <= target:
        return rows
    for b in range(target, 15, -1):
        if rows % b == 0 and b % 16 == 0:
            return b
    return rows


def _linear_kernel(*refs, n_w, norm, n_tile, n_row, epilogue):
    x_ref = refs[0]
    k = 1
    g_ref = refs[k] if norm else None
    k += int(norm)
    w_refs = refs[k:k + n_w]
    k += n_w
    tile_refs = refs[k:k + n_tile]
    k += n_tile
    row_refs = refs[k:k + n_row]
    k += n_row
    o_ref, h_ref = refs[k], refs[k + 1]

    @pl.when(pl.program_id(1) == 0)
    def _():
        x = x_ref[...].astype(jnp.float32)
        if norm:
            x = x * lax.rsqrt(jnp.mean(x * x, axis=-1, keepdims=True) + EPS) * g_ref[...]
        h_ref[...] = x.astype(jnp.bfloat16)

    h = h_ref[...]
    zs = [jnp.dot(h, w[...], preferred_element_type=jnp.float32) for w in w_refs]
    o_ref[...] = epilogue(zs, [t[...] for t in tile_refs], [r[...] for r in row_refs]).astype(o_ref.dtype)


def fused_linear(x, ws, *, gain=None, tiles=(), rows=(), epilogue=None, out_dtype=jnp.float32,
                 block_m=1024, block_n=512, name="fused_linear"):
    m, k = x.shape
    n = ws[0].shape[1]
    bm = _row_block(m, block_m)
    bn = min(block_n, n)
    assert m % bm == 0 and n % bn == 0 and all(w.shape == (k, n) for w in ws)
    if epilogue is None:
        epilogue = _first_epilogue
    norm = gain is not None
    in_specs = [pl.BlockSpec((bm, k), lambda i, j: (i, 0))]
    args = [x]
    if norm:
        in_specs.append(pl.BlockSpec((1, k), lambda i, j: (0, 0)))
        args.append(gain.reshape(1, k).astype(jnp.float32))
    in_specs += [pl.BlockSpec((k, bn), lambda i, j: (0, j)) for _ in ws]
    args += list(ws)
    in_specs += [pl.BlockSpec((bm, bn), lambda i, j: (i, j)) for _ in tiles]
    args += list(tiles)
    in_specs += [pl.BlockSpec((1, bn), lambda i, j: (0, j)) for _ in rows]
    args += list(rows)
    return pl.pallas_call(
        functools.partial(_linear_kernel, n_w=len(ws), norm=norm, n_tile=len(tiles), n_row=len(rows),
                          epilogue=epilogue),
        grid=(m // bm, n // bn),
        in_specs=in_specs,
        out_specs=pl.BlockSpec((bm, bn), lambda i, j: (i, j)),
        out_shape=jax.ShapeDtypeStruct((m, n), out_dtype),
        scratch_shapes=[pltpu.VMEM((bm, k), jnp.bfloat16)],
        compiler_params=_params("parallel", "arbitrary"),
        name=name,
    )(*args)


def _first_epilogue(zs, ts, rs):
    return zs[0]


def _glu_epilogue(zs, ts, rs):
    return zs[0] * jax.nn.sigmoid(zs[1])


def _sigmoid_epilogue(zs, ts, rs):
    return jax.nn.sigmoid(zs[0])


def _residual_epilogue(zs, ts, rs):
    return ts[0] + zs[0]


def _merge_kernel(o_ref, a_ref, wo_ref, wc_ref, ga_ref, gc_ref, out_ref):
    y_att = jnp.dot(o_ref[...], wo_ref[...], preferred_element_type=jnp.float32)
    y_conv = jnp.dot(a_ref[...], wc_ref[...], preferred_element_type=jnp.float32)
    out_ref[...] = (ga_ref[...] * y_att + gc_ref[...] * y_conv).astype(out_ref.dtype)


def merge_branches(o, act, w_nsa_o, w_conv_o, merge_gates, *, block_m=1024, block_n=512):
    m = o.shape[0]
    d = w_nsa_o.shape[1]
    bm = _row_block(m, block_m)
    bn = min(block_n, d)
    nj = d // bn
    return pl.pallas_call(
        _merge_kernel,
        grid=(m // bm, nj),
        in_specs=[pl.BlockSpec((bm, o.shape[1]), lambda i, j: (i, 0)),
                  pl.BlockSpec((bm, act.shape[1]), lambda i, j: (i, 0)),
                  pl.BlockSpec((w_nsa_o.shape[0], bn), lambda i, j: (0, j)),
                  pl.BlockSpec((w_conv_o.shape[0], bn), lambda i, j: (0, j)),
                  pl.BlockSpec((bm, bn), lambda i, j: (i, j)),
                  pl.BlockSpec((bm, bn), lambda i, j: (i, j + nj))],
        out_specs=pl.BlockSpec((bm, bn), lambda i, j: (i, j)),
        out_shape=jax.ShapeDtypeStruct((m, d), jnp.bfloat16),
        compiler_params=_params("parallel", "arbitrary"),
        name="merge_branches",
    )(o, act, w_nsa_o, w_conv_o, merge_gates, merge_gates)


def _compress_kernel(x_ref, w1_ref, pe_ref, w2_ref, o_ref):
    c = x_ref.shape[1]
    for j in range(2):
        w_a = w1_ref[j, :CMP_HALF, :]
        w_b = w1_ref[j, CMP_HALF:, :]
        pe = pe_ref[j]
        bias = (jnp.dot(pe[:, :CMP_HALF], w_a, preferred_element_type=jnp.float32)
                + jnp.dot(pe[:, CMP_HALF:], w_b, preferred_element_type=jnp.float32))[0:1]
        xs = []
        for g in range(N_KV):
            base = (j * N_KV + g) * HEAD_DIM
            xs.append(jnp.concatenate(
                [x_ref[0, :, i * KV_W + base:i * KV_W + base + HEAD_DIM] for i in range(CMP_STRIDE)], axis=1))
        x = jnp.concatenate(xs, axis=0).astype(jnp.bfloat16)
        a = jnp.dot(x, w_a, preferred_element_type=jnp.float32)
        b = jnp.dot(x, w_b, preferred_element_type=jnp.float32)
        for g in range(N_KV):
            b_next = pltpu.roll(b[g * c:(g + 1) * c], c - 1, axis=0)
            h = jax.nn.gelu(a[g * c:(g + 1) * c] + b_next + bias)
            o_ref[0, j, g] = jnp.dot(h.astype(jnp.bfloat16), w2_ref[j],
                                     preferred_element_type=jnp.float32).astype(o_ref.dtype)


def _compress_weights(cmp_pe, cmp_w1, cmp_w2):
    bf = jnp.bfloat16
    w1 = cmp_w1.reshape(2, CMP_LEN * HEAD_DIM, HEAD_DIM).astype(bf)
    pe = jnp.broadcast_to(cmp_pe.reshape(2, 1, CMP_LEN * HEAD_DIM), (2, 8, CMP_LEN * HEAD_DIM)).astype(bf)
    return w1, pe, cmp_w2.astype(bf)


def compress_prompt(kvc, cmp_pe, cmp_w1, cmp_w2):
    n, t, _ = kvc.shape
    c = t // CMP_STRIDE
    w1, pe, w2 = _compress_weights(cmp_pe, cmp_w1, cmp_w2)
    return pl.pallas_call(
        _compress_kernel,
        grid=(n,),
        in_specs=[pl.BlockSpec((1, c, CMP_STRIDE * KV_W), lambda b: (b, 0, 0)),
                  pl.BlockSpec(w1.shape, lambda b: (0, 0, 0)),
                  pl.BlockSpec(pe.shape, lambda b: (0, 0, 0)),
                  pl.BlockSpec(w2.shape, lambda b: (0, 0, 0))],
        out_specs=pl.BlockSpec((1, 2, N_KV, c, HEAD_DIM), lambda b: (b, 0, 0, 0, 0)),
        out_shape=jax.ShapeDtypeStruct((n, 2, N_KV, c, HEAD_DIM), jnp.bfloat16),
        compiler_params=_params("parallel"),
        name="compress_prompt",
    )(kvc.reshape(n, c, CMP_STRIDE * KV_W), w1, pe, w2)


def _softmax_rows(s, mask):
    sm = jnp.where(mask, s, NEG)
    e = jnp.where(mask, jnp.exp(sm - jnp.max(sm, axis=-1, keepdims=True)), 0.0)
    return e / jnp.maximum(jnp.sum(e, axis=-1, keepdims=True), 1e-30)


def _select_blocks_t(score_t, qpos_l):
    blk = lax.broadcasted_iota(jnp.int32, score_t.shape, 0)
    cur = lax.shift_right_logical(qpos_l, SEL_BLOCK.bit_length() - 1)
    avail = blk <= cur
    forced = avail & ((blk == 0) | (blk == cur) | (blk == cur - 1))
    work = jnp.where(forced, jnp.inf, jnp.where(avail, score_t, -jnp.inf))
    blk_f = blk.astype(jnp.float32)
    sel = jnp.zeros(score_t.shape, jnp.float32)
    for _ in range(SEL_TOPK):
        m = jnp.max(work, axis=0, keepdims=True)
        first = jnp.min(jnp.where(work == m, blk_f, float(score_t.shape[0])), axis=0, keepdims=True)
        hit = blk_f == first
        sel = jnp.where(hit & (m > -jnp.inf), 1.0, sel)
        work = jnp.where(hit, -jnp.inf, work)
    return sel


def _nsa_prompt_kernel(q_ref, gate_ref, ckv_ref, ks_ref, kw_ref, o_ref, m_ref, l_ref, acc_ref, *, tq, tk):
    b = pl.program_id(1)
    q0 = b * tq
    qpos = q0 + lax.broadcasted_iota(jnp.int32, (tq, 1), 0)
    qpos_l = q0 + lax.broadcasted_iota(jnp.int32, (1, tq), 1)
    n_cmp = ckv_ref.shape[3]
    gates = gate_ref[0]
    blocks_per_tile = tk // SEL_BLOCK

    for g in range(N_KV):
        kcol = slice(g * HEAD_DIM, (g + 1) * HEAD_DIM)
        vcol = slice((N_KV + g) * HEAD_DIM, (N_KV + g + 1) * HEAD_DIM)
        qg = jnp.concatenate([q_ref[0, :, (g * HPG + hh) * HEAD_DIM:(g * HPG + hh + 1) * HEAD_DIM]
                              for hh in range(HPG)], axis=0)

        s = lax.dot_general(qg, ckv_ref[0, 0, g], _NT, preferred_element_type=jnp.float32) * SCALE
        cend = lax.broadcasted_iota(jnp.int32, (1, n_cmp), 1) * CMP_STRIDE + (CMP_LEN - 1)
        vis = cend <= qpos
        ps = [_softmax_rows(s[hh * tq:(hh + 1) * tq], vis) for hh in range(HPG)]
        o_cmp = jnp.dot(jnp.concatenate(ps, axis=0).astype(jnp.bfloat16), ckv_ref[0, 1, g],
                        preferred_element_type=jnp.float32)
        p_grp = ps[0]
        for p in ps[1:]:
            p_grp = p_grp + p

        c0 = lax.broadcasted_iota(jnp.int32, (LANES, n_cmp), 1) * CMP_STRIDE
        s0 = lax.broadcasted_iota(jnp.int32, (LANES, n_cmp), 0) * SEL_BLOCK
        overlap_t = ((c0 < s0 + SEL_BLOCK) & (c0 + CMP_LEN > s0)).astype(jnp.bfloat16)
        score_t = lax.dot_general(overlap_t, p_grp.astype(jnp.bfloat16), _NT, preferred_element_type=jnp.float32)
        sel = _select_blocks_t(score_t, qpos_l).T.astype(jnp.bfloat16)

        m_ref[...] = jnp.full(m_ref.shape, NEG, jnp.float32)
        l_ref[...] = jnp.zeros(l_ref.shape, jnp.float32)
        acc_ref[...] = jnp.zeros(acc_ref.shape, jnp.float32)

        def sel_tile(kt, causal):
            k0 = pl.multiple_of(kt * tk, tk)
            s = lax.dot_general(qg, ks_ref[0, pl.ds(k0, tk), kcol], _NT, preferred_element_type=jnp.float32) * SCALE
            jrow = lax.broadcasted_iota(jnp.int32, (LANES, tk), 0)
            jcol = lax.shift_right_logical(lax.broadcasted_iota(jnp.int32, (LANES, tk), 1),
                                           SEL_BLOCK.bit_length() - 1)
            expand = (jrow == kt * blocks_per_tile + jcol).astype(jnp.bfloat16)
            mask = jnp.dot(sel, expand, preferred_element_type=jnp.float32) > 0.5
            if causal:
                mask = mask & (k0 + lax.broadcasted_iota(jnp.int32, (1, tk), 1) <= qpos)
            ps = []
            for hh in range(HPG):
                rows = slice(hh * tq, (hh + 1) * tq)
                sh = jnp.where(mask, s[rows], NEG)
                m_old = m_ref[rows]
                m_new = jnp.maximum(m_old, jnp.max(sh, axis=-1, keepdims=True))
                p = jnp.where(mask, jnp.exp(sh - m_new), 0.0)
                alpha = jnp.exp(m_old - m_new)
                l_ref[rows] = alpha * l_ref[rows] + jnp.sum(p, axis=-1, keepdims=True)
                acc_ref[rows] = alpha * acc_ref[rows]
                m_ref[rows] = m_new
                ps.append(p.astype(jnp.bfloat16))
            acc_ref[...] += jnp.dot(jnp.concatenate(ps, axis=0), ks_ref[0, pl.ds(k0, tk), vcol],
                                    preferred_element_type=jnp.float32)

        n_full = q0 // tk

        def full_tile(kt, carry):
            sel_tile(kt, False)
            return carry

        lax.fori_loop(0, n_full, full_tile, 0)
        sel_tile(n_full, True)
        o_sel = acc_ref[...] / l_ref[...]

        w0 = pl.multiple_of(jnp.maximum(q0 - WINDOW, 0), tq)
        s = lax.dot_general(qg, kw_ref[0, pl.ds(w0, WINDOW + tq), kcol], _NT,
                            preferred_element_type=jnp.float32) * SCALE
        kpos = w0 + lax.broadcasted_iota(jnp.int32, (1, WINDOW + tq), 1)
        mask = (kpos <= qpos) & (kpos > qpos - WINDOW)
        ps = [_softmax_rows(s[hh * tq:(hh + 1) * tq], mask).astype(jnp.bfloat16) for hh in range(HPG)]
        o_win = jnp.dot(jnp.concatenate(ps, axis=0), kw_ref[0, pl.ds(w0, WINDOW + tq), vcol],
                        preferred_element_type=jnp.float32)

        for hh in range(HPG):
            head = g * HPG + hh
            rows = slice(hh * tq, (hh + 1) * tq)
            o = (gates[:, head:head + 1] * o_cmp[rows]
                 + gates[:, N_HEADS + head:N_HEADS + head + 1] * o_sel[rows]
                 + gates[:, 2 * N_HEADS + head:2 * N_HEADS + head + 1] * o_win[rows])
            o_ref[0, :, head * HEAD_DIM:(head + 1) * HEAD_DIM] = o.astype(o_ref.dtype)


def nsa_prompt(q, gates, ckv, kvs, kvw, *, tq=NSA_Q_TILE, tk=NSA_KV_TILE):
    n, t, _ = q.shape
    assert t % tk == 0 and tk % tq == 0 and t >= WINDOW + tq and t // SEL_BLOCK <= LANES
    rows = HPG * tq
    return pl.pallas_call(
        functools.partial(_nsa_prompt_kernel, tq=tq, tk=tk),
        grid=(n, t // tq),
        in_specs=[pl.BlockSpec((1, tq, Q_W), lambda b, i: (b, i, 0)),
                  pl.BlockSpec((1, tq, LANES), lambda b, i: (b, i, 0)),
                  pl.BlockSpec((1,) + ckv.shape[1:], lambda b, i: (b, 0, 0, 0, 0)),
                  pl.BlockSpec((1, t, KV_W), lambda b, i: (b, 0, 0)),
                  pl.BlockSpec((1, t, KV_W), lambda b, i: (b, 0, 0))],
        out_specs=pl.BlockSpec((1, tq, Q_W), lambda b, i: (b, i, 0)),
        out_shape=jax.ShapeDtypeStruct((n, t, Q_W), jnp.bfloat16),
        scratch_shapes=[pltpu.VMEM((rows, 1), jnp.float32), pltpu.VMEM((rows, 1), jnp.float32),
                        pltpu.VMEM((rows, HEAD_DIM), jnp.float32)],
        compiler_params=_params("parallel", "arbitrary"),
        name="nsa_prompt",
    )(q, gates, ckv, kvs, kvw)


def _conv_kernel(a_ref, b_ref, w_ref, bias_ref, g_ref, beta_ref, o_ref, win_ref, y_ref, *, tt):
    win_ref[0:tt, :] = a_ref[0]
    win_ref[tt:tt + CONV_CTX, :] = b_ref[0]
    first = CONV_CTX - (CONV_WIDTH - 1)
    rc = min(CONV_ROWS, tt)
    for cc in range(CONV_CH // LANES):
        cols = slice(cc * LANES, (cc + 1) * LANES)
        for r0 in range(0, tt, rc):
            acc = jnp.broadcast_to(bias_ref[:, cols], (rc, LANES))
            for k in range(CONV_WIDTH):
                acc = acc + w_ref[k:k + 1, cols] * win_ref[r0 + first + k:r0 + first + k + rc, cols]
            y_ref[r0:r0 + rc, cols] = acc
    y = y_ref[...]
    yc = y - jnp.mean(y, axis=-1, keepdims=True)
    yn = yc * lax.rsqrt(jnp.mean(yc * yc, axis=-1, keepdims=True) + EPS) * g_ref[...] + beta_ref[...]
    o_ref[0] = (yn * jax.nn.sigmoid(yn)).astype(o_ref.dtype)


def conv_act(ctx, up, w_dw, b_dw, ln_g, ln_b, *, block_t=128):
    n, t, ch = up.shape
    tp = -(-t // CONV_CTX) * CONV_CTX
    tt = _row_block(tp, block_t)
    padded = jnp.concatenate([jnp.zeros((n, CONV_CTX - ctx.shape[1], ch), up.dtype), ctx, up,
                              jnp.zeros((n, tp - t, ch), up.dtype)], axis=1)
    w = jnp.zeros((CONV_CTX, ch), jnp.float32).at[:CONV_WIDTH].set(w_dw)
    step = tt // CONV_CTX
    row = lambda v: v.reshape(1, ch).astype(jnp.float32)
    out = pl.pallas_call(
        functools.partial(_conv_kernel, tt=tt),
        grid=(n, tp // tt),
        in_specs=[pl.BlockSpec((1, tt, ch), lambda b, i: (b, i, 0)),
                  pl.BlockSpec((1, CONV_CTX, ch), lambda b, i: (b, (i + 1) * step, 0)),
                  pl.BlockSpec((CONV_CTX, ch), lambda b, i: (0, 0)),
                  pl.BlockSpec((1, ch), lambda b, i: (0, 0)),
                  pl.BlockSpec((1, ch), lambda b, i: (0, 0)),
                  pl.BlockSpec((1, ch), lambda b, i: (0, 0))],
        out_specs=pl.BlockSpec((1, tt, ch), lambda b, i: (b, i, 0)),
        out_shape=jax.ShapeDtypeStruct((n, tp, ch), jnp.bfloat16),
        scratch_shapes=[pltpu.VMEM((tt + CONV_CTX, ch), jnp.float32), pltpu.VMEM((tt, ch), jnp.float32)],
        compiler_params=_params("parallel", "arbitrary"),
        name="conv_act",
    )(padded, padded, w, row(b_dw), row(ln_g), row(ln_b))
    return out[:, :t]


def _cross_attn_kernel(q_ref, kv_ref, o_ref):
    outs = []
    for h in range(MEM_HEADS):
        lo = h * MEM_HEAD_DIM
        q = q_ref[0, :, lo:lo + MEM_HEAD_DIM]
        k = kv_ref[0, :, lo:lo + MEM_HEAD_DIM].astype(jnp.bfloat16)
        v = kv_ref[0, :, MEM_W + lo:MEM_W + lo + MEM_HEAD_DIM].astype(jnp.bfloat16)
        s = lax.dot_general(q, k, _NT, preferred_element_type=jnp.float32) * MEM_SCALE
        e = jnp.exp(s - jnp.max(s, axis=-1, keepdims=True))
        p = e / jnp.sum(e, axis=-1, keepdims=True)
        outs.append(jnp.dot(p.astype(jnp.bfloat16), v, preferred_element_type=jnp.float32))
    o_ref[0] = jnp.concatenate(outs, axis=-1).astype(o_ref.dtype)


def cross_attn_core(q, kv, *, block_q=512):
    n, t, _ = q.shape
    bq = _row_block(t, block_q)
    return pl.pallas_call(
        _cross_attn_kernel,
        grid=(n, t // bq),
        in_specs=[pl.BlockSpec((1, bq, MEM_W), lambda b, i: (b, i, 0)),
                  pl.BlockSpec((1, MEM_LEN, 2 * MEM_W), lambda b, i: (b, 0, 0))],
        out_specs=pl.BlockSpec((1, bq, MEM_W), lambda b, i: (b, i, 0)),
        out_shape=jax.ShapeDtypeStruct((n, t, MEM_W), jnp.bfloat16),
        compiler_params=_params("parallel", "arbitrary"),
        name="cross_attn_core",
    )(q, kv)


def _router_kernel(x_ref, g_ref, w_ref, b_ref, h_ref, e_ref, p_ref):
    x = x_ref[...]
    h = x * lax.rsqrt(jnp.mean(x * x, axis=-1, keepdims=True) + EPS) * g_ref[...]
    h_ref[...] = h
    logits = jnp.dot(h.astype(jnp.bfloat16), w_ref[...], preferred_element_type=jnp.float32) + b_ref[...]
    lt = logits.T[:N_EXPERTS]
    ids = lax.broadcasted_iota(jnp.int32, lt.shape, 0).astype(jnp.float32)
    vals, idxs = [], []
    for _ in range(TOP_K):
        m = jnp.max(lt, axis=0, keepdims=True)
        idx = jnp.min(jnp.where(lt == m, ids, float(N_EXPERTS)), axis=0, keepdims=True)
        vals.append(m)
        idxs.append(idx)
        lt = jnp.where(ids == idx, -jnp.inf, lt)
    top_v = jnp.concatenate(vals, axis=0)
    e = jnp.exp(top_v - vals[0])
    e_ref[...] = jnp.concatenate(idxs, axis=0).astype(jnp.int32)
    p_ref[...] = e / jnp.sum(e, axis=0, keepdims=True)


def moe_router(x, gain, w_router, b_router, *, block_m=512):
    m, d = x.shape
    bm = _row_block(m, block_m)
    w = jnp.zeros((d, LANES), jnp.bfloat16).at[:, :N_EXPERTS].set(w_router.astype(jnp.bfloat16))
    b = jnp.zeros((1, LANES), jnp.float32).at[0, :N_EXPERTS].set(b_router.astype(jnp.float32))
    return pl.pallas_call(
        _router_kernel,
        grid=(m // bm,),
        in_specs=[pl.BlockSpec((bm, d), lambda i: (i, 0)),
                  pl.BlockSpec((1, d), lambda i: (0, 0)),
                  pl.BlockSpec((d, LANES), lambda i: (0, 0)),
                  pl.BlockSpec((1, LANES), lambda i: (0, 0))],
        out_specs=[pl.BlockSpec((bm, d), lambda i: (i, 0)),
                   pl.BlockSpec((TOP_K, bm), lambda i: (0, i)),
                   pl.BlockSpec((TOP_K, bm), lambda i: (0, i))],
        out_shape=[jax.ShapeDtypeStruct((m, d), jnp.float32),
                   jax.ShapeDtypeStruct((TOP_K, m), jnp.int32),
                   jax.ShapeDtypeStruct((TOP_K, m), jnp.float32)],
        compiler_params=_params("parallel"),
        name="moe_router",
    )(x, gain.reshape(1, d).astype(jnp.float32), w, b)


def _expert_kernel(blk_e_ref, blk_rows_ref, tok_cur_ref, tok_nxt_ref, h_hbm, wg_ref, wu_ref, bg_ref, bu_ref,
                   wd_ref, bd_ref, o_ref, x_buf, sem, wg_s, wu_s, wd_s):
    i, j = pl.program_id(0), pl.program_id(1)
    rows = blk_rows_ref[i]
    buf = lax.rem(i, 2)

    def row_copy(tok, dst_buf, r):
        return pltpu.make_async_copy(h_hbm.at[pl.ds(tok, 1)], x_buf.at[dst_buf, pl.ds(r, 1)], sem.at[dst_buf])

    def for_fetched_rows(n_rows, fn):
        for s in range(MOE_BLOCK_ROWS // MOE_SUB_ROWS):
            lo = s * MOE_SUB_ROWS

            @pl.when(lo < n_rows)
            def _():
                def body(r, carry):
                    fn(lo + r)
                    return carry
                lax.fori_loop(0, MOE_SUB_ROWS, body, 0, unroll=8)

    def start_block(tok_ref, n_rows, dst_buf):
        for_fetched_rows(n_rows, lambda r: row_copy(tok_ref[0, 0, r], dst_buf, r).start())

    @pl.when(j == 0)
    def _():
        o_ref[...] = jnp.broadcast_to(bd_ref[0], o_ref.shape)

        @pl.when(i == 0)
        def _():
            start_block(tok_cur_ref, rows, buf)

        for_fetched_rows(rows, lambda r: row_copy(0, buf, r).wait())

        @pl.when(i + 1 < pl.num_programs(0))
        def _():
            start_block(tok_nxt_ref, blk_rows_ref[i + 1], 1 - buf)

    @pl.when(rows > 0)
    def _():
        wg_s[...] = wg_ref[0].astype(jnp.bfloat16)
        wu_s[...] = wu_ref[0].astype(jnp.bfloat16)
        wd_s[...] = wd_ref[0].astype(jnp.bfloat16)

    for s in range(MOE_BLOCK_ROWS // MOE_SUB_ROWS):
        lo = s * MOE_SUB_ROWS

        @pl.when(lo < rows)
        def _():
            x = x_buf[buf, lo:lo + MOE_SUB_ROWS, :].astype(jnp.bfloat16)
            a = jnp.dot(x, wg_s[...], preferred_element_type=jnp.float32) + bg_ref[0]
            u = jnp.dot(x, wu_s[...], preferred_element_type=jnp.float32) + bu_ref[0]
            a = jnp.minimum(a, SWIGLU_LIMIT)
            u = jnp.clip(u, -SWIGLU_LIMIT, SWIGLU_LIMIT)
            act = a * jax.nn.sigmoid(SWIGLU_ALPHA * a) * (u + 1.0)
            o_ref[lo:lo + MOE_SUB_ROWS, :] += jnp.dot(act.astype(jnp.bfloat16), wd_s[...],
                                                      preferred_element_type=jnp.float32)


def moe_experts(h, slot_tok, blk_e, blk_rows, w_gu, b_gu, w_down, b_down):
    s_total = slot_tok.shape[0]
    d = h.shape[1]
    n_e, _, two_ff = w_gu.shape
    ff = two_ff // 2
    nb = s_total // MOE_BLOCK_ROWS
    nf = ff // MOE_FF_TILE
    tok3 = slot_tok.reshape(nb, 1, MOE_BLOCK_ROWS)

    def col(i, j, rows_ref):
        return jnp.where(rows_ref[i] > 0, j, nf - 1)

    grid_spec = pltpu.PrefetchScalarGridSpec(
        num_scalar_prefetch=2,
        grid=(nb, nf),
        in_specs=[
            pl.BlockSpec((1, 1, MOE_BLOCK_ROWS), lambda i, j, e, r: (i, 0, 0), memory_space=pltpu.SMEM),
            pl.BlockSpec((1, 1, MOE_BLOCK_ROWS), lambda i, j, e, r: (jnp.minimum(i + 1, nb - 1), 0, 0),
                         memory_space=pltpu.SMEM),
            pl.BlockSpec(memory_space=pl.ANY),
            pl.BlockSpec((1, d, MOE_FF_TILE), lambda i, j, e, r: (e[i], 0, col(i, j, r))),
            pl.BlockSpec((1, d, MOE_FF_TILE), lambda i, j, e, r: (e[i], 0, nf + col(i, j, r))),
            pl.BlockSpec((1, 1, MOE_FF_TILE), lambda i, j, e, r: (e[i], 0, col(i, j, r))),
            pl.BlockSpec((1, 1, MOE_FF_TILE), lambda i, j, e, r: (e[i], 0, nf + col(i, j, r))),
            pl.BlockSpec((1, MOE_FF_TILE, d), lambda i, j, e, r: (e[i], col(i, j, r), 0)),
            pl.BlockSpec((1, 1, d), lambda i, j, e, r: (e[i], 0, 0)),
        ],
        out_specs=pl.BlockSpec((MOE_BLOCK_ROWS, d), lambda i, j, e, r: (i, 0)),
        scratch_shapes=[pltpu.VMEM((2, MOE_BLOCK_ROWS, d), jnp.float32),
                        pltpu.SemaphoreType.DMA((2,)),
                        pltpu.VMEM((d, MOE_FF_TILE), jnp.bfloat16),
                        pltpu.VMEM((d, MOE_FF_TILE), jnp.bfloat16),
                        pltpu.VMEM((MOE_FF_TILE, d), jnp.bfloat16)],
    )
    b_gu3 = b_gu.reshape(n_e, 1, two_ff)
    return pl.pallas_call(
        _expert_kernel,
        grid_spec=grid_spec,
        out_shape=jax.ShapeDtypeStruct((s_total, d), jnp.float32),
        compiler_params=_params("arbitrary", "arbitrary"),
        name="moe_experts",
    )(blk_e, blk_rows, tok3, tok3, h, w_gu, w_gu, b_gu3, b_gu3, w_down, b_down.reshape(n_e, 1, d))


def moe_layer(x_groups, gain, w_router, b_router, w_gu, b_gu, w_down, b_down, g_final):
    routed = [moe_router(x, gain, w_router, b_router) for x in x_groups]
    h = jnp.concatenate([r[0] for r in routed], axis=0)
    top_e = jnp.concatenate([r[1] for r in routed], axis=1)
    top_p = jnp.concatenate([r[2] for r in routed], axis=1)
    t = h.shape[0]
    bm = MOE_BLOCK_ROWS
    n_blocks = -(-(t * TOP_K) // bm) + N_EXPERTS
    onehot = (top_e[:, :, None] == jnp.arange(N_EXPERTS)[None, None, :]).astype(jnp.int32)
    per_tok = jnp.sum(onehot, axis=0)
    before = jnp.cumsum(per_tok, axis=0) - per_tok
    counts = jnp.sum(per_tok, axis=0)
    blocks_e = (counts + bm - 1) // bm
    blk_end = jnp.cumsum(blocks_e)
    blk_start = blk_end - blocks_e
    first_slot = before + (blk_start * bm)[None, :]
    slot = jnp.sum(onehot * first_slot[None], axis=-1).astype(jnp.int32)
    tok_ids = jnp.broadcast_to(jnp.arange(t, dtype=jnp.int32)[None], slot.shape)
    slot_tok = jnp.zeros((n_blocks * bm,), jnp.int32).at[slot.reshape(-1)].set(tok_ids.reshape(-1))
    blk = jnp.arange(n_blocks)
    blk_e = jnp.minimum(jnp.searchsorted(blk_end, blk, side='right'), N_EXPERTS - 1).astype(jnp.int32)
    used = blk < blk_end[-1]
    last_e = blk_e[jnp.maximum(blk_end[-1] - 1, 0)]
    blk_e = jnp.where(used, blk_e, last_e).astype(jnp.int32)
    blk_rows = jnp.where(used, jnp.clip(counts[blk_e] - (blk - blk_start[blk_e]) * bm, 0, bm), 0).astype(jnp.int32)
    outs = moe_experts(h, slot_tok, blk_e, blk_rows, w_gu, b_gu, w_down, b_down)
    ys, start = [], 0
    for x in x_groups:
        m = x.shape[0]
        ys.append(moe_combine_norm(x, outs, slot[:, start:start + m], top_p[:, start:start + m], g_final))
        start += m
    return ys


def _combine_kernel(idx_cur_ref, idx_nxt_ref, x_ref, p_ref, g_ref, outs_hbm, o_ref, buf, sem, *, bm):
    i = pl.program_id(0)
    cur = lax.rem(i, 2)
    n_rows = TOP_K * bm

    def row_copy(slot, dst, r):
        return pltpu.make_async_copy(outs_hbm.at[pl.ds(slot, 1)], buf.at[dst, pl.ds(r, 1)], sem.at[dst])

    def for_rows(fn):
        def body(r, carry):
            fn(r)
            return carry
        lax.fori_loop(0, n_rows, body, 0, unroll=8)

    @pl.when(i == 0)
    def _():
        for_rows(lambda r: row_copy(idx_cur_ref[0, 0, r], cur, r).start())

    for_rows(lambda r: row_copy(0, cur, r).wait())

    @pl.when(i + 1 < pl.num_programs(0))
    def _():
        for_rows(lambda r: row_copy(idx_nxt_ref[0, 0, r], 1 - cur, r).start())

    x = x_ref[...]
    for k in range(TOP_K):
        x = x + p_ref[:, k:k + 1] * buf[cur, k * bm:(k + 1) * bm, :]
    o_ref[...] = x * lax.rsqrt(jnp.mean(x * x, axis=-1, keepdims=True) + EPS) * g_ref[...]


def moe_combine_norm(x, outs, slot, gate, gain, *, block_m=128):
    m, d = x.shape
    bm = _row_block(m, block_m)
    nt = m // bm
    idx = slot.reshape(TOP_K, nt, bm).transpose(1, 0, 2).reshape(nt, 1, TOP_K * bm)
    return pl.pallas_call(
        functools.partial(_combine_kernel, bm=bm),
        grid=(nt,),
        in_specs=[pl.BlockSpec((1, 1, TOP_K * bm), lambda i: (i, 0, 0), memory_space=pltpu.SMEM),
                  pl.BlockSpec((1, 1, TOP_K * bm), lambda i: (jnp.minimum(i + 1, nt - 1), 0, 0),
                               memory_space=pltpu.SMEM),
                  pl.BlockSpec((bm, d), lambda i: (i, 0)),
                  pl.BlockSpec((bm, TOP_K), lambda i: (i, 0)),
                  pl.BlockSpec((1, d), lambda i: (0, 0)),
                  pl.BlockSpec(memory_space=pl.ANY)],
        out_specs=pl.BlockSpec((bm, d), lambda i: (i, 0)),
        out_shape=jax.ShapeDtypeStruct((m, d), jnp.float32),
        scratch_shapes=[pltpu.VMEM((2, TOP_K * bm, d), jnp.float32), pltpu.SemaphoreType.DMA((2,))],
        compiler_params=_params("arbitrary"),
        name="moe_combine_norm",
    )(idx, idx, x, gate.T, gain.reshape(1, d).astype(jnp.float32), outs)


def _hist_compress_kernel(pt_ref, *refs):
    pages, (w1_ref, pe_ref, w2_ref, o_ref) = refs[:CMP_PAGES + 1], refs[CMP_PAGES + 1:]
    c = CMP_PAGES * CHUNKS_PER_PAGE
    last = lax.broadcasted_iota(jnp.int32, (c, 1), 0) == c - 1

    def rows(page, sub):
        return jnp.concatenate([page[pl.ds(i * KV_ROWS + sub, CHUNKS_PER_PAGE, stride=CMP_STRIDE * KV_ROWS), :]
                                for i in range(CMP_STRIDE)], axis=1)

    for j in range(2):
        w_a = w1_ref[j, :CMP_HALF, :]
        w_b = w1_ref[j, CMP_HALF:, :]
        pe = pe_ref[j]
        bias = (jnp.dot(pe[:, :CMP_HALF], w_a, preferred_element_type=jnp.float32)
                + jnp.dot(pe[:, CMP_HALF:], w_b, preferred_element_type=jnp.float32))[0:1]
        bases = [j * N_KV + g for g in range(N_KV)]
        x = jnp.concatenate([rows(p, base) for base in bases for p in pages[:CMP_PAGES]],
                            axis=0).astype(jnp.bfloat16)
        x_next = jnp.concatenate([rows(pages[CMP_PAGES], base) for base in bases], axis=0).astype(jnp.bfloat16)
        a = jnp.dot(x, w_a, preferred_element_type=jnp.float32)
        b = jnp.dot(x, w_b, preferred_element_type=jnp.float32)
        b_tail = jnp.dot(x_next, w_b, preferred_element_type=jnp.float32)
        for g in range(N_KV):
            b_next = jnp.where(last, b_tail[g * CHUNKS_PER_PAGE:g * CHUNKS_PER_PAGE + 1],
                               pltpu.roll(b[g * c:(g + 1) * c], c - 1, axis=0))
            h = jax.nn.gelu(a[g * c:(g + 1) * c] + b_next + bias)
            o_ref[0, j, g] = jnp.dot(h.astype(jnp.bfloat16), w2_ref[j],
                                     preferred_element_type=jnp.float32).astype(o_ref.dtype)


def compress_history(pool, page_table, cmp_pe, cmp_w1, cmp_w2):
    n, n_pages = page_table.shape
    assert n_pages % CMP_PAGES == 0
    c = n_pages * CHUNKS_PER_PAGE
    w1, pe, w2 = _compress_weights(cmp_pe, cmp_w1, cmp_w2)

    def page_spec(k):
        return pl.BlockSpec((PAGE_SIZE * KV_ROWS, HEAD_DIM),
                            lambda b, s, pt: (pt[b * n_pages + jnp.minimum(s * CMP_PAGES + k, n_pages - 1)], 0))

    grid_spec = pltpu.PrefetchScalarGridSpec(
        num_scalar_prefetch=1,
        grid=(n, n_pages // CMP_PAGES),
        in_specs=[page_spec(k) for k in range(CMP_PAGES + 1)] + [
            pl.BlockSpec(w1.shape, lambda b, s, pt: (0, 0, 0)),
            pl.BlockSpec(pe.shape, lambda b, s, pt: (0, 0, 0)),
            pl.BlockSpec(w2.shape, lambda b, s, pt: (0, 0, 0))],
        out_specs=pl.BlockSpec((1, 2, N_KV, CMP_PAGES * CHUNKS_PER_PAGE, HEAD_DIM), lambda b, s, pt: (b, 0, 0, s, 0)),
    )
    return pl.pallas_call(
        _hist_compress_kernel,
        grid_spec=grid_spec,
        out_shape=jax.ShapeDtypeStruct((n, 2, N_KV, c, HEAD_DIM), jnp.bfloat16),
        compiler_params=_params("parallel", "arbitrary"),
        name="compress_history",
    )(page_table.reshape(-1), *([pool] * (CMP_PAGES + 1)), w1, pe, w2)


def _nsa_sample_dense_kernel(q_ref, ckv_ref, kw_ref, ocmp_ref, owin_ref, idx_ref, val_ref, *, past, n_cmp, n_sel):
    rows = HPG * DEC_PAD
    tok = lax.bitwise_and(lax.broadcasted_iota(jnp.int32, (rows, 1), 0), DEC_PAD - 1)
    qpos = past + tok
    qpos_l = past + lax.broadcasted_iota(jnp.int32, (1, LANES), 1)
    c_all = ckv_ref.shape[3]
    n_sel_pad = -(-n_sel // 8) * 8
    wlen = kw_ref.shape[1]
    for g in range(N_KV):
        qg = q_ref[0, g]
        s = lax.dot_general(qg, ckv_ref[0, 0, g], _NT, preferred_element_type=jnp.float32) * SCALE
        cidx = lax.broadcasted_iota(jnp.int32, (1, c_all), 1)
        vis = (cidx * CMP_STRIDE + (CMP_LEN - 1) <= qpos) & (cidx < n_cmp)
        p = _softmax_rows(s, vis)
        ocmp_ref[0, g] = jnp.dot(p.astype(jnp.bfloat16), ckv_ref[0, 1, g], preferred_element_type=jnp.float32)
        p_grp = p[0:DEC_PAD]
        for hh in range(1, HPG):
            p_grp = p_grp + p[hh * DEC_PAD:(hh + 1) * DEC_PAD]
        p_grp = jnp.concatenate([p_grp, jnp.zeros((LANES - DEC_PAD, c_all), jnp.float32)], axis=0)
        c0 = lax.broadcasted_iota(jnp.int32, (n_sel_pad, c_all), 1) * CMP_STRIDE
        s0 = lax.broadcasted_iota(jnp.int32, (n_sel_pad, c_all), 0) * SEL_BLOCK
        overlap_t = ((c0 < s0 + SEL_BLOCK) & (c0 + CMP_LEN > s0)).astype(jnp.bfloat16)
        score_t = lax.dot_general(overlap_t, p_grp.astype(jnp.bfloat16), _NT, preferred_element_type=jnp.float32)
        blk = lax.broadcasted_iota(jnp.int32, score_t.shape, 0)
        cur = lax.shift_right_logical(qpos_l, SEL_BLOCK.bit_length() - 1)
        avail = (blk <= cur) & (blk < n_sel)
        forced = avail & ((blk == 0) | (blk == cur) | (blk == cur - 1))
        work = jnp.where(forced, jnp.inf, jnp.where(avail, score_t, -jnp.inf))
        blk_f = blk.astype(jnp.float32)
        idxs, vals = [], []
        for _ in range(SEL_TOPK):
            m = jnp.max(work, axis=0, keepdims=True)
            first = jnp.min(jnp.where(work == m, blk_f, float(n_sel_pad)), axis=0, keepdims=True)
            idxs.append(first)
            vals.append(jnp.where(m > -jnp.inf, 1.0, 0.0))
            work = jnp.where(blk_f == first, -jnp.inf, work)
        idx_ref[0, g] = jnp.concatenate(idxs, axis=0).astype(jnp.int32)
        val_ref[0, g] = jnp.concatenate(vals, axis=0).astype(jnp.int32)

        kcol = slice(g * HEAD_DIM, (g + 1) * HEAD_DIM)
        vcol = slice((N_KV + g) * HEAD_DIM, (N_KV + g + 1) * HEAD_DIM)
        s = lax.dot_general(qg, kw_ref[0, :, kcol].astype(jnp.bfloat16), _NT,
                            preferred_element_type=jnp.float32) * SCALE
        kpos = past - WINDOW + lax.broadcasted_iota(jnp.int32, (1, wlen), 1)
        p = _softmax_rows(s, (kpos <= qpos) & (kpos > qpos - WINDOW))
        owin_ref[0, g] = jnp.dot(p.astype(jnp.bfloat16), kw_ref[0, :, vcol].astype(jnp.bfloat16),
                                 preferred_element_type=jnp.float32)


def nsa_sample_dense(qd, ckv, kw, *, past, n_cmp, n_sel):
    n = qd.shape[0]
    rows = HPG * DEC_PAD
    o_shape = jax.ShapeDtypeStruct((n, N_KV, rows, HEAD_DIM), jnp.float32)
    i_shape = jax.ShapeDtypeStruct((n, N_KV, SEL_TOPK, LANES), jnp.int32)
    o_spec = pl.BlockSpec((1, N_KV, rows, HEAD_DIM), lambda b: (b, 0, 0, 0))
    i_spec = pl.BlockSpec((1, N_KV, SEL_TOPK, LANES), lambda b: (b, 0, 0, 0))
    return pl.pallas_call(
        functools.partial(_nsa_sample_dense_kernel, past=past, n_cmp=n_cmp, n_sel=n_sel),
        grid=(n,),
        in_specs=[pl.BlockSpec((1, N_KV, rows, HEAD_DIM), lambda b: (b, 0, 0, 0)),
                  pl.BlockSpec((1,) + ckv.shape[1:], lambda b: (b, 0, 0, 0, 0)),
                  pl.BlockSpec((1,) + kw.shape[1:], lambda b: (b, 0, 0))],
        out_specs=[o_spec, o_spec, i_spec, i_spec],
        out_shape=[o_shape, o_shape, i_shape, i_shape],
        compiler_params=_params("parallel"),
        name="nsa_sample_dense",
    )(qd, ckv, kw)


def _nsa_sample_sel_kernel(idx_ref, val_ref, pt_ref, q_ref, new_ref, *refs, tq, past, n_past_blk):
    blocks, o_ref = refs[:N_KV * SEL_TOPK], refs[N_KV * SEL_TOPK]
    b, t = pl.program_id(0), pl.program_id(1)
    qpos = past + t
    lane = lax.broadcasted_iota(jnp.int32, (1, SEL_TOPK * SEL_BLOCK), 1)
    slot_of_lane = lax.shift_right_logical(lane, SEL_BLOCK.bit_length() - 1)
    for g in range(N_KV):
        base = ((b * tq + t) * N_KV + g) * SEL_TOPK
        ks, vs = [], []
        tokpos = jnp.zeros(lane.shape, jnp.int32)
        ok = jnp.zeros(lane.shape, jnp.int32)
        for s in range(SEL_TOPK):
            blk = idx_ref[base + s]
            from_pool = blk < n_past_blk
            pool_blk = blocks[g * SEL_TOPK + s]
            k_rows = pl.ds(g, SEL_BLOCK, stride=KV_ROWS)
            v_rows = pl.ds(N_KV + g, SEL_BLOCK, stride=KV_ROWS)
            ks.append(jnp.where(from_pool, pool_blk[k_rows, :], new_ref[0, k_rows, :]))
            vs.append(jnp.where(from_pool, pool_blk[v_rows, :], new_ref[0, v_rows, :]))
            here = slot_of_lane == s
            tokpos = jnp.where(here, blk * SEL_BLOCK, tokpos)
            ok = jnp.where(here, val_ref[base + s], ok)
        tokpos = tokpos + lax.bitwise_and(lane, SEL_BLOCK - 1)
        mask = (ok > 0) & (tokpos <= qpos)
        k = jnp.concatenate(ks, axis=0).astype(jnp.bfloat16)
        v = jnp.concatenate(vs, axis=0).astype(jnp.bfloat16)
        s = lax.dot_general(q_ref[0, 0, g], k, _NT, preferred_element_type=jnp.float32) * SCALE
        p = _softmax_rows(s, mask)
        o_ref[0, 0, g] = jnp.dot(p.astype(jnp.bfloat16), v, preferred_element_type=jnp.float32)


def nsa_sample_sel(qs, idx, valid, page_table, pool, new_blocks, *, past):
    n, tq = qs.shape[:2]
    n_pages = page_table.shape[1]
    bpp = PAGE_SIZE // SEL_BLOCK
    n_past_blk = n_pages * bpp

    def block_spec(g, s):
        def index(b, t, idx_ref, val_ref, pt_ref):
            blk = jnp.minimum(idx_ref[((b * tq + t) * N_KV + g) * SEL_TOPK + s], n_past_blk - 1)
            return (pt_ref[b * n_pages + blk // bpp] * bpp + blk % bpp, 0)
        return pl.BlockSpec((SEL_BLOCK * KV_ROWS, HEAD_DIM), index)

    grid_spec = pltpu.PrefetchScalarGridSpec(
        num_scalar_prefetch=3,
        grid=(n, tq),
        in_specs=[pl.BlockSpec((1, 1, N_KV, 16, HEAD_DIM), lambda b, t, *_: (b, t, 0, 0, 0)),
                  pl.BlockSpec((1, SEL_BLOCK * KV_ROWS, HEAD_DIM), lambda b, t, *_: (b, 0, 0))]
                 + [block_spec(g, s) for g in range(N_KV) for s in range(SEL_TOPK)],
        out_specs=pl.BlockSpec((1, 1, N_KV, 16, HEAD_DIM), lambda b, t, *_: (b, t, 0, 0, 0)),
    )
    return pl.pallas_call(
        functools.partial(_nsa_sample_sel_kernel, tq=tq, past=past, n_past_blk=n_past_blk),
        grid_spec=grid_spec,
        out_shape=jax.ShapeDtypeStruct((n, tq, N_KV, 16, HEAD_DIM), jnp.float32),
        compiler_params=_params("parallel", "arbitrary"),
        name="nsa_sample_sel",
    )(idx.reshape(-1), valid.reshape(-1), page_table.reshape(-1), qs, new_blocks,
      *([pool] * (N_KV * SEL_TOPK)))


def nsa_sample(q, gates, kvs_new, kvw_new, pool_cmp, pool_sel, win_cache, page_table, cmp_pe, cmp_w1, cmp_w2):
    n, tq, _ = q.shape
    past = page_table.shape[1] * PAGE_SIZE
    assert tq <= DEC_PAD and tq <= SEL_BLOCK and win_cache.shape[1] == WINDOW
    assert (past + tq) // CMP_STRIDE * CMP_STRIDE == past
    n_cmp = past // CMP_STRIDE - 1
    n_sel = -(-(past + tq) // SEL_BLOCK)
    ckv = compress_history(pool_cmp, page_table, cmp_pe, cmp_w1, cmp_w2)
    q5 = q.reshape(n, tq, N_KV, HPG, HEAD_DIM)
    qd = jnp.pad(q5.transpose(0, 2, 3, 1, 4), ((0, 0), (0, 0), (0, 0), (0, DEC_PAD - tq), (0, 0)))
    qd = qd.reshape(n, N_KV, HPG * DEC_PAD, HEAD_DIM)
    kw_all = jnp.concatenate([win_cache, kvw_new], axis=1)
    kw = jnp.pad(kw_all, ((0, 0), (0, -(-(WINDOW + DEC_PAD) // LANES) * LANES - WINDOW - tq), (0, 0)))
    o_cmp, o_win, idx, valid = nsa_sample_dense(qd, ckv, kw, past=past, n_cmp=n_cmp, n_sel=n_sel)
    idx = idx[..., :tq].transpose(0, 3, 1, 2)
    valid = valid[..., :tq].transpose(0, 3, 1, 2)
    qs = jnp.pad(q5, ((0, 0), (0, 0), (0, 0), (0, 16 - HPG), (0, 0)))
    new_blocks = jnp.pad(kvs_new, ((0, 0), (0, SEL_BLOCK - tq), (0, 0))).reshape(n, SEL_BLOCK * KV_ROWS, HEAD_DIM)
    o_sel = nsa_sample_sel(qs, idx, valid, page_table, pool_sel, new_blocks, past=past)[:, :, :, :HPG]
    unpack = lambda o: o.reshape(n, N_KV, HPG, DEC_PAD, HEAD_DIM)[:, :, :, :tq].transpose(0, 3, 1, 2, 4)
    g = gates[..., :GATE_W].reshape(n, tq, 3, N_KV, HPG)[..., None]
    o = g[:, :, 0] * unpack(o_cmp) + g[:, :, 1] * o_sel + g[:, :, 2] * unpack(o_win)
    return o.reshape(n, tq, Q_W), kw_all


def prepare_weights(w_in, w_nsa_o, w_conv_o, w_out, w_xq, w_xkv, w_xo):
    bf = jnp.bfloat16
    w = w_in.astype(bf)
    d = w.shape[0]
    s = IN_SPLITS
    return dict(
        q=w[:, :s[0]], kv=w[:, s[0]:s[3]],
        gates=jnp.zeros((d, LANES), bf).at[:, :GATE_W].set(w[:, s[3]:s[4]]),
        glu_a=w[:, s[4]:s[4] + CONV_CH], glu_b=w[:, s[4] + CONV_CH:s[5]], merge=w[:, s[5]:],
        nsa_o=w_nsa_o.astype(bf), conv_o=w_conv_o.astype(bf), out=w_out.astype(bf),
        xq=w_xq.astype(bf), xkv=w_xkv.astype(bf), xo=w_xo.astype(bf))


def front(x2, g_mix, wts):
    q = fused_linear(x2, [wts['q']], gain=g_mix, out_dtype=jnp.bfloat16, name="front_q")
    kv = fused_linear(x2, [wts['kv']], gain=g_mix, name="front_kv")
    gates = fused_linear(x2, [wts['gates']], gain=g_mix, epilogue=_sigmoid_epilogue, name="front_gates")
    up = fused_linear(x2, [wts['glu_a'], wts['glu_b']], gain=g_mix, epilogue=_glu_epilogue, name="front_glu")
    mg = fused_linear(x2, [wts['merge']], gain=g_mix, epilogue=_sigmoid_epilogue, name="front_merge")
    return q, kv, gates, up, mg


def mixer_tail(x2, o, act, mg, wts):
    merged = merge_branches(o, act, wts['nsa_o'], wts['conv_o'], mg)
    return fused_linear(merged, [wts['out']], tiles=[x2], epilogue=_residual_epilogue, name="mixer_out_proj")


def cross_attn_block(x2, n, mkv, g_xattn, wts):
    m = x2.shape[0]
    qx = fused_linear(x2, [wts['xq']], gain=g_xattn, out_dtype=jnp.bfloat16, name="xattn_q")
    oc = cross_attn_core(qx.reshape(n, m // n, MEM_W), mkv)
    return fused_linear(oc.reshape(m, MEM_W), [wts['xo']], tiles=[x2], epilogue=_residual_epilogue, name="xattn_o")


def kernel(x_prompt, x_sample, cache_kv_cmp, cache_kv_sel, cache_kv_win, cache_conv, cache_mem_kv, page_table,
           mem_prompt, g_mix, w_in, cmp_pe, cmp_w1, cmp_w2, w_nsa_o, w_dw, b_dw, ln_conv_g, ln_conv_b, w_conv_o,
           w_out, g_xattn, g_mem, w_xq, w_xkv, w_xo, g_moe, w_router, b_router, w_gu, b_gu, w_down, b_down, g_final):
    nb, t, d = x_prompt.shape
    nd, tq = x_sample.shape[:2]
    l = 0
    bf = jnp.bfloat16
    wts = prepare_weights(w_in[l], w_nsa_o[l], w_conv_o[l], w_out[l], w_xq[l], w_xkv[l], w_xo[l])
    kv_shape = lambda n_, t_: (n_, t_, 2, N_KV, HEAD_DIM)

    xp2 = x_prompt.reshape(nb * t, d)
    q, kv, gates, up_new, mg = front(xp2, g_mix[l], wts)
    kvc, kvs, kvw = (kv[:, i * KV_W:(i + 1) * KV_W].reshape(nb, t, KV_W) for i in range(3))
    ckv = compress_prompt(kvc, cmp_pe[l], cmp_w1[l], cmp_w2[l])
    o = nsa_prompt(q.reshape(nb, t, Q_W), gates.reshape(nb, t, LANES), ckv, kvs.astype(bf), kvw.astype(bf))
    up_new = up_new.reshape(nb, t, CONV_CH)
    act = conv_act(jnp.zeros((nb, CONV_WIDTH - 1, CONV_CH), jnp.float32), up_new, w_dw[l], b_dw[l],
                   ln_conv_g[l], ln_conv_b[l])
    xp = mixer_tail(xp2, o.reshape(nb * t, Q_W), act.reshape(nb * t, CONV_CH), mg, wts)
    mkv = fused_linear(mem_prompt.reshape(nb * MEM_LEN, d), [wts['xkv']], gain=g_mem[l], name="mem_kv")
    xp = cross_attn_block(xp, nb, mkv.reshape(nb, MEM_LEN, 2 * MEM_W), g_xattn[l], wts)
    p_kv_cmp, p_kv_sel = kvc.reshape(kv_shape(nb, t))[None], kvs.reshape(kv_shape(nb, t))[None]
    p_kv_win = kvw.reshape(kv_shape(nb, t))[:, t - min(WINDOW, t):][None]
    p_conv = jnp.pad(up_new, ((0, 0), (CONV_WIDTH - 1, 0), (0, 0)))[:, t:][None]
    p_mem = mkv.reshape(nb, MEM_LEN, 2, MEM_HEADS, MEM_HEAD_DIM)[None]

    xs2 = x_sample.reshape(nd * tq, d)
    q, kv, gates, up_new, mg = front(xs2, g_mix[l], wts)
    kvc, kvs, kvw = (kv[:, i * KV_W:(i + 1) * KV_W].reshape(nd, tq, KV_W) for i in range(3))
    o, win_all = nsa_sample(q.reshape(nd, tq, Q_W), gates.reshape(nd, tq, LANES), kvs, kvw,
                            cache_kv_cmp[l].reshape(-1, HEAD_DIM), cache_kv_sel[l].reshape(-1, HEAD_DIM),
                            cache_kv_win[l].reshape(nd, -1, KV_W), page_table, cmp_pe[l], cmp_w1[l], cmp_w2[l])
    win_new = win_all[:, -min(WINDOW, win_all.shape[1]):].reshape(kv_shape(nd, min(WINDOW, win_all.shape[1])))
    kvc, kvs = kvc.reshape(kv_shape(nd, tq)), kvs.reshape(kv_shape(nd, tq))
    up_new = up_new.reshape(nd, tq, CONV_CH)
    act = conv_act(cache_conv[l], up_new, w_dw[l], b_dw[l], ln_conv_g[l], ln_conv_b[l])
    xs = mixer_tail(xs2, o.reshape(nd * tq, Q_W).astype(bf), act.reshape(nd * tq, CONV_CH), mg, wts)
    xs = cross_attn_block(xs, nd, cache_mem_kv[l].reshape(nd, MEM_LEN, 2 * MEM_W), g_xattn[l], wts)
    s_conv = jnp.concatenate([cache_conv[l], up_new], axis=1)[:, tq:][None]

    y_prompt, y_sample = moe_layer([xp, xs], g_moe[l], w_router[l], b_router[l], w_gu[l], b_gu[l], w_down[l],
                                   b_down[l], g_final)
    y_prompt, y_sample = y_prompt.reshape(nb, t, d), y_sample.reshape(nd, tq, d)
    return (y_prompt, y_sample, p_kv_cmp, p_kv_sel, p_kv_win, p_conv, p_mem, kvc[None], kvs[None], win_new[None],
            s_conv)
```

```python
import functools

import jax
import jax.numpy as jnp
from jax import lax
from jax.experimental import pallas as pl
from jax.experimental.pallas import tpu as pltpu

D_MODEL = 2048
PAGE_SIZE = 128

N_HEADS = 16
HEAD_DIM = 128
N_KV = 2
HPG = N_HEADS // N_KV
CMP_STRIDE = 16
CMP_LEN = 2 * CMP_STRIDE
CMP_HALF = CMP_STRIDE * HEAD_DIM
SEL_BLOCK = 64
SEL_TOPK = 16
WINDOW = 512
SCALE = HEAD_DIM ** -0.5
EXP2_SCALE = SCALE * 1.4426950408889634
CONV_CH = D_MODEL // 2
CONV_WIDTH = 31
MEM_LEN = 256
MEM_HEADS = 4
MEM_HEAD_DIM = 128
MEM_W = MEM_HEADS * MEM_HEAD_DIM
MEM_SCALE = MEM_HEAD_DIM ** -0.5
N_EXPERTS = 32
TOP_K = 4
SWIGLU_LIMIT = 7.0
SWIGLU_ALPHA = 1.702
EPS = 1e-6
Q_W = N_HEADS * HEAD_DIM
KV_W = 2 * N_KV * HEAD_DIM
KV_ROWS = KV_W // HEAD_DIM
GATE_W = 3 * N_HEADS
GLU_W = 2 * CONV_CH
MERGE_W = 2 * D_MODEL
IN_SPLITS = [Q_W, Q_W + KV_W, Q_W + 2 * KV_W, Q_W + 3 * KV_W, Q_W + 3 * KV_W + GATE_W, Q_W + 3 * KV_W + GATE_W + GLU_W]

V7X_VMEM_BYTES = 64 * 1024 * 1024
VMEM_LIMIT_BYTES = V7X_VMEM_BYTES - 8 * 1024 * 1024
LANES = 128

MOE_BLOCK_ROWS = 1152
MOE_SUB_ROWS = 128
MOE_FF_TILE = 256
NSA_Q_TILE = 128
NSA_KV_TILE = 1024
CONV_CTX = 32
CONV_ROWS = 64
CHUNKS_PER_PAGE = PAGE_SIZE // CMP_STRIDE
CMP_PAGES = 32
SLAB_PITCH = CMP_PAGES * CHUNKS_PER_PAGE + 8
DEC_PAD = 8

NEG = -1e30
_NT = (((1,), (1,)), ((), ()))


def _params(*semantics):
    return pltpu.CompilerParams(dimension_semantics=semantics, vmem_limit_bytes=VMEM_LIMIT_BYTES)


def _row_block(rows, target):
    if rows <= target:
        return rows
    for b in range(target, 15, -1):
        if rows % b == 0 and b % 16 == 0:
            return b
    return rows


def _linear_kernel(*refs, n_w, norm, n_tile, n_row, epilogue):
    x_ref = refs[0]
    k = 1
    g_ref = refs[k] if norm else None
    k += int(norm)
    w_refs = refs[k:k + n_w]
    k += n_w
    tile_refs = refs[k:k + n_tile]
    k += n_tile
    row_refs = refs[k:k + n_row]
    k += n_row
    o_ref, h_ref = refs[k], refs[k + 1]

    @pl.when(pl.program_id(1) == 0)
    def _():
        x = x_ref[...].astype(jnp.float32)
        if norm:
            x = x * lax.rsqrt(jnp.mean(x * x, axis=-1, keepdims=True) + EPS) * g_ref[...]
        h_ref[...] = x.astype(jnp.bfloat16)

    h = h_ref[...]
    zs = [jnp.dot(h, w[...], preferred_element_type=jnp.float32) for w in w_refs]
    o_ref[...] = epilogue(zs, [t[...] for t in tile_refs], [r[...] for r in row_refs]).astype(o_ref.dtype)


def fused_linear(x, ws, *, gain=None, tiles=(), rows=(), epilogue=None, out_dtype=jnp.float32,
                 block_m=1024, block_n=512, name="fused_linear"):
    m, k = x.shape
    n = ws[0].shape[1]
    bm = _row_block(m, block_m)
    bn = min(block_n, n)
    assert m % bm == 0 and n % bn == 0 and all(w.shape == (k, n) for w in ws)
    if epilogue is None:
        epilogue = _first_epilogue
    norm = gain is not None
    in_specs = [pl.BlockSpec((bm, k), lambda i, j: (i, 0))]
    args = [x]
    if norm:
        in_specs.append(pl.BlockSpec((1, k), lambda i, j: (0, 0)))
        args.append(gain.reshape(1, k).astype(jnp.float32))
    in_specs += [pl.BlockSpec((k, bn), lambda i, j: (0, j)) for _ in ws]
    args += list(ws)
    in_specs += [pl.BlockSpec((bm, bn), lambda i, j: (i, j)) for _ in tiles]
    args += list(tiles)
    in_specs += [pl.BlockSpec((1, bn), lambda i, j: (0, j)) for _ in rows]
    args += list(rows)
    return pl.pallas_call(
        functools.partial(_linear_kernel, n_w=len(ws), norm=norm, n_tile=len(tiles), n_row=len(rows),
                          epilogue=epilogue),
        grid=(m // bm, n // bn),
        in_specs=in_specs,
        out_specs=pl.BlockSpec((bm, bn), lambda i, j: (i, j)),
        out_shape=jax.ShapeDtypeStruct((m, n), out_dtype),
        scratch_shapes=[pltpu.VMEM((bm, k), jnp.bfloat16)],
        compiler_params=_params("parallel", "arbitrary"),
        name=name,
    )(*args)


def _first_epilogue(zs, ts, rs):
    return zs[0]


def _glu_epilogue(zs, ts, rs):
    return zs[0] * jax.nn.sigmoid(zs[1])


def _sigmoid_epilogue(zs, ts, rs):
    return jax.nn.sigmoid(zs[0])


def _residual_epilogue(zs, ts, rs):
    return ts[0] + zs[0]


def _merge_kernel(o_ref, a_ref, wo_ref, wc_ref, ga_ref, gc_ref, out_ref):
    y_att = jnp.dot(o_ref[...], wo_ref[...], preferred_element_type=jnp.float32)
    y_conv = jnp.dot(a_ref[...], wc_ref[...], preferred_element_type=jnp.float32)
    out_ref[...] = (ga_ref[...] * y_att + gc_ref[...] * y_conv).astype(out_ref.dtype)


def merge_branches(o, act, w_nsa_o, w_conv_o, merge_gates, *, block_m=1024, block_n=512):
    m = o.shape[0]
    d = w_nsa_o.shape[1]
    bm = _row_block(m, block_m)
    bn = min(block_n, d)
    nj = d // bn
    return pl.pallas_call(
        _merge_kernel,
        grid=(m // bm, nj),
        in_specs=[pl.BlockSpec((bm, o.shape[1]), lambda i, j: (i, 0)),
                  pl.BlockSpec((bm, act.shape[1]), lambda i, j: (i, 0)),
                  pl.BlockSpec((w_nsa_o.shape[0], bn), lambda i, j: (0, j)),
                  pl.BlockSpec((w_conv_o.shape[0], bn), lambda i, j: (0, j)),
                  pl.BlockSpec((bm, bn), lambda i, j: (i, j)),
                  pl.BlockSpec((bm, bn), lambda i, j: (i, j + nj))],
        out_specs=pl.BlockSpec((bm, bn), lambda i, j: (i, j)),
        out_shape=jax.ShapeDtypeStruct((m, d), jnp.bfloat16),
        compiler_params=_params("parallel", "arbitrary"),
        name="merge_branches",
    )(o, act, w_nsa_o, w_conv_o, merge_gates, merge_gates)


def _compress_kernel(x_ref, w1_ref, pe_ref, w2_ref, o_ref):
    c = x_ref.shape[1]
    for j in range(2):
        w_a = w1_ref[j, :, :HEAD_DIM]
        w_b = w1_ref[j, :, HEAD_DIM:]
        pe = pe_ref[j]
        bias = (jnp.dot(pe[:, :CMP_HALF], w_a, preferred_element_type=jnp.float32)
                + jnp.dot(pe[:, CMP_HALF:], w_b, preferred_element_type=jnp.float32))[0:1]
        xs = []
        for g in range(N_KV):
            base = (j * N_KV + g) * HEAD_DIM
            xs.append(jnp.concatenate(
                [x_ref[0, :, i * KV_W + base:i * KV_W + base + HEAD_DIM] for i in range(CMP_STRIDE)], axis=1))
        x = jnp.concatenate(xs, axis=0).astype(jnp.bfloat16)
        ab = jnp.dot(x, w1_ref[j], preferred_element_type=jnp.float32)
        a, b = ab[:, :HEAD_DIM], ab[:, HEAD_DIM:]
        for g in range(N_KV):
            b_next = pltpu.roll(b[g * c:(g + 1) * c], c - 1, axis=0)
            h = jax.nn.gelu(a[g * c:(g + 1) * c] + b_next + bias)
            o_ref[0, j, g] = jnp.dot(h.astype(jnp.bfloat16), w2_ref[j],
                                     preferred_element_type=jnp.float32).astype(o_ref.dtype)


def _compress_weights(cmp_pe, cmp_w1, cmp_w2):
    bf = jnp.bfloat16
    w1 = jnp.concatenate([cmp_w1[:, :CMP_STRIDE].reshape(2, CMP_HALF, HEAD_DIM),
                          cmp_w1[:, CMP_STRIDE:].reshape(2, CMP_HALF, HEAD_DIM)], axis=-1).astype(bf)
    pe = jnp.broadcast_to(cmp_pe.reshape(2, 1, CMP_LEN * HEAD_DIM), (2, 8, CMP_LEN * HEAD_DIM)).astype(bf)
    return w1, pe, cmp_w2.astype(bf)


def compress_prompt(kvc, cmp_pe, cmp_w1, cmp_w2):
    n, t, _ = kvc.shape
    c = t // CMP_STRIDE
    w1, pe, w2 = _compress_weights(cmp_pe, cmp_w1, cmp_w2)
    return pl.pallas_call(
        _compress_kernel,
        grid=(n,),
        in_specs=[pl.BlockSpec((1, c, CMP_STRIDE * KV_W), lambda b: (b, 0, 0)),
                  pl.BlockSpec(w1.shape, lambda b: (0, 0, 0)),
                  pl.BlockSpec(pe.shape, lambda b: (0, 0, 0)),
                  pl.BlockSpec(w2.shape, lambda b: (0, 0, 0))],
        out_specs=pl.BlockSpec((1, 2, N_KV, c, HEAD_DIM), lambda b: (b, 0, 0, 0, 0)),
        out_shape=jax.ShapeDtypeStruct((n, 2, N_KV, c, HEAD_DIM), jnp.bfloat16),
        compiler_params=_params("parallel"),
        name="compress_prompt",
    )(kvc.reshape(n, c, CMP_STRIDE * KV_W), w1, pe, w2)


def _softmax_rows(s, mask, every_row_valid=False):
    sm = jnp.where(mask, s, NEG)
    e = jnp.exp2((sm - jnp.max(sm, axis=-1, keepdims=True)) * EXP2_SCALE)
    if not every_row_valid:
        e = jnp.where(mask, e, 0.0)
    return e / jnp.maximum(jnp.sum(e, axis=-1, keepdims=True), 1e-30)


def _select_blocks_t(score_t, qpos_l):
    blk = lax.broadcasted_iota(jnp.int32, score_t.shape, 0)
    cur = lax.shift_right_logical(qpos_l, SEL_BLOCK.bit_length() - 1)
    avail = blk <= cur
    forced = avail & ((blk == 0) | (blk == cur) | (blk == cur - 1))
    work = jnp.where(forced, jnp.inf, jnp.where(avail, score_t, -jnp.inf))
    blk_f = blk.astype(jnp.float32)
    sel = jnp.zeros(score_t.shape, jnp.float32)
    for _ in range(SEL_TOPK):
        m = jnp.max(work, axis=0, keepdims=True)
        first = jnp.min(jnp.where(work == m, blk_f, float(score_t.shape[0])), axis=0, keepdims=True)
        hit = blk_f == first
        sel = jnp.where(hit & (m > -jnp.inf), 1.0, sel)
        work = jnp.where(hit, -jnp.inf, work)
    return sel


def _nsa_prompt_kernel(q_ref, gate_ref, ckv_ref, ks_ref, kw_ref, o_ref, m_ref, l_ref, acc_ref, *, tq, tk):
    b = pl.program_id(1)
    q0 = b * tq
    qpos = q0 + lax.broadcasted_iota(jnp.int32, (tq, 1), 0)
    qpos_l = q0 + lax.broadcasted_iota(jnp.int32, (1, tq), 1)
    n_cmp = ckv_ref.shape[3]
    gates = gate_ref[0]
    blocks_per_tile = tk // SEL_BLOCK

    for g in range(N_KV):
        kcol = slice(g * HEAD_DIM, (g + 1) * HEAD_DIM)
        vcol = slice((N_KV + g) * HEAD_DIM, (N_KV + g + 1) * HEAD_DIM)
        qg = jnp.concatenate([q_ref[0, :, (g * HPG + hh) * HEAD_DIM:(g * HPG + hh + 1) * HEAD_DIM]
                              for hh in range(HPG)], axis=0)

        s = lax.dot_general(qg, ckv_ref[0, 0, g], _NT, preferred_element_type=jnp.float32)
        cend = lax.broadcasted_iota(jnp.int32, (1, n_cmp), 1) * CMP_STRIDE + (CMP_LEN - 1)
        vis = cend <= qpos
        ps = [_softmax_rows(s[hh * tq:(hh + 1) * tq], vis) for hh in range(HPG)]
        o_cmp = jnp.dot(jnp.concatenate(ps, axis=0).astype(jnp.bfloat16), ckv_ref[0, 1, g],
                        preferred_element_type=jnp.float32)
        p_grp = ps[0]
        for p in ps[1:]:
            p_grp = p_grp + p

        c0 = lax.broadcasted_iota(jnp.int32, (LANES, n_cmp), 1) * CMP_STRIDE
        s0 = lax.broadcasted_iota(jnp.int32, (LANES, n_cmp), 0) * SEL_BLOCK
        overlap_t = ((c0 < s0 + SEL_BLOCK) & (c0 + CMP_LEN > s0)).astype(jnp.bfloat16)
        score_t = lax.dot_general(overlap_t, p_grp.astype(jnp.bfloat16), _NT, preferred_element_type=jnp.float32)
        sel = _select_blocks_t(score_t, qpos_l).T.astype(jnp.bfloat16)

        m_ref[...] = jnp.full(m_ref.shape, NEG, jnp.float32)
        l_ref[...] = jnp.zeros(l_ref.shape, jnp.float32)
        acc_ref[...] = jnp.zeros(acc_ref.shape, jnp.float32)

        def sel_tile(kt, causal):
            k0 = pl.multiple_of(kt * tk, tk)
            s = lax.dot_general(qg, ks_ref[0, pl.ds(k0, tk), kcol], _NT, preferred_element_type=jnp.float32)
            jrow = lax.broadcasted_iota(jnp.int32, (LANES, tk), 0)
            jcol = lax.shift_right_logical(lax.broadcasted_iota(jnp.int32, (LANES, tk), 1),
                                           SEL_BLOCK.bit_length() - 1)
            expand = (jrow == kt * blocks_per_tile + jcol).astype(jnp.bfloat16)
            mask = jnp.dot(sel, expand, preferred_element_type=jnp.float32) > 0.5
            if causal:
                mask = mask & (k0 + lax.broadcasted_iota(jnp.int32, (1, tk), 1) <= qpos)
            ps = []
            for hh in range(HPG):
                rows = slice(hh * tq, (hh + 1) * tq)
                sh = jnp.where(mask, s[rows], NEG)
                m_old = m_ref[rows]
                m_new = jnp.maximum(m_old, jnp.max(sh, axis=-1, keepdims=True))
                p = jnp.exp2((sh - m_new) * EXP2_SCALE)
                alpha = jnp.exp2((m_old - m_new) * EXP2_SCALE)
                l_ref[rows] = alpha * l_ref[rows] + jnp.sum(p, axis=-1, keepdims=True)
                acc_ref[rows] = alpha * acc_ref[rows]
                m_ref[rows] = m_new
                ps.append(p.astype(jnp.bfloat16))
            acc_ref[...] += jnp.dot(jnp.concatenate(ps, axis=0), ks_ref[0, pl.ds(k0, tk), vcol],
                                    preferred_element_type=jnp.float32)

        n_full = q0 // tk

        def full_tile(kt, carry):
            sel_tile(kt, False)
            return carry

        lax.fori_loop(0, n_full, full_tile, 0)
        sel_tile(n_full, True)
        o_sel = acc_ref[...] / l_ref[...]

        w0 = pl.multiple_of(jnp.maximum(q0 - WINDOW, 0), tq)
        s = lax.dot_general(qg, kw_ref[0, pl.ds(w0, WINDOW + tq), kcol], _NT, preferred_element_type=jnp.float32)
        kpos = w0 + lax.broadcasted_iota(jnp.int32, (1, WINDOW + tq), 1)
        mask = (kpos <= qpos) & (kpos > qpos - WINDOW)
        ps = [_softmax_rows(s[hh * tq:(hh + 1) * tq], mask, every_row_valid=True).astype(jnp.bfloat16)
              for hh in range(HPG)]
        o_win = jnp.dot(jnp.concatenate(ps, axis=0), kw_ref[0, pl.ds(w0, WINDOW + tq), vcol],
                        preferred_element_type=jnp.float32)

        for hh in range(HPG):
            head = g * HPG + hh
            rows = slice(hh * tq, (hh + 1) * tq)
            o = (gates[:, head:head + 1] * o_cmp[rows]
                 + gates[:, N_HEADS + head:N_HEADS + head + 1] * o_sel[rows]
                 + gates[:, 2 * N_HEADS + head:2 * N_HEADS + head + 1] * o_win[rows])
            o_ref[0, :, head * HEAD_DIM:(head + 1) * HEAD_DIM] = o.astype(o_ref.dtype)


def nsa_prompt(q, gates, ckv, kvs, kvw, *, tq=NSA_Q_TILE, tk=NSA_KV_TILE):
    n, t, _ = q.shape
    assert t % tk == 0 and tk % tq == 0 and t >= WINDOW + tq and t // SEL_BLOCK <= LANES
    rows = HPG * tq
    return pl.pallas_call(
        functools.partial(_nsa_prompt_kernel, tq=tq, tk=tk),
        grid=(n, t // tq),
        in_specs=[pl.BlockSpec((1, tq, Q_W), lambda b, i: (b, i, 0)),
                  pl.BlockSpec((1, tq, LANES), lambda b, i: (b, i, 0)),
                  pl.BlockSpec((1,) + ckv.shape[1:], lambda b, i: (b, 0, 0, 0, 0)),
                  pl.BlockSpec((1, t, KV_W), lambda b, i: (b, 0, 0)),
                  pl.BlockSpec((1, t, KV_W), lambda b, i: (b, 0, 0))],
        out_specs=pl.BlockSpec((1, tq, Q_W), lambda b, i: (b, i, 0)),
        out_shape=jax.ShapeDtypeStruct((n, t, Q_W), jnp.bfloat16),
        scratch_shapes=[pltpu.VMEM((rows, 1), jnp.float32), pltpu.VMEM((rows, 1), jnp.float32),
                        pltpu.VMEM((rows, HEAD_DIM), jnp.float32)],
        compiler_params=_params("parallel", "arbitrary"),
        name="nsa_prompt",
    )(q, gates, ckv, kvs, kvw)


def _conv_kernel(a_ref, b_ref, w_ref, bias_ref, g_ref, beta_ref, o_ref, win_ref, y_ref, *, tt):
    win_ref[0:tt, :] = a_ref[0]
    win_ref[tt:tt + CONV_CTX, :] = b_ref[0]
    first = CONV_CTX - (CONV_WIDTH - 1)
    rc = min(CONV_ROWS, tt)
    for cc in range(CONV_CH // LANES):
        cols = slice(cc * LANES, (cc + 1) * LANES)
        for r0 in range(0, tt, rc):
            acc = jnp.broadcast_to(bias_ref[:, cols], (rc, LANES))
            for k in range(CONV_WIDTH):
                acc = acc + w_ref[k:k + 1, cols] * win_ref[r0 + first + k:r0 + first + k + rc, cols]
            y_ref[r0:r0 + rc, cols] = acc
    y = y_ref[...]
    yc = y - jnp.mean(y, axis=-1, keepdims=True)
    yn = yc * lax.rsqrt(jnp.mean(yc * yc, axis=-1, keepdims=True) + EPS) * g_ref[...] + beta_ref[...]
    o_ref[0] = (yn * jax.nn.sigmoid(yn)).astype(o_ref.dtype)


def conv_act(ctx, up, w_dw, b_dw, ln_g, ln_b, *, block_t=128):
    n, t, ch = up.shape
    tp = -(-t // CONV_CTX) * CONV_CTX
    tt = _row_block(tp, block_t)
    padded = jnp.concatenate([jnp.zeros((n, CONV_CTX - ctx.shape[1], ch), up.dtype), ctx, up,
                              jnp.zeros((n, tp - t, ch), up.dtype)], axis=1)
    w = jnp.zeros((CONV_CTX, ch), jnp.float32).at[:CONV_WIDTH].set(w_dw)
    step = tt // CONV_CTX
    row = lambda v: v.reshape(1, ch).astype(jnp.float32)
    out = pl.pallas_call(
        functools.partial(_conv_kernel, tt=tt),
        grid=(n, tp // tt),
        in_specs=[pl.BlockSpec((1, tt, ch), lambda b, i: (b, i, 0)),
                  pl.BlockSpec((1, CONV_CTX, ch), lambda b, i: (b, (i + 1) * step, 0)),
                  pl.BlockSpec((CONV_CTX, ch), lambda b, i: (0, 0)),
                  pl.BlockSpec((1, ch), lambda b, i: (0, 0)),
                  pl.BlockSpec((1, ch), lambda b, i: (0, 0)),
                  pl.BlockSpec((1, ch), lambda b, i: (0, 0))],
        out_specs=pl.BlockSpec((1, tt, ch), lambda b, i: (b, i, 0)),
        out_shape=jax.ShapeDtypeStruct((n, tp, ch), jnp.bfloat16),
        scratch_shapes=[pltpu.VMEM((tt + CONV_CTX, ch), jnp.float32), pltpu.VMEM((tt, ch), jnp.float32)],
        compiler_params=_params("parallel", "arbitrary"),
        name="conv_act",
    )(padded, padded, w, row(b_dw), row(ln_g), row(ln_b))
    return out[:, :t]


def _cross_attn_kernel(q_ref, kv_ref, o_ref):
    outs = []
    for h in range(MEM_HEADS):
        lo = h * MEM_HEAD_DIM
        q = q_ref[0, :, lo:lo + MEM_HEAD_DIM]
        k = kv_ref[0, :, lo:lo + MEM_HEAD_DIM].astype(jnp.bfloat16)
        v = kv_ref[0, :, MEM_W + lo:MEM_W + lo + MEM_HEAD_DIM].astype(jnp.bfloat16)
        s = lax.dot_general(q, k, _NT, preferred_element_type=jnp.float32) * MEM_SCALE
        e = jnp.exp(s - jnp.max(s, axis=-1, keepdims=True))
        p = e / jnp.sum(e, axis=-1, keepdims=True)
        outs.append(jnp.dot(p.astype(jnp.bfloat16), v, preferred_element_type=jnp.float32))
    o_ref[0] = jnp.concatenate(outs, axis=-1).astype(o_ref.dtype)


def cross_attn_core(q, kv, *, block_q=512):
    n, t, _ = q.shape
    bq = _row_block(t, block_q)
    return pl.pallas_call(
        _cross_attn_kernel,
        grid=(n, t // bq),
        in_specs=[pl.BlockSpec((1, bq, MEM_W), lambda b, i: (b, i, 0)),
                  pl.BlockSpec((1, MEM_LEN, 2 * MEM_W), lambda b, i: (b, 0, 0))],
        out_specs=pl.BlockSpec((1, bq, MEM_W), lambda b, i: (b, i, 0)),
        out_shape=jax.ShapeDtypeStruct((n, t, MEM_W), jnp.bfloat16),
        compiler_params=_params("parallel", "arbitrary"),
        name="cross_attn_core",
    )(q, kv)


def _router_kernel(x_ref, g_ref, w_ref, b_ref, h_ref, e_ref, p_ref):
    x = x_ref[...]
    h = x * lax.rsqrt(jnp.mean(x * x, axis=-1, keepdims=True) + EPS) * g_ref[...]
    h_ref[...] = h
    logits = jnp.dot(h.astype(jnp.bfloat16), w_ref[...], preferred_element_type=jnp.float32) + b_ref[...]
    lt = logits.T[:N_EXPERTS]
    ids = lax.broadcasted_iota(jnp.int32, lt.shape, 0).astype(jnp.float32)
    vals, idxs = [], []
    for _ in range(TOP_K):
        m = jnp.max(lt, axis=0, keepdims=True)
        idx = jnp.min(jnp.where(lt == m, ids, float(N_EXPERTS)), axis=0, keepdims=True)
        vals.append(m)
        idxs.append(idx)
        lt = jnp.where(ids == idx, -jnp.inf, lt)
    top_v = jnp.concatenate(vals, axis=0)
    e = jnp.exp(top_v - vals[0])
    e_ref[...] = jnp.concatenate(idxs, axis=0).astype(jnp.int32)
    p_ref[...] = e / jnp.sum(e, axis=0, keepdims=True)


def moe_router(x, gain, w_router, b_router, *, block_m=512):
    m, d = x.shape
    bm = _row_block(m, block_m)
    w = jnp.zeros((d, LANES), jnp.bfloat16).at[:, :N_EXPERTS].set(w_router.astype(jnp.bfloat16))
    b = jnp.zeros((1, LANES), jnp.float32).at[0, :N_EXPERTS].set(b_router.astype(jnp.float32))
    return pl.pallas_call(
        _router_kernel,
        grid=(m // bm,),
        in_specs=[pl.BlockSpec((bm, d), lambda i: (i, 0)),
                  pl.BlockSpec((1, d), lambda i: (0, 0)),
                  pl.BlockSpec((d, LANES), lambda i: (0, 0)),
                  pl.BlockSpec((1, LANES), lambda i: (0, 0))],
        out_specs=[pl.BlockSpec((bm, d), lambda i: (i, 0)),
                   pl.BlockSpec((TOP_K, bm), lambda i: (0, i)),
                   pl.BlockSpec((TOP_K, bm), lambda i: (0, i))],
        out_shape=[jax.ShapeDtypeStruct((m, d), jnp.float32),
                   jax.ShapeDtypeStruct((TOP_K, m), jnp.int32),
                   jax.ShapeDtypeStruct((TOP_K, m), jnp.float32)],
        compiler_params=_params("parallel"),
        name="moe_router",
    )(x, gain.reshape(1, d).astype(jnp.float32), w, b)


def _expert_kernel(blk_e_ref, blk_rows_ref, tok_cur_ref, tok_nxt_ref, h_hbm, wg_ref, wu_ref, bg_ref, bu_ref,
                   wd_ref, bd_ref, o_ref, x_stage, x_cur, sem, wg_s, wu_s, wd_s):
    i, j = pl.program_id(0), pl.program_id(1)
    rows = blk_rows_ref[i]

    def row_copy(tok, r):
        return pltpu.make_async_copy(h_hbm.at[pl.ds(tok, 1)], x_stage.at[pl.ds(r, 1)], sem.at[0])

    def for_groups(n_rows, fn):
        for s in range(MOE_BLOCK_ROWS // MOE_SUB_ROWS):
            lo = s * MOE_SUB_ROWS
            pl.when(lo < n_rows)(functools.partial(fn, lo))

    def for_group_rows(lo, fn):
        def body(r, carry):
            fn(lo + r)
            return carry
        lax.fori_loop(0, MOE_SUB_ROWS, body, 0, unroll=8)

    def start_block(tok_ref, n_rows):
        for_groups(n_rows, lambda lo: for_group_rows(lo, lambda r: row_copy(tok_ref[0, 0, r], r).start()))

    def wait_block(n_rows):
        for_groups(n_rows, lambda lo: for_group_rows(lo, lambda r: row_copy(0, r).wait()))

    def copy_group(lo):
        x_cur[lo:lo + MOE_SUB_ROWS, :] = x_stage[lo:lo + MOE_SUB_ROWS, :].astype(jnp.bfloat16)

    @pl.when(j == 0)
    def _():
        o_ref[...] = jnp.broadcast_to(bd_ref[0], o_ref.shape)

        @pl.when(i == 0)
        def _():
            start_block(tok_cur_ref, rows)

        wait_block(rows)
        for_groups(rows, copy_group)

        @pl.when(i + 1 < pl.num_programs(0))
        def _():
            start_block(tok_nxt_ref, blk_rows_ref[i + 1])

    @pl.when(rows > 0)
    def _():
        wg_s[...] = wg_ref[0].astype(jnp.bfloat16)
        wu_s[...] = wu_ref[0].astype(jnp.bfloat16)
        wd_s[...] = wd_ref[0].astype(jnp.bfloat16)

    for s in range(MOE_BLOCK_ROWS // MOE_SUB_ROWS):
        lo = s * MOE_SUB_ROWS

        @pl.when(lo < rows)
        def _():
            x = x_cur[lo:lo + MOE_SUB_ROWS, :]
            a = jnp.dot(x, wg_s[...], preferred_element_type=jnp.float32) + bg_ref[0]
            u = jnp.dot(x, wu_s[...], preferred_element_type=jnp.float32) + bu_ref[0]
            a = jnp.minimum(a, SWIGLU_LIMIT)
            u = jnp.clip(u, -SWIGLU_LIMIT, SWIGLU_LIMIT)
            act = a * jax.nn.sigmoid(SWIGLU_ALPHA * a) * (u + 1.0)
            o_ref[lo:lo + MOE_SUB_ROWS, :] += jnp.dot(act.astype(jnp.bfloat16), wd_s[...],
                                                      preferred_element_type=jnp.float32)


def moe_experts(h, slot_tok, blk_e, blk_rows, w_gu, b_gu, w_down, b_down):
    s_total = slot_tok.shape[0]
    d = h.shape[1]
    n_e, _, two_ff = w_gu.shape
    ff = two_ff // 2
    nb = s_total // MOE_BLOCK_ROWS
    nf = ff // MOE_FF_TILE
    tok3 = slot_tok.reshape(nb, 1, MOE_BLOCK_ROWS)

    def col(i, j, rows_ref):
        return jnp.where(rows_ref[i] > 0, j, nf - 1)

    grid_spec = pltpu.PrefetchScalarGridSpec(
        num_scalar_prefetch=2,
        grid=(nb, nf),
        in_specs=[
            pl.BlockSpec((1, 1, MOE_BLOCK_ROWS), lambda i, j, e, r: (i, 0, 0), memory_space=pltpu.SMEM),
            pl.BlockSpec((1, 1, MOE_BLOCK_ROWS), lambda i, j, e, r: (jnp.minimum(i + 1, nb - 1), 0, 0),
                         memory_space=pltpu.SMEM),
            pl.BlockSpec(memory_space=pl.ANY),
            pl.BlockSpec((1, d, MOE_FF_TILE), lambda i, j, e, r: (e[i], 0, col(i, j, r))),
            pl.BlockSpec((1, d, MOE_FF_TILE), lambda i, j, e, r: (e[i], 0, nf + col(i, j, r))),
            pl.BlockSpec((1, 1, MOE_FF_TILE), lambda i, j, e, r: (e[i], 0, col(i, j, r))),
            pl.BlockSpec((1, 1, MOE_FF_TILE), lambda i, j, e, r: (e[i], 0, nf + col(i, j, r))),
            pl.BlockSpec((1, MOE_FF_TILE, d), lambda i, j, e, r: (e[i], col(i, j, r), 0)),
            pl.BlockSpec((1, 1, d), lambda i, j, e, r: (e[i], 0, 0)),
        ],
        out_specs=pl.BlockSpec((MOE_BLOCK_ROWS, d), lambda i, j, e, r: (i, 0)),
        scratch_shapes=[pltpu.VMEM((MOE_BLOCK_ROWS, d), jnp.float32),
                        pltpu.VMEM((MOE_BLOCK_ROWS, d), jnp.bfloat16),
                        pltpu.SemaphoreType.DMA((1,)),
                        pltpu.VMEM((d, MOE_FF_TILE), jnp.bfloat16),
                        pltpu.VMEM((d, MOE_FF_TILE), jnp.bfloat16),
                        pltpu.VMEM((MOE_FF_TILE, d), jnp.bfloat16)],
    )
    b_gu3 = b_gu.reshape(n_e, 1, two_ff)
    return pl.pallas_call(
        _expert_kernel,
        grid_spec=grid_spec,
        out_shape=jax.ShapeDtypeStruct((s_total, d), jnp.float32),
        compiler_params=_params("arbitrary", "arbitrary"),
        name="moe_experts",
    )(blk_e, blk_rows, tok3, tok3, h, w_gu, w_gu, b_gu3, b_gu3, w_down, b_down.reshape(n_e, 1, d))


def moe_layer(x_groups, gain, w_router, b_router, w_gu, b_gu, w_down, b_down, g_final):
    routed = [moe_router(x, gain, w_router, b_router) for x in x_groups]
    h = jnp.concatenate([r[0] for r in routed], axis=0)
    top_e = jnp.concatenate([r[1] for r in routed], axis=1)
    top_p = jnp.concatenate([r[2] for r in routed], axis=1)
    t = h.shape[0]
    bm = MOE_BLOCK_ROWS
    n_blocks = -(-(t * TOP_K) // bm) + N_EXPERTS
    onehot = (top_e[:, :, None] == jnp.arange(N_EXPERTS)[None, None, :]).astype(jnp.int32)
    per_tok = jnp.sum(onehot, axis=0)
    before = jnp.cumsum(per_tok, axis=0) - per_tok
    counts = jnp.sum(per_tok, axis=0)
    blocks_e = (counts + bm - 1) // bm
    blk_end = jnp.cumsum(blocks_e)
    blk_start = blk_end - blocks_e
    first_slot = before + (blk_start * bm)[None, :]
    slot = jnp.sum(onehot * first_slot[None], axis=-1).astype(jnp.int32)
    tok_ids = jnp.broadcast_to(jnp.arange(t, dtype=jnp.int32)[None], slot.shape)
    slot_tok = jnp.zeros((n_blocks * bm,), jnp.int32).at[slot.reshape(-1)].set(tok_ids.reshape(-1))
    blk = jnp.arange(n_blocks)
    blk_e = jnp.minimum(jnp.searchsorted(blk_end, blk, side='right'), N_EXPERTS - 1).astype(jnp.int32)
    used = blk < blk_end[-1]
    last_e = blk_e[jnp.maximum(blk_end[-1] - 1, 0)]
    blk_e = jnp.where(used, blk_e, last_e).astype(jnp.int32)
    blk_rows = jnp.where(used, jnp.clip(counts[blk_e] - (blk - blk_start[blk_e]) * bm, 0, bm), 0).astype(jnp.int32)
    outs = moe_experts(h, slot_tok, blk_e, blk_rows, w_gu, b_gu, w_down, b_down)
    ys, start = [], 0
    for x in x_groups:
        m = x.shape[0]
        ys.append(moe_combine_norm(x, outs, slot[:, start:start + m], top_p[:, start:start + m], g_final))
        start += m
    return ys


def _combine_kernel(idx_cur_ref, idx_nxt_ref, x_ref, p_ref, g_ref, outs_hbm, o_ref, buf, sem, *, bm):
    i = pl.program_id(0)
    cur = lax.rem(i, 2)
    n_rows = TOP_K * bm

    def row_copy(slot, dst, r):
        return pltpu.make_async_copy(outs_hbm.at[pl.ds(slot, 1)], buf.at[dst, pl.ds(r, 1)], sem.at[dst])

    def for_rows(fn):
        def body(r, carry):
            fn(r)
            return carry
        lax.fori_loop(0, n_rows, body, 0, unroll=8)

    @pl.when(i == 0)
    def _():
        for_rows(lambda r: row_copy(idx_cur_ref[0, 0, r], cur, r).start())

    for_rows(lambda r: row_copy(0, cur, r).wait())

    @pl.when(i + 1 < pl.num_programs(0))
    def _():
        for_rows(lambda r: row_copy(idx_nxt_ref[0, 0, r], 1 - cur, r).start())

    x = x_ref[...]
    for k in range(TOP_K):
        x = x + p_ref[:, k:k + 1] * buf[cur, k * bm:(k + 1) * bm, :]
    o_ref[...] = x * lax.rsqrt(jnp.mean(x * x, axis=-1, keepdims=True) + EPS) * g_ref[...]


def moe_combine_norm(x, outs, slot, gate, gain, *, block_m=128):
    m, d = x.shape
    bm = _row_block(m, block_m)
    nt = m // bm
    idx = slot.reshape(TOP_K, nt, bm).transpose(1, 0, 2).reshape(nt, 1, TOP_K * bm)
    return pl.pallas_call(
        functools.partial(_combine_kernel, bm=bm),
        grid=(nt,),
        in_specs=[pl.BlockSpec((1, 1, TOP_K * bm), lambda i: (i, 0, 0), memory_space=pltpu.SMEM),
                  pl.BlockSpec((1, 1, TOP_K * bm), lambda i: (jnp.minimum(i + 1, nt - 1), 0, 0),
                               memory_space=pltpu.SMEM),
                  pl.BlockSpec((bm, d), lambda i: (i, 0)),
                  pl.BlockSpec((bm, TOP_K), lambda i: (i, 0)),
                  pl.BlockSpec((1, d), lambda i: (0, 0)),
                  pl.BlockSpec(memory_space=pl.ANY)],
        out_specs=pl.BlockSpec((bm, d), lambda i: (i, 0)),
        out_shape=jax.ShapeDtypeStruct((m, d), jnp.float32),
        scratch_shapes=[pltpu.VMEM((2, TOP_K * bm, d), jnp.float32), pltpu.SemaphoreType.DMA((2,))],
        compiler_params=_params("arbitrary"),
        name="moe_combine_norm",
    )(idx, idx, x, gate.T, gain.reshape(1, d).astype(jnp.float32), outs)


def _hist_compress_kernel(pt_ref, *refs):
    pages, (w1_ref, pe_ref, w2_ref, o_ref, slab_ref) = refs[:CMP_PAGES + 1], refs[CMP_PAGES + 1:]
    c = CMP_PAGES * CHUNKS_PER_PAGE
    last = lax.broadcasted_iota(jnp.int32, (c, 1), 0) == c - 1

    slab_rows = CMP_STRIDE * KV_ROWS
    regs_per_chunk = slab_rows // 8
    for k in range(CMP_PAGES + 1):
        chunks = CHUNKS_PER_PAGE if k < CMP_PAGES else 1
        if k == CMP_PAGES:
            for s in range(slab_rows):
                slab_ref[s * SLAB_PITCH + c:s * SLAB_PITCH + c + 8, :] = jnp.zeros((8, HEAD_DIM), jnp.float32)
        for cc in range(chunks):
            for sg in range(regs_per_chunk):
                v = cc * regs_per_chunk + sg
                slab_ref[pl.ds(8 * sg * SLAB_PITCH + k * CHUNKS_PER_PAGE + cc, 8, stride=SLAB_PITCH), :] = (
                    pages[k][v * 8:(v + 1) * 8, :])

    def rows(sub, lo, n):
        return jnp.concatenate([slab_ref[(i * KV_ROWS + sub) * SLAB_PITCH + lo:(i * KV_ROWS + sub) * SLAB_PITCH + lo + n, :]
                                for i in range(CMP_STRIDE)], axis=1)

    for j in range(2):
        w_a = w1_ref[j, :, :HEAD_DIM]
        w_b = w1_ref[j, :, HEAD_DIM:]
        pe = pe_ref[j]
        bias = (jnp.dot(pe[:, :CMP_HALF], w_a, preferred_element_type=jnp.float32)
                + jnp.dot(pe[:, CMP_HALF:], w_b, preferred_element_type=jnp.float32))[0:1]
        bases = [j * N_KV + g for g in range(N_KV)]
        x = jnp.concatenate([rows(base, 0, c) for base in bases], axis=0).astype(jnp.bfloat16)
        x_next = jnp.concatenate([rows(base, c, CHUNKS_PER_PAGE) for base in bases], axis=0).astype(jnp.bfloat16)
        ab = jnp.dot(x, w1_ref[j], preferred_element_type=jnp.float32)
        a, b = ab[:, :HEAD_DIM], ab[:, HEAD_DIM:]
        b_tail = jnp.dot(x_next, w_b, preferred_element_type=jnp.float32)
        for g in range(N_KV):
            b_next = jnp.where(last, b_tail[g * CHUNKS_PER_PAGE:g * CHUNKS_PER_PAGE + 1],
                               pltpu.roll(b[g * c:(g + 1) * c], c - 1, axis=0))
            h = jax.nn.gelu(a[g * c:(g + 1) * c] + b_next + bias)
            o_ref[0, j, g] = jnp.dot(h.astype(jnp.bfloat16), w2_ref[j],
                                     preferred_element_type=jnp.float32).astype(o_ref.dtype)


def compress_history(pool, page_table, cmp_pe, cmp_w1, cmp_w2):
    n, n_pages = page_table.shape
    assert n_pages % CMP_PAGES == 0
    c = n_pages * CHUNKS_PER_PAGE
    w1, pe, w2 = _compress_weights(cmp_pe, cmp_w1, cmp_w2)

    def page_spec(k):
        return pl.BlockSpec((PAGE_SIZE * KV_ROWS, HEAD_DIM),
                            lambda b, s, pt: (pt[b * n_pages + jnp.minimum(s * CMP_PAGES + k, n_pages - 1)], 0))

    grid_spec = pltpu.PrefetchScalarGridSpec(
        num_scalar_prefetch=1,
        grid=(n, n_pages // CMP_PAGES),
        in_specs=[page_spec(k) for k in range(CMP_PAGES + 1)] + [
            pl.BlockSpec(w1.shape, lambda b, s, pt: (0, 0, 0)),
            pl.BlockSpec(pe.shape, lambda b, s, pt: (0, 0, 0)),
            pl.BlockSpec(w2.shape, lambda b, s, pt: (0, 0, 0))],
        out_specs=pl.BlockSpec((1, 2, N_KV, CMP_PAGES * CHUNKS_PER_PAGE, HEAD_DIM), lambda b, s, pt: (b, 0, 0, s, 0)),
        scratch_shapes=[pltpu.VMEM((CMP_STRIDE * KV_ROWS * SLAB_PITCH, HEAD_DIM), jnp.float32)],
    )
    return pl.pallas_call(
        _hist_compress_kernel,
        grid_spec=grid_spec,
        out_shape=jax.ShapeDtypeStruct((n, 2, N_KV, c, HEAD_DIM), jnp.bfloat16),
        compiler_params=_params("parallel", "arbitrary"),
        name="compress_history",
    )(page_table.reshape(-1), *([pool] * (CMP_PAGES + 1)), w1, pe, w2)


def _nsa_sample_dense_kernel(q_ref, ckv_ref, kw_ref, ocmp_ref, owin_ref, idx_ref, val_ref, *, past, n_cmp, n_sel):
    rows = HPG * DEC_PAD
    tok = lax.bitwise_and(lax.broadcasted_iota(jnp.int32, (rows, 1), 0), DEC_PAD - 1)
    qpos = past + tok
    qpos_l = past + lax.broadcasted_iota(jnp.int32, (1, LANES), 1)
    c_all = ckv_ref.shape[3]
    n_sel_pad = -(-n_sel // 8) * 8
    wlen = kw_ref.shape[1]
    for g in range(N_KV):
        qg = q_ref[0, g]
        s = lax.dot_general(qg, ckv_ref[0, 0, g], _NT, preferred_element_type=jnp.float32)
        cidx = lax.broadcasted_iota(jnp.int32, (1, c_all), 1)
        vis = (cidx * CMP_STRIDE + (CMP_LEN - 1) <= qpos) & (cidx < n_cmp)
        p = _softmax_rows(s, vis)
        ocmp_ref[0, g] = jnp.dot(p.astype(jnp.bfloat16), ckv_ref[0, 1, g], preferred_element_type=jnp.float32)
        p_grp = p[0:DEC_PAD]
        for hh in range(1, HPG):
            p_grp = p_grp + p[hh * DEC_PAD:(hh + 1) * DEC_PAD]
        p_grp = jnp.concatenate([p_grp, jnp.zeros((LANES - DEC_PAD, c_all), jnp.float32)], axis=0)
        c0 = lax.broadcasted_iota(jnp.int32, (n_sel_pad, c_all), 1) * CMP_STRIDE
        s0 = lax.broadcasted_iota(jnp.int32, (n_sel_pad, c_all), 0) * SEL_BLOCK
        overlap_t = ((c0 < s0 + SEL_BLOCK) & (c0 + CMP_LEN > s0)).astype(jnp.bfloat16)
        score_t = lax.dot_general(overlap_t, p_grp.astype(jnp.bfloat16), _NT, preferred_element_type=jnp.float32)
        blk = lax.broadcasted_iota(jnp.int32, score_t.shape, 0)
        cur = lax.shift_right_logical(qpos_l, SEL_BLOCK.bit_length() - 1)
        avail = (blk <= cur) & (blk < n_sel)
        forced = avail & ((blk == 0) | (blk == cur) | (blk == cur - 1))
        work = jnp.where(forced, jnp.inf, jnp.where(avail, score_t, -jnp.inf))
        blk_f = blk.astype(jnp.float32)
        idxs, vals = [], []
        for _ in range(SEL_TOPK):
            m = jnp.max(work, axis=0, keepdims=True)
            first = jnp.min(jnp.where(work == m, blk_f, float(n_sel_pad)), axis=0, keepdims=True)
            idxs.append(first)
            vals.append(jnp.where(m > -jnp.inf, 1.0, 0.0))
            work = jnp.where(blk_f == first, -jnp.inf, work)
        idx_ref[0, g] = jnp.concatenate(idxs, axis=0).astype(jnp.int32)
        val_ref[0, g] = jnp.concatenate(vals, axis=0).astype(jnp.int32)

        kcol = slice(g * HEAD_DIM, (g + 1) * HEAD_DIM)
        vcol = slice((N_KV + g) * HEAD_DIM, (N_KV + g + 1) * HEAD_DIM)
        s = lax.dot_general(qg, kw_ref[0, :, kcol].astype(jnp.bfloat16), _NT, preferred_element_type=jnp.float32)
        kpos = past - WINDOW + lax.broadcasted_iota(jnp.int32, (1, wlen), 1)
        p = _softmax_rows(s, (kpos <= qpos) & (kpos > qpos - WINDOW))
        owin_ref[0, g] = jnp.dot(p.astype(jnp.bfloat16), kw_ref[0, :, vcol].astype(jnp.bfloat16),
                                 preferred_element_type=jnp.float32)


def nsa_sample_dense(qd, ckv, kw, *, past, n_cmp, n_sel):
    n = qd.shape[0]
    rows = HPG * DEC_PAD
    o_shape = jax.ShapeDtypeStruct((n, N_KV, rows, HEAD_DIM), jnp.float32)
    i_shape = jax.ShapeDtypeStruct((n, N_KV, SEL_TOPK, LANES), jnp.int32)
    o_spec = pl.BlockSpec((1, N_KV, rows, HEAD_DIM), lambda b: (b, 0, 0, 0))
    i_spec = pl.BlockSpec((1, N_KV, SEL_TOPK, LANES), lambda b: (b, 0, 0, 0))
    return pl.pallas_call(
        functools.partial(_nsa_sample_dense_kernel, past=past, n_cmp=n_cmp, n_sel=n_sel),
        grid=(n,),
        in_specs=[pl.BlockSpec((1, N_KV, rows, HEAD_DIM), lambda b: (b, 0, 0, 0)),
                  pl.BlockSpec((1,) + ckv.shape[1:], lambda b: (b, 0, 0, 0, 0)),
                  pl.BlockSpec((1,) + kw.shape[1:], lambda b: (b, 0, 0))],
        out_specs=[o_spec, o_spec, i_spec, i_spec],
        out_shape=[o_shape, o_shape, i_shape, i_shape],
        compiler_params=_params("parallel"),
        name="nsa_sample_dense",
    )(qd, ckv, kw)


def _nsa_sample_sel_kernel(idx_ref, val_ref, pt_ref, q_ref, new_ref, *refs, tq, past, n_past_blk):
    blocks, o_ref = refs[:N_KV * SEL_TOPK], refs[N_KV * SEL_TOPK]
    b, t = pl.program_id(0), pl.program_id(1)
    qpos = past + t
    lane = lax.broadcasted_iota(jnp.int32, (1, SEL_TOPK * SEL_BLOCK), 1)
    slot_of_lane = lax.shift_right_logical(lane, SEL_BLOCK.bit_length() - 1)
    for g in range(N_KV):
        base = ((b * tq + t) * N_KV + g) * SEL_TOPK
        ks, vs = [], []
        tokpos = jnp.zeros(lane.shape, jnp.int32)
        ok = jnp.zeros(lane.shape, jnp.int32)
        for s in range(SEL_TOPK):
            blk = idx_ref[base + s]
            from_pool = blk < n_past_blk
            pool_blk = blocks[g * SEL_TOPK + s]
            k_rows = pl.ds(g, SEL_BLOCK, stride=KV_ROWS)
            v_rows = pl.ds(N_KV + g, SEL_BLOCK, stride=KV_ROWS)
            ks.append(jnp.where(from_pool, pool_blk[k_rows, :], new_ref[0, k_rows, :]))
            vs.append(jnp.where(from_pool, pool_blk[v_rows, :], new_ref[0, v_rows, :]))
            here = slot_of_lane == s
            tokpos = jnp.where(here, blk * SEL_BLOCK, tokpos)
            ok = jnp.where(here, val_ref[base + s], ok)
        tokpos = tokpos + lax.bitwise_and(lane, SEL_BLOCK - 1)
        mask = (ok > 0) & (tokpos <= qpos)
        k = jnp.concatenate(ks, axis=0).astype(jnp.bfloat16)
        v = jnp.concatenate(vs, axis=0).astype(jnp.bfloat16)
        s = lax.dot_general(q_ref[0, 0, g], k, _NT, preferred_element_type=jnp.float32)
        p = _softmax_rows(s, mask)
        o_ref[0, 0, g] = jnp.dot(p.astype(jnp.bfloat16), v, preferred_element_type=jnp.float32)


def nsa_sample_sel(qs, idx, valid, page_table, pool, new_blocks, *, past):
    n, tq = qs.shape[:2]
    n_pages = page_table.shape[1]
    bpp = PAGE_SIZE // SEL_BLOCK
    n_past_blk = n_pages * bpp

    def block_spec(g, s):
        def index(b, t, idx_ref, val_ref, pt_ref):
            blk = jnp.minimum(idx_ref[((b * tq + t) * N_KV + g) * SEL_TOPK + s], n_past_blk - 1)
            return (pt_ref[b * n_pages + blk // bpp] * bpp + blk % bpp, 0)
        return pl.BlockSpec((SEL_BLOCK * KV_ROWS, HEAD_DIM), index)

    grid_spec = pltpu.PrefetchScalarGridSpec(
        num_scalar_prefetch=3,
        grid=(n, tq),
        in_specs=[pl.BlockSpec((1, 1, N_KV, 16, HEAD_DIM), lambda b, t, *_: (b, t, 0, 0, 0)),
                  pl.BlockSpec((1, SEL_BLOCK * KV_ROWS, HEAD_DIM), lambda b, t, *_: (b, 0, 0))]
                 + [block_spec(g, s) for g in range(N_KV) for s in range(SEL_TOPK)],
        out_specs=pl.BlockSpec((1, 1, N_KV, 16, HEAD_DIM), lambda b, t, *_: (b, t, 0, 0, 0)),
    )
    return pl.pallas_call(
        functools.partial(_nsa_sample_sel_kernel, tq=tq, past=past, n_past_blk=n_past_blk),
        grid_spec=grid_spec,
        out_shape=jax.ShapeDtypeStruct((n, tq, N_KV, 16, HEAD_DIM), jnp.float32),
        compiler_params=_params("parallel", "arbitrary"),
        name="nsa_sample_sel",
    )(idx.reshape(-1), valid.reshape(-1), page_table.reshape(-1), qs, new_blocks,
      *([pool] * (N_KV * SEL_TOPK)))


def nsa_sample(q, gates, kvs_new, kvw_new, pool_cmp, pool_sel, win_cache, page_table, cmp_pe, cmp_w1, cmp_w2):
    n, tq, _ = q.shape
    past = page_table.shape[1] * PAGE_SIZE
    assert tq <= DEC_PAD and tq <= SEL_BLOCK and win_cache.shape[1] == WINDOW
    assert (past + tq) // CMP_STRIDE * CMP_STRIDE == past
    n_cmp = past // CMP_STRIDE - 1
    n_sel = -(-(past + tq) // SEL_BLOCK)
    ckv = compress_history(pool_cmp, page_table, cmp_pe, cmp_w1, cmp_w2)
    q5 = q.reshape(n, tq, N_KV, HPG, HEAD_DIM)
    qd = jnp.pad(q5.transpose(0, 2, 3, 1, 4), ((0, 0), (0, 0), (0, 0), (0, DEC_PAD - tq), (0, 0)))
    qd = qd.reshape(n, N_KV, HPG * DEC_PAD, HEAD_DIM)
    kw_all = jnp.concatenate([win_cache, kvw_new], axis=1)
    kw = jnp.pad(kw_all, ((0, 0), (0, -(-(WINDOW + DEC_PAD) // LANES) * LANES - WINDOW - tq), (0, 0)))
    o_cmp, o_win, idx, valid = nsa_sample_dense(qd, ckv, kw, past=past, n_cmp=n_cmp, n_sel=n_sel)
    idx = idx[..., :tq].transpose(0, 3, 1, 2)
    valid = valid[..., :tq].transpose(0, 3, 1, 2)
    qs = jnp.pad(q5, ((0, 0), (0, 0), (0, 0), (0, 16 - HPG), (0, 0)))
    new_blocks = jnp.pad(kvs_new, ((0, 0), (0, SEL_BLOCK - tq), (0, 0))).reshape(n, SEL_BLOCK * KV_ROWS, HEAD_DIM)
    o_sel = nsa_sample_sel(qs, idx, valid, page_table, pool_sel, new_blocks, past=past)[:, :, :, :HPG]
    unpack = lambda o: o.reshape(n, N_KV, HPG, DEC_PAD, HEAD_DIM)[:, :, :, :tq].transpose(0, 3, 1, 2, 4)
    g = gates[..., :GATE_W].reshape(n, tq, 3, N_KV, HPG)[..., None]
    o = g[:, :, 0] * unpack(o_cmp) + g[:, :, 1] * o_sel + g[:, :, 2] * unpack(o_win)
    return o.reshape(n, tq, Q_W), kw_all


def prepare_weights(w_in, w_nsa_o, w_conv_o, w_out, w_xq, w_xkv, w_xo):
    bf = jnp.bfloat16
    w = w_in.astype(bf)
    d = w.shape[0]
    s = IN_SPLITS
    return dict(
        q=w[:, :s[0]], kv=w[:, s[0]:s[3]],
        gates=jnp.zeros((d, LANES), bf).at[:, :GATE_W].set(w[:, s[3]:s[4]]),
        glu_a=w[:, s[4]:s[4] + CONV_CH], glu_b=w[:, s[4] + CONV_CH:s[5]], merge=w[:, s[5]:],
        nsa_o=w_nsa_o.astype(bf), conv_o=w_conv_o.astype(bf), out=w_out.astype(bf),
        xq=w_xq.astype(bf), xkv=w_xkv.astype(bf), xo=w_xo.astype(bf))


def front(x2, g_mix, wts):
    q = fused_linear(x2, [wts['q']], gain=g_mix, out_dtype=jnp.bfloat16, name="front_q")
    kv = fused_linear(x2, [wts['kv']], gain=g_mix, name="front_kv")
    gates = fused_linear(x2, [wts['gates']], gain=g_mix, epilogue=_sigmoid_epilogue, name="front_gates")
    up = fused_linear(x2, [wts['glu_a'], wts['glu_b']], gain=g_mix, epilogue=_glu_epilogue, name="front_glu")
    mg = fused_linear(x2, [wts['merge']], gain=g_mix, epilogue=_sigmoid_epilogue, name="front_merge")
    return q, kv, gates, up, mg


def mixer_tail(x2, o, act, mg, wts):
    merged = merge_branches(o, act, wts['nsa_o'], wts['conv_o'], mg)
    return fused_linear(merged, [wts['out']], tiles=[x2], epilogue=_residual_epilogue, name="mixer_out_proj")


def cross_attn_block(x2, n, mkv, g_xattn, wts):
    m = x2.shape[0]
    qx = fused_linear(x2, [wts['xq']], gain=g_xattn, out_dtype=jnp.bfloat16, name="xattn_q")
    oc = cross_attn_core(qx.reshape(n, m // n, MEM_W), mkv)
    return fused_linear(oc.reshape(m, MEM_W), [wts['xo']], tiles=[x2], epilogue=_residual_epilogue, name="xattn_o")


def kernel(x_prompt, x_sample, cache_kv_cmp, cache_kv_sel, cache_kv_win, cache_conv, cache_mem_kv, page_table,
           mem_prompt, g_mix, w_in, cmp_pe, cmp_w1, cmp_w2, w_nsa_o, w_dw, b_dw, ln_conv_g, ln_conv_b, w_conv_o,
           w_out, g_xattn, g_mem, w_xq, w_xkv, w_xo, g_moe, w_router, b_router, w_gu, b_gu, w_down, b_down, g_final):
    nb, t, d = x_prompt.shape
    nd, tq = x_sample.shape[:2]
    l = 0
    bf = jnp.bfloat16
    wts = prepare_weights(w_in[l], w_nsa_o[l], w_conv_o[l], w_out[l], w_xq[l], w_xkv[l], w_xo[l])
    kv_shape = lambda n_, t_: (n_, t_, 2, N_KV, HEAD_DIM)

    xp2 = x_prompt.reshape(nb * t, d)
    q, kv, gates, up_new, mg = front(xp2, g_mix[l], wts)
    kvc, kvs, kvw = (kv[:, i * KV_W:(i + 1) * KV_W].reshape(nb, t, KV_W) for i in range(3))
    ckv = compress_prompt(kvc, cmp_pe[l], cmp_w1[l], cmp_w2[l])
    o = nsa_prompt(q.reshape(nb, t, Q_W), gates.reshape(nb, t, LANES), ckv, kvs.astype(bf), kvw.astype(bf))
    up_new = up_new.reshape(nb, t, CONV_CH)
    act = conv_act(jnp.zeros((nb, CONV_WIDTH - 1, CONV_CH), jnp.float32), up_new, w_dw[l], b_dw[l],
                   ln_conv_g[l], ln_conv_b[l])
    xp = mixer_tail(xp2, o.reshape(nb * t, Q_W), act.reshape(nb * t, CONV_CH), mg, wts)
    mkv = fused_linear(mem_prompt.reshape(nb * MEM_LEN, d), [wts['xkv']], gain=g_mem[l], name="mem_kv")
    xp = cross_attn_block(xp, nb, mkv.reshape(nb, MEM_LEN, 2 * MEM_W), g_xattn[l], wts)
    p_kv_cmp, p_kv_sel = kvc.reshape(kv_shape(nb, t))[None], kvs.reshape(kv_shape(nb, t))[None]
    p_kv_win = kvw.reshape(kv_shape(nb, t))[:, t - min(WINDOW, t):][None]
    p_conv = jnp.pad(up_new, ((0, 0), (CONV_WIDTH - 1, 0), (0, 0)))[:, t:][None]
    p_mem = mkv.reshape(nb, MEM_LEN, 2, MEM_HEADS, MEM_HEAD_DIM)[None]

    xs2 = x_sample.reshape(nd * tq, d)
    q, kv, gates, up_new, mg = front(xs2, g_mix[l], wts)
    kvc, kvs, kvw = (kv[:, i * KV_W:(i + 1) * KV_W].reshape(nd, tq, KV_W) for i in range(3))
    o, win_all = nsa_sample(q.reshape(nd, tq, Q_W), gates.reshape(nd, tq, LANES), kvs, kvw,
                            cache_kv_cmp[l].reshape(-1, HEAD_DIM), cache_kv_sel[l].reshape(-1, HEAD_DIM),
                            cache_kv_win[l].reshape(nd, -1, KV_W), page_table, cmp_pe[l], cmp_w1[l], cmp_w2[l])
    win_new = win_all[:, -min(WINDOW, win_all.shape[1]):].reshape(kv_shape(nd, min(WINDOW, win_all.shape[1])))
    kvc, kvs = kvc.reshape(kv_shape(nd, tq)), kvs.reshape(kv_shape(nd, tq))
    up_new = up_new.reshape(nd, tq, CONV_CH)
    act = conv_act(cache_conv[l], up_new, w_dw[l], b_dw[l], ln_conv_g[l], ln_conv_b[l])
    xs = mixer_tail(xs2, o.reshape(nd * tq, Q_W).astype(bf), act.reshape(nd * tq, CONV_CH), mg, wts)
    xs = cross_attn_block(xs, nd, cache_mem_kv[l].reshape(nd, MEM_LEN, 2 * MEM_W), g_xattn[l], wts)
    s_conv = jnp.concatenate([cache_conv[l], up_new], axis=1)[:, tq:][None]

    y_prompt, y_sample = moe_layer([xp, xs], g_moe[l], w_router[l], b_router[l], w_gu[l], b_gu[l], w_down[l],
                                   b_down[l], g_final)
    y_prompt, y_sample = y_prompt.reshape(nb, t, d), y_sample.reshape(nd, tq, d)
    return (y_prompt, y_sample, p_kv_cmp, p_kv_sel, p_kv_win, p_conv, p_mem, kvc[None], kvs[None], win_new[None],
            s_conv)
```

```python
import functools

import jax
import jax.numpy as jnp
from jax import lax
from jax.experimental import pallas as pl
from jax.experimental.pallas import tpu as pltpu

D_MODEL = 2048
PAGE_SIZE = 128

N_HEADS = 16
HEAD_DIM = 128
N_KV = 2
HPG = N_HEADS // N_KV
CMP_STRIDE = 16
CMP_LEN = 2 * CMP_STRIDE
CMP_HALF = CMP_STRIDE * HEAD_DIM
SEL_BLOCK = 64
SEL_TOPK = 16
WINDOW = 512
SCALE = HEAD_DIM ** -0.5
EXP2_SCALE = SCALE * 1.4426950408889634
CONV_CH = D_MODEL // 2
CONV_WIDTH = 31
MEM_LEN = 256
MEM_HEADS = 4
MEM_HEAD_DIM = 128
MEM_W = MEM_HEADS * MEM_HEAD_DIM
MEM_SCALE = MEM_HEAD_DIM ** -0.5
N_EXPERTS = 32
TOP_K = 4
SWIGLU_LIMIT = 7.0
SWIGLU_ALPHA = 1.702
EPS = 1e-6
Q_W = N_HEADS * HEAD_DIM
KV_W = 2 * N_KV * HEAD_DIM
KV_ROWS = KV_W // HEAD_DIM
GATE_W = 3 * N_HEADS
GLU_W = 2 * CONV_CH
MERGE_W = 2 * D_MODEL
IN_SPLITS = [Q_W, Q_W + KV_W, Q_W + 2 * KV_W, Q_W + 3 * KV_W, Q_W + 3 * KV_W + GATE_W, Q_W + 3 * KV_W + GATE_W + GLU_W]

V7X_VMEM_BYTES = 64 * 1024 * 1024
VMEM_LIMIT_BYTES = V7X_VMEM_BYTES - 8 * 1024 * 1024
LANES = 128

MOE_BLOCK_ROWS = 1152
MOE_FF_TILE = 256
NSA_Q_TILE = 128
NSA_KV_TILE = 1024
CONV_CTX = 32
CONV_ROWS = 64
CHUNKS_PER_PAGE = PAGE_SIZE // CMP_STRIDE
CMP_PAGES = 32
SLAB_PITCH = CMP_PAGES * CHUNKS_PER_PAGE + 8
DEC_PAD = 8

NEG = -1e30
_NT = (((1,), (1,)), ((), ()))


def _params(*semantics):
    return pltpu.CompilerParams(dimension_semantics=semantics, vmem_limit_bytes=VMEM_LIMIT_BYTES)


def _row_block(rows, target):
    if rows <= target:
        return rows
    for b in range(target, 15, -1):
        if rows % b == 0 and b % 16 == 0:
            return b
    return rows


def _linear_kernel(*refs, n_w, norm, n_tile, n_row, epilogue):
    x_ref = refs[0]
    k = 1
    g_ref = refs[k] if norm else None
    k += int(norm)
    w_refs = refs[k:k + n_w]
    k += n_w
    tile_refs = refs[k:k + n_tile]
    k += n_tile
    row_refs = refs[k:k + n_row]
    k += n_row
    o_ref, h_ref = refs[k], refs[k + 1]

    @pl.when(pl.program_id(1) == 0)
    def _():
        x = x_ref[...].astype(jnp.float32)
        if norm:
            x = x * lax.rsqrt(jnp.mean(x * x, axis=-1, keepdims=True) + EPS) * g_ref[...]
        h_ref[...] = x.astype(jnp.bfloat16)

    h = h_ref[...]
    zs = [jnp.dot(h, w[...], preferred_element_type=jnp.float32) for w in w_refs]
    o_ref[...] = epilogue(zs, [t[...] for t in tile_refs], [r[...] for r in row_refs]).astype(o_ref.dtype)


def fused_linear(x, ws, *, gain=None, tiles=(), rows=(), epilogue=None, out_dtype=jnp.float32,
                 block_m=1024, block_n=512, name="fused_linear"):
    m, k = x.shape
    n = ws[0].shape[1]
    bm = _row_block(m, block_m)
    bn = min(block_n, n)
    assert m % bm == 0 and n % bn == 0 and all(w.shape == (k, n) for w in ws)
    if epilogue is None:
        epilogue = _first_epilogue
    norm = gain is not None
    in_specs = [pl.BlockSpec((bm, k), lambda i, j: (i, 0))]
    args = [x]
    if norm:
        in_specs.append(pl.BlockSpec((1, k), lambda i, j: (0, 0)))
        args.append(gain.reshape(1, k).astype(jnp.float32))
    in_specs += [pl.BlockSpec((k, bn), lambda i, j: (0, j)) for _ in ws]
    args += list(ws)
    in_specs += [pl.BlockSpec((bm, bn), lambda i, j: (i, j)) for _ in tiles]
    args += list(tiles)
    in_specs += [pl.BlockSpec((1, bn), lambda i, j: (0, j)) for _ in rows]
    args += list(rows)
    return pl.pallas_call(
        functools.partial(_linear_kernel, n_w=len(ws), norm=norm, n_tile=len(tiles), n_row=len(rows),
                          epilogue=epilogue),
        grid=(m // bm, n // bn),
        in_specs=in_specs,
        out_specs=pl.BlockSpec((bm, bn), lambda i, j: (i, j)),
        out_shape=jax.ShapeDtypeStruct((m, n), out_dtype),
        scratch_shapes=[pltpu.VMEM((bm, k), jnp.bfloat16)],
        compiler_params=_params("parallel", "arbitrary"),
        name=name,
    )(*args)


def _first_epilogue(zs, ts, rs):
    return zs[0]


def _glu_epilogue(zs, ts, rs):
    return zs[0] * jax.nn.sigmoid(zs[1])


def _sigmoid_epilogue(zs, ts, rs):
    return jax.nn.sigmoid(zs[0])


def _residual_epilogue(zs, ts, rs):
    return ts[0] + zs[0]


def _merge_kernel(o_ref, a_ref, wo_ref, wc_ref, ga_ref, gc_ref, out_ref):
    y_att = jnp.dot(o_ref[...], wo_ref[...], preferred_element_type=jnp.float32)
    y_conv = jnp.dot(a_ref[...], wc_ref[...], preferred_element_type=jnp.float32)
    out_ref[...] = (ga_ref[...] * y_att + gc_ref[...] * y_conv).astype(out_ref.dtype)


def merge_branches(o, act, w_nsa_o, w_conv_o, merge_gates, *, block_m=1024, block_n=512):
    m = o.shape[0]
    d = w_nsa_o.shape[1]
    bm = _row_block(m, block_m)
    bn = min(block_n, d)
    nj = d // bn
    return pl.pallas_call(
        _merge_kernel,
        grid=(m // bm, nj),
        in_specs=[pl.BlockSpec((bm, o.shape[1]), lambda i, j: (i, 0)),
                  pl.BlockSpec((bm, act.shape[1]), lambda i, j: (i, 0)),
                  pl.BlockSpec((w_nsa_o.shape[0], bn), lambda i, j: (0, j)),
                  pl.BlockSpec((w_conv_o.shape[0], bn), lambda i, j: (0, j)),
                  pl.BlockSpec((bm, bn), lambda i, j: (i, j)),
                  pl.BlockSpec((bm, bn), lambda i, j: (i, j + nj))],
        out_specs=pl.BlockSpec((bm, bn), lambda i, j: (i, j)),
        out_shape=jax.ShapeDtypeStruct((m, d), jnp.bfloat16),
        compiler_params=_params("parallel", "arbitrary"),
        name="merge_branches",
    )(o, act, w_nsa_o, w_conv_o, merge_gates, merge_gates)


def _compress_kernel(x_ref, w1_ref, pe_ref, w2_ref, o_ref):
    c = x_ref.shape[1]
    for j in range(2):
        w_a = w1_ref[j, :, :HEAD_DIM]
        w_b = w1_ref[j, :, HEAD_DIM:]
        pe = pe_ref[j]
        bias = (jnp.dot(pe[:, :CMP_HALF], w_a, preferred_element_type=jnp.float32)
                + jnp.dot(pe[:, CMP_HALF:], w_b, preferred_element_type=jnp.float32))[0:1]
        xs = []
        for g in range(N_KV):
            base = (j * N_KV + g) * HEAD_DIM
            xs.append(jnp.concatenate(
                [x_ref[0, :, i * KV_W + base:i * KV_W + base + HEAD_DIM] for i in range(CMP_STRIDE)], axis=1))
        x = jnp.concatenate(xs, axis=0).astype(jnp.bfloat16)
        ab = jnp.dot(x, w1_ref[j], preferred_element_type=jnp.float32)
        a, b = ab[:, :HEAD_DIM], ab[:, HEAD_DIM:]
        for g in range(N_KV):
            b_next = pltpu.roll(b[g * c:(g + 1) * c], c - 1, axis=0)
            h = jax.nn.gelu(a[g * c:(g + 1) * c] + b_next + bias)
            o_ref[0, j, g] = jnp.dot(h.astype(jnp.bfloat16), w2_ref[j],
                                     preferred_element_type=jnp.float32).astype(o_ref.dtype)


def _compress_weights(cmp_pe, cmp_w1, cmp_w2):
    bf = jnp.bfloat16
    w1 = jnp.concatenate([cmp_w1[:, :CMP_STRIDE].reshape(2, CMP_HALF, HEAD_DIM),
                          cmp_w1[:, CMP_STRIDE:].reshape(2, CMP_HALF, HEAD_DIM)], axis=-1).astype(bf)
    pe = jnp.broadcast_to(cmp_pe.reshape(2, 1, CMP_LEN * HEAD_DIM), (2, 8, CMP_LEN * HEAD_DIM)).astype(bf)
    return w1, pe, cmp_w2.astype(bf)


def compress_prompt(kvc, cmp_pe, cmp_w1, cmp_w2):
    n, t, _ = kvc.shape
    c = t // CMP_STRIDE
    w1, pe, w2 = _compress_weights(cmp_pe, cmp_w1, cmp_w2)
    return pl.pallas_call(
        _compress_kernel,
        grid=(n,),
        in_specs=[pl.BlockSpec((1, c, CMP_STRIDE * KV_W), lambda b: (b, 0, 0)),
                  pl.BlockSpec(w1.shape, lambda b: (0, 0, 0)),
                  pl.BlockSpec(pe.shape, lambda b: (0, 0, 0)),
                  pl.BlockSpec(w2.shape, lambda b: (0, 0, 0))],
        out_specs=pl.BlockSpec((1, 2, N_KV, c, HEAD_DIM), lambda b: (b, 0, 0, 0, 0)),
        out_shape=jax.ShapeDtypeStruct((n, 2, N_KV, c, HEAD_DIM), jnp.bfloat16),
        compiler_params=_params("parallel"),
        name="compress_prompt",
    )(kvc.reshape(n, c, CMP_STRIDE * KV_W), w1, pe, w2)


def _softmax_rows(s, mask, every_row_valid=False):
    sm = jnp.where(mask, s, NEG)
    e = jnp.exp2((sm - jnp.max(sm, axis=-1, keepdims=True)) * EXP2_SCALE)
    if not every_row_valid:
        e = jnp.where(mask, e, 0.0)
    return e / jnp.maximum(jnp.sum(e, axis=-1, keepdims=True), 1e-30)


def _select_blocks_t(score_t, qpos_l):
    blk = lax.broadcasted_iota(jnp.int32, score_t.shape, 0)
    cur = lax.shift_right_logical(qpos_l, SEL_BLOCK.bit_length() - 1)
    avail = blk <= cur
    forced = avail & ((blk == 0) | (blk == cur) | (blk == cur - 1))
    work = jnp.where(forced, jnp.inf, jnp.where(avail, score_t, -jnp.inf))
    blk_f = blk.astype(jnp.float32)
    sel = jnp.zeros(score_t.shape, jnp.float32)
    for _ in range(SEL_TOPK):
        m = jnp.max(work, axis=0, keepdims=True)
        first = jnp.min(jnp.where(work == m, blk_f, float(score_t.shape[0])), axis=0, keepdims=True)
        hit = blk_f == first
        sel = jnp.where(hit & (m > -jnp.inf), 1.0, sel)
        work = jnp.where(hit, -jnp.inf, work)
    return sel


def _nsa_prompt_kernel(q_ref, gate_ref, ckv_ref, ks_ref, kw_ref, o_ref, m_ref, l_ref, acc_ref, *, tq, tk):
    b = pl.program_id(1)
    q0 = b * tq
    qpos = q0 + lax.broadcasted_iota(jnp.int32, (tq, 1), 0)
    qpos_l = q0 + lax.broadcasted_iota(jnp.int32, (1, tq), 1)
    n_cmp = ckv_ref.shape[3]
    gates = gate_ref[0]
    blocks_per_tile = tk // SEL_BLOCK

    for g in range(N_KV):
        kcol = slice(g * HEAD_DIM, (g + 1) * HEAD_DIM)
        vcol = slice((N_KV + g) * HEAD_DIM, (N_KV + g + 1) * HEAD_DIM)
        qg = jnp.concatenate([q_ref[0, :, (g * HPG + hh) * HEAD_DIM:(g * HPG + hh + 1) * HEAD_DIM]
                              for hh in range(HPG)], axis=0)

        s = lax.dot_general(qg, ckv_ref[0, 0, g], _NT, preferred_element_type=jnp.float32)
        cend = lax.broadcasted_iota(jnp.int32, (1, n_cmp), 1) * CMP_STRIDE + (CMP_LEN - 1)
        vis = cend <= qpos
        ps = [_softmax_rows(s[hh * tq:(hh + 1) * tq], vis) for hh in range(HPG)]
        o_cmp = jnp.dot(jnp.concatenate(ps, axis=0).astype(jnp.bfloat16), ckv_ref[0, 1, g],
                        preferred_element_type=jnp.float32)
        p_grp = ps[0]
        for p in ps[1:]:
            p_grp = p_grp + p

        c0 = lax.broadcasted_iota(jnp.int32, (LANES, n_cmp), 1) * CMP_STRIDE
        s0 = lax.broadcasted_iota(jnp.int32, (LANES, n_cmp), 0) * SEL_BLOCK
        overlap_t = ((c0 < s0 + SEL_BLOCK) & (c0 + CMP_LEN > s0)).astype(jnp.bfloat16)
        score_t = lax.dot_general(overlap_t, p_grp.astype(jnp.bfloat16), _NT, preferred_element_type=jnp.float32)
        sel = _select_blocks_t(score_t, qpos_l).T.astype(jnp.bfloat16)

        m_ref[...] = jnp.full(m_ref.shape, NEG, jnp.float32)
        l_ref[...] = jnp.zeros(l_ref.shape, jnp.float32)
        acc_ref[...] = jnp.zeros(acc_ref.shape, jnp.float32)

        def sel_tile(kt, causal):
            k0 = pl.multiple_of(kt * tk, tk)
            s = lax.dot_general(qg, ks_ref[0, pl.ds(k0, tk), kcol], _NT, preferred_element_type=jnp.float32)
            jrow = lax.broadcasted_iota(jnp.int32, (LANES, tk), 0)
            jcol = lax.shift_right_logical(lax.broadcasted_iota(jnp.int32, (LANES, tk), 1),
                                           SEL_BLOCK.bit_length() - 1)
            expand = (jrow == kt * blocks_per_tile + jcol).astype(jnp.bfloat16)
            mask = jnp.dot(sel, expand, preferred_element_type=jnp.float32) > 0.5
            if causal:
                mask = mask & (k0 + lax.broadcasted_iota(jnp.int32, (1, tk), 1) <= qpos)
            ps = []
            for hh in range(HPG):
                rows = slice(hh * tq, (hh + 1) * tq)
                sh = jnp.where(mask, s[rows], NEG)
                m_old = m_ref[rows]
                m_new = jnp.maximum(m_old, jnp.max(sh, axis=-1, keepdims=True))
                p = jnp.exp2((sh - m_new) * EXP2_SCALE)
                alpha = jnp.exp2((m_old - m_new) * EXP2_SCALE)
                l_ref[rows] = alpha * l_ref[rows] + jnp.sum(p, axis=-1, keepdims=True)
                acc_ref[rows] = alpha * acc_ref[rows]
                m_ref[rows] = m_new
                ps.append(p.astype(jnp.bfloat16))
            acc_ref[...] += jnp.dot(jnp.concatenate(ps, axis=0), ks_ref[0, pl.ds(k0, tk), vcol],
                                    preferred_element_type=jnp.float32)

        n_full = q0 // tk

        def full_tile(kt, carry):
            sel_tile(kt, False)
            return carry

        lax.fori_loop(0, n_full, full_tile, 0)
        sel_tile(n_full, True)
        o_sel = acc_ref[...] / l_ref[...]

        w0 = pl.multiple_of(jnp.maximum(q0 - WINDOW, 0), tq)
        s = lax.dot_general(qg, kw_ref[0, pl.ds(w0, WINDOW + tq), kcol], _NT, preferred_element_type=jnp.float32)
        kpos = w0 + lax.broadcasted_iota(jnp.int32, (1, WINDOW + tq), 1)
        mask = (kpos <= qpos) & (kpos > qpos - WINDOW)
        ps = [_softmax_rows(s[hh * tq:(hh + 1) * tq], mask, every_row_valid=True).astype(jnp.bfloat16)
              for hh in range(HPG)]
        o_win = jnp.dot(jnp.concatenate(ps, axis=0), kw_ref[0, pl.ds(w0, WINDOW + tq), vcol],
                        preferred_element_type=jnp.float32)

        for hh in range(HPG):
            head = g * HPG + hh
            rows = slice(hh * tq, (hh + 1) * tq)
            o = (gates[:, head:head + 1] * o_cmp[rows]
                 + gates[:, N_HEADS + head:N_HEADS + head + 1] * o_sel[rows]
                 + gates[:, 2 * N_HEADS + head:2 * N_HEADS + head + 1] * o_win[rows])
            o_ref[0, :, head * HEAD_DIM:(head + 1) * HEAD_DIM] = o.astype(o_ref.dtype)


def nsa_prompt(q, gates, ckv, kvs, kvw, *, tq=NSA_Q_TILE, tk=NSA_KV_TILE):
    n, t, _ = q.shape
    assert t % tk == 0 and tk % tq == 0 and t >= WINDOW + tq and t // SEL_BLOCK <= LANES
    rows = HPG * tq
    return pl.pallas_call(
        functools.partial(_nsa_prompt_kernel, tq=tq, tk=tk),
        grid=(n, t // tq),
        in_specs=[pl.BlockSpec((1, tq, Q_W), lambda b, i: (b, i, 0)),
                  pl.BlockSpec((1, tq, LANES), lambda b, i: (b, i, 0)),
                  pl.BlockSpec((1,) + ckv.shape[1:], lambda b, i: (b, 0, 0, 0, 0)),
                  pl.BlockSpec((1, t, KV_W), lambda b, i: (b, 0, 0)),
                  pl.BlockSpec((1, t, KV_W), lambda b, i: (b, 0, 0))],
        out_specs=pl.BlockSpec((1, tq, Q_W), lambda b, i: (b, i, 0)),
        out_shape=jax.ShapeDtypeStruct((n, t, Q_W), jnp.bfloat16),
        scratch_shapes=[pltpu.VMEM((rows, 1), jnp.float32), pltpu.VMEM((rows, 1), jnp.float32),
                        pltpu.VMEM((rows, HEAD_DIM), jnp.float32)],
        compiler_params=_params("parallel", "arbitrary"),
        name="nsa_prompt",
    )(q, gates, ckv, kvs, kvw)


def _conv_kernel(a_ref, b_ref, w_ref, bias_ref, g_ref, beta_ref, o_ref, win_ref, y_ref, *, tt):
    win_ref[0:tt, :] = a_ref[0]
    win_ref[tt:tt + CONV_CTX, :] = b_ref[0]
    first = CONV_CTX - (CONV_WIDTH - 1)
    rc = min(CONV_ROWS, tt)
    for cc in range(CONV_CH // LANES):
        cols = slice(cc * LANES, (cc + 1) * LANES)
        for r0 in range(0, tt, rc):
            acc = jnp.broadcast_to(bias_ref[:, cols], (rc, LANES))
            for k in range(CONV_WIDTH):
                acc = acc + w_ref[k:k + 1, cols] * win_ref[r0 + first + k:r0 + first + k + rc, cols]
            y_ref[r0:r0 + rc, cols] = acc
    y = y_ref[...]
    yc = y - jnp.mean(y, axis=-1, keepdims=True)
    yn = yc * lax.rsqrt(jnp.mean(yc * yc, axis=-1, keepdims=True) + EPS) * g_ref[...] + beta_ref[...]
    o_ref[0] = (yn * jax.nn.sigmoid(yn)).astype(o_ref.dtype)


def conv_act(ctx, up, w_dw, b_dw, ln_g, ln_b, *, block_t=128):
    n, t, ch = up.shape
    tp = -(-t // CONV_CTX) * CONV_CTX
    tt = _row_block(tp, block_t)
    padded = jnp.concatenate([jnp.zeros((n, CONV_CTX - ctx.shape[1], ch), up.dtype), ctx, up,
                              jnp.zeros((n, tp - t, ch), up.dtype)], axis=1)
    w = jnp.zeros((CONV_CTX, ch), jnp.float32).at[:CONV_WIDTH].set(w_dw)
    step = tt // CONV_CTX
    row = lambda v: v.reshape(1, ch).astype(jnp.float32)
    out = pl.pallas_call(
        functools.partial(_conv_kernel, tt=tt),
        grid=(n, tp // tt),
        in_specs=[pl.BlockSpec((1, tt, ch), lambda b, i: (b, i, 0)),
                  pl.BlockSpec((1, CONV_CTX, ch), lambda b, i: (b, (i + 1) * step, 0)),
                  pl.BlockSpec((CONV_CTX, ch), lambda b, i: (0, 0)),
                  pl.BlockSpec((1, ch), lambda b, i: (0, 0)),
                  pl.BlockSpec((1, ch), lambda b, i: (0, 0)),
                  pl.BlockSpec((1, ch), lambda b, i: (0, 0))],
        out_specs=pl.BlockSpec((1, tt, ch), lambda b, i: (b, i, 0)),
        out_shape=jax.ShapeDtypeStruct((n, tp, ch), jnp.bfloat16),
        scratch_shapes=[pltpu.VMEM((tt + CONV_CTX, ch), jnp.float32), pltpu.VMEM((tt, ch), jnp.float32)],
        compiler_params=_params("parallel", "arbitrary"),
        name="conv_act",
    )(padded, padded, w, row(b_dw), row(ln_g), row(ln_b))
    return out[:, :t]


def _cross_attn_kernel(q_ref, kv_ref, o_ref):
    outs = []
    for h in range(MEM_HEADS):
        lo = h * MEM_HEAD_DIM
        q = q_ref[0, :, lo:lo + MEM_HEAD_DIM]
        k = kv_ref[0, :, lo:lo + MEM_HEAD_DIM].astype(jnp.bfloat16)
        v = kv_ref[0, :, MEM_W + lo:MEM_W + lo + MEM_HEAD_DIM].astype(jnp.bfloat16)
        s = lax.dot_general(q, k, _NT, preferred_element_type=jnp.float32) * MEM_SCALE
        e = jnp.exp(s - jnp.max(s, axis=-1, keepdims=True))
        p = e / jnp.sum(e, axis=-1, keepdims=True)
        outs.append(jnp.dot(p.astype(jnp.bfloat16), v, preferred_element_type=jnp.float32))
    o_ref[0] = jnp.concatenate(outs, axis=-1).astype(o_ref.dtype)


def cross_attn_core(q, kv, *, block_q=512):
    n, t, _ = q.shape
    bq = _row_block(t, block_q)
    return pl.pallas_call(
        _cross_attn_kernel,
        grid=(n, t // bq),
        in_specs=[pl.BlockSpec((1, bq, MEM_W), lambda b, i: (b, i, 0)),
                  pl.BlockSpec((1, MEM_LEN, 2 * MEM_W), lambda b, i: (b, 0, 0))],
        out_specs=pl.BlockSpec((1, bq, MEM_W), lambda b, i: (b, i, 0)),
        out_shape=jax.ShapeDtypeStruct((n, t, MEM_W), jnp.bfloat16),
        compiler_params=_params("parallel", "arbitrary"),
        name="cross_attn_core",
    )(q, kv)


def _router_kernel(x_ref, g_ref, w_ref, b_ref, h_ref, e_ref, p_ref):
    x = x_ref[...]
    h = x * lax.rsqrt(jnp.mean(x * x, axis=-1, keepdims=True) + EPS) * g_ref[...]
    h_ref[...] = h
    logits = jnp.dot(h.astype(jnp.bfloat16), w_ref[...], preferred_element_type=jnp.float32) + b_ref[...]
    lt = logits.T[:N_EXPERTS]
    ids = lax.broadcasted_iota(jnp.int32, lt.shape, 0).astype(jnp.float32)
    vals, idxs = [], []
    for _ in range(TOP_K):
        m = jnp.max(lt, axis=0, keepdims=True)
        idx = jnp.min(jnp.where(lt == m, ids, float(N_EXPERTS)), axis=0, keepdims=True)
        vals.append(m)
        idxs.append(idx)
        lt = jnp.where(ids == idx, -jnp.inf, lt)
    top_v = jnp.concatenate(vals, axis=0)
    e = jnp.exp(top_v - vals[0])
    e_ref[...] = jnp.concatenate(idxs, axis=0).astype(jnp.int32)
    p_ref[...] = e / jnp.sum(e, axis=0, keepdims=True)


def moe_router(x, gain, w_router, b_router, *, block_m=512):
    m, d = x.shape
    bm = _row_block(m, block_m)
    w = jnp.zeros((d, LANES), jnp.bfloat16).at[:, :N_EXPERTS].set(w_router.astype(jnp.bfloat16))
    b = jnp.zeros((1, LANES), jnp.float32).at[0, :N_EXPERTS].set(b_router.astype(jnp.float32))
    return pl.pallas_call(
        _router_kernel,
        grid=(m // bm,),
        in_specs=[pl.BlockSpec((bm, d), lambda i: (i, 0)),
                  pl.BlockSpec((1, d), lambda i: (0, 0)),
                  pl.BlockSpec((d, LANES), lambda i: (0, 0)),
                  pl.BlockSpec((1, LANES), lambda i: (0, 0))],
        out_specs=[pl.BlockSpec((bm, d), lambda i: (i, 0)),
                   pl.BlockSpec((TOP_K, bm), lambda i: (0, i)),
                   pl.BlockSpec((TOP_K, bm), lambda i: (0, i))],
        out_shape=[jax.ShapeDtypeStruct((m, d), jnp.float32),
                   jax.ShapeDtypeStruct((TOP_K, m), jnp.int32),
                   jax.ShapeDtypeStruct((TOP_K, m), jnp.float32)],
        compiler_params=_params("parallel"),
        name="moe_router",
    )(x, gain.reshape(1, d).astype(jnp.float32), w, b)


def _expert_kernel(blk_e_ref, blk_rows_ref, tok_cur_ref, tok_nxt_ref, h_hbm, wg_ref, wu_ref, bg_ref, bu_ref,
                   wd_ref, bd_ref, o_ref, x_stage, x_cur, sem, *, nf):
    i, j = pl.program_id(0), pl.program_id(1)
    rows = blk_rows_ref[i]
    share = MOE_BLOCK_ROWS // nf

    def row_copy(tok, r):
        return pltpu.make_async_copy(h_hbm.at[pl.ds(tok, 1)], x_stage.at[pl.ds(r, 1)], sem.at[0])

    def for_all_rows(fn):
        def body(r, carry):
            fn(r)
            return carry
        lax.fori_loop(0, MOE_BLOCK_ROWS, body, 0, unroll=8)

    @pl.when(j == 0)
    def _():
        o_ref[...] = jnp.broadcast_to(bd_ref[0], o_ref.shape)

        @pl.when(i == 0)
        def _():
            for_all_rows(lambda r: row_copy(tok_cur_ref[0, 0, r], r).start())

        @pl.when((i == 0) | (blk_rows_ref[jnp.maximum(i - 1, 0)] > 0))
        def _():
            for_all_rows(lambda r: row_copy(0, r).wait())

        @pl.when(rows > 0)
        def _():
            x_cur[...] = x_stage[...].astype(jnp.bfloat16)

    @pl.when(rows > 0)
    def _():
        for r in range(share):
            row_copy(tok_nxt_ref[0, 0, j * share + r], j * share + r).start()
        x = x_cur[...]
        a = jnp.dot(x, wg_ref[0].astype(jnp.bfloat16), preferred_element_type=jnp.float32) + bg_ref[0]
        u = jnp.dot(x, wu_ref[0].astype(jnp.bfloat16), preferred_element_type=jnp.float32) + bu_ref[0]
        a = jnp.minimum(a, SWIGLU_LIMIT)
        u = jnp.clip(u, -SWIGLU_LIMIT, SWIGLU_LIMIT)
        act = a * jax.nn.sigmoid(SWIGLU_ALPHA * a) * (u + 1.0)
        o_ref[...] += jnp.dot(act.astype(jnp.bfloat16), wd_ref[0].astype(jnp.bfloat16),
                              preferred_element_type=jnp.float32)


def moe_experts(h, slot_tok, blk_e, blk_rows, w_gu, b_gu, w_down, b_down):
    s_total = slot_tok.shape[0]
    d = h.shape[1]
    n_e, _, two_ff = w_gu.shape
    ff = two_ff // 2
    nb = s_total // MOE_BLOCK_ROWS
    nf = ff // MOE_FF_TILE
    tok3 = slot_tok.reshape(nb, 1, MOE_BLOCK_ROWS)

    def col(i, j, rows_ref):
        return jnp.where(rows_ref[i] > 0, j, nf - 1)

    grid_spec = pltpu.PrefetchScalarGridSpec(
        num_scalar_prefetch=2,
        grid=(nb, nf),
        in_specs=[
            pl.BlockSpec((1, 1, MOE_BLOCK_ROWS), lambda i, j, e, r: (i, 0, 0), memory_space=pltpu.SMEM),
            pl.BlockSpec((1, 1, MOE_BLOCK_ROWS), lambda i, j, e, r: (jnp.minimum(i + 1, nb - 1), 0, 0),
                         memory_space=pltpu.SMEM),
            pl.BlockSpec(memory_space=pl.ANY),
            pl.BlockSpec((1, d, MOE_FF_TILE), lambda i, j, e, r: (e[i], 0, col(i, j, r))),
            pl.BlockSpec((1, d, MOE_FF_TILE), lambda i, j, e, r: (e[i], 0, nf + col(i, j, r))),
            pl.BlockSpec((1, 1, MOE_FF_TILE), lambda i, j, e, r: (e[i], 0, col(i, j, r))),
            pl.BlockSpec((1, 1, MOE_FF_TILE), lambda i, j, e, r: (e[i], 0, nf + col(i, j, r))),
            pl.BlockSpec((1, MOE_FF_TILE, d), lambda i, j, e, r: (e[i], col(i, j, r), 0)),
            pl.BlockSpec((1, 1, d), lambda i, j, e, r: (e[i], 0, 0)),
        ],
        out_specs=pl.BlockSpec((MOE_BLOCK_ROWS, d), lambda i, j, e, r: (i, 0)),
        scratch_shapes=[pltpu.VMEM((MOE_BLOCK_ROWS, d), jnp.float32),
                        pltpu.VMEM((MOE_BLOCK_ROWS, d), jnp.bfloat16),
                        pltpu.SemaphoreType.DMA((1,))],
    )
    b_gu3 = b_gu.reshape(n_e, 1, two_ff)
    return pl.pallas_call(
        functools.partial(_expert_kernel, nf=nf),
        grid_spec=grid_spec,
        out_shape=jax.ShapeDtypeStruct((s_total, d), jnp.float32),
        compiler_params=_params("arbitrary", "arbitrary"),
        name="moe_experts",
    )(blk_e, blk_rows, tok3, tok3, h, w_gu, w_gu, b_gu3, b_gu3, w_down, b_down.reshape(n_e, 1, d))


def moe_layer(x_groups, gain, w_router, b_router, w_gu, b_gu, w_down, b_down, g_final):
    routed = [moe_router(x, gain, w_router, b_router) for x in x_groups]
    h = jnp.concatenate([r[0] for r in routed], axis=0)
    top_e = jnp.concatenate([r[1] for r in routed], axis=1)
    top_p = jnp.concatenate([r[2] for r in routed], axis=1)
    t = h.shape[0]
    bm = MOE_BLOCK_ROWS
    n_blocks = (t * TOP_K) // bm + N_EXPERTS + 1
    onehot = (top_e[:, :, None] == jnp.arange(N_EXPERTS)[None, None, :]).astype(jnp.int32)
    per_tok = jnp.sum(onehot, axis=0)
    before = jnp.cumsum(per_tok, axis=0) - per_tok
    counts = jnp.sum(per_tok, axis=0)
    blocks_e = (counts + bm - 1) // bm
    blk_end = jnp.cumsum(blocks_e)
    blk_start = blk_end - blocks_e
    first_slot = before + (blk_start * bm)[None, :]
    slot = jnp.sum(onehot * first_slot[None], axis=-1).astype(jnp.int32)
    tok_ids = jnp.broadcast_to(jnp.arange(t, dtype=jnp.int32)[None], slot.shape)
    slot_tok = jnp.zeros((n_blocks * bm,), jnp.int32).at[slot.reshape(-1)].set(tok_ids.reshape(-1))
    blk = jnp.arange(n_blocks)
    blk_e = jnp.minimum(jnp.searchsorted(blk_end, blk, side='right'), N_EXPERTS - 1).astype(jnp.int32)
    used = blk < blk_end[-1]
    last_e = blk_e[jnp.maximum(blk_end[-1] - 1, 0)]
    blk_e = jnp.where(used, blk_e, last_e).astype(jnp.int32)
    blk_rows = jnp.where(used, jnp.clip(counts[blk_e] - (blk - blk_start[blk_e]) * bm, 0, bm), 0).astype(jnp.int32)
    outs = moe_experts(h, slot_tok, blk_e, blk_rows, w_gu, b_gu, w_down, b_down)
    ys, start = [], 0
    for x in x_groups:
        m = x.shape[0]
        ys.append(moe_combine_norm(x, outs, slot[:, start:start + m], top_p[:, start:start + m], g_final))
        start += m
    return ys


def _combine_kernel(idx_cur_ref, idx_nxt_ref, x_ref, p_ref, g_ref, outs_hbm, o_ref, buf, sem, *, bm):
    i = pl.program_id(0)
    cur = lax.rem(i, 2)
    n_rows = TOP_K * bm

    def row_copy(slot, dst, r):
        return pltpu.make_async_copy(outs_hbm.at[pl.ds(slot, 1)], buf.at[dst, pl.ds(r, 1)], sem.at[dst])

    def for_rows(fn):
        def body(r, carry):
            fn(r)
            return carry
        lax.fori_loop(0, n_rows, body, 0, unroll=8)

    @pl.when(i == 0)
    def _():
        for_rows(lambda r: row_copy(idx_cur_ref[0, 0, r], cur, r).start())

    for_rows(lambda r: row_copy(0, cur, r).wait())

    @pl.when(i + 1 < pl.num_programs(0))
    def _():
        for_rows(lambda r: row_copy(idx_nxt_ref[0, 0, r], 1 - cur, r).start())

    x = x_ref[...]
    for k in range(TOP_K):
        x = x + p_ref[:, k:k + 1] * buf[cur, k * bm:(k + 1) * bm, :]
    o_ref[...] = x * lax.rsqrt(jnp.mean(x * x, axis=-1, keepdims=True) + EPS) * g_ref[...]


def moe_combine_norm(x, outs, slot, gate, gain, *, block_m=128):
    m, d = x.shape
    bm = _row_block(m, block_m)
    nt = m // bm
    idx = slot.reshape(TOP_K, nt, bm).transpose(1, 0, 2).reshape(nt, 1, TOP_K * bm)
    return pl.pallas_call(
        functools.partial(_combine_kernel, bm=bm),
        grid=(nt,),
        in_specs=[pl.BlockSpec((1, 1, TOP_K * bm), lambda i: (i, 0, 0), memory_space=pltpu.SMEM),
                  pl.BlockSpec((1, 1, TOP_K * bm), lambda i: (jnp.minimum(i + 1, nt - 1), 0, 0),
                               memory_space=pltpu.SMEM),
                  pl.BlockSpec((bm, d), lambda i: (i, 0)),
                  pl.BlockSpec((bm, TOP_K), lambda i: (i, 0)),
                  pl.BlockSpec((1, d), lambda i: (0, 0)),
                  pl.BlockSpec(memory_space=pl.ANY)],
        out_specs=pl.BlockSpec((bm, d), lambda i: (i, 0)),
        out_shape=jax.ShapeDtypeStruct((m, d), jnp.float32),
        scratch_shapes=[pltpu.VMEM((2, TOP_K * bm, d), jnp.float32), pltpu.SemaphoreType.DMA((2,))],
        compiler_params=_params("arbitrary"),
        name="moe_combine_norm",
    )(idx, idx, x, gate.T, gain.reshape(1, d).astype(jnp.float32), outs)


def _hist_compress_kernel(pt_ref, *refs):
    pages, (w1_ref, pe_ref, w2_ref, o_ref, slab_ref) = refs[:CMP_PAGES + 1], refs[CMP_PAGES + 1:]
    c = CMP_PAGES * CHUNKS_PER_PAGE
    last = lax.broadcasted_iota(jnp.int32, (c, 1), 0) == c - 1

    slab_rows = CMP_STRIDE * KV_ROWS
    regs_per_chunk = slab_rows // 8
    for k in range(CMP_PAGES + 1):
        chunks = CHUNKS_PER_PAGE if k < CMP_PAGES else 1
        if k == CMP_PAGES:
            for s in range(slab_rows):
                slab_ref[s * SLAB_PITCH + c:s * SLAB_PITCH + c + 8, :] = jnp.zeros((8, HEAD_DIM), jnp.float32)
        for cc in range(chunks):
            for sg in range(regs_per_chunk):
                v = cc * regs_per_chunk + sg
                slab_ref[pl.ds(8 * sg * SLAB_PITCH + k * CHUNKS_PER_PAGE + cc, 8, stride=SLAB_PITCH), :] = (
                    pages[k][v * 8:(v + 1) * 8, :])

    def rows(sub, lo, n):
        return jnp.concatenate([slab_ref[(i * KV_ROWS + sub) * SLAB_PITCH + lo:(i * KV_ROWS + sub) * SLAB_PITCH + lo + n, :]
                                for i in range(CMP_STRIDE)], axis=1)

    for j in range(2):
        w_a = w1_ref[j, :, :HEAD_DIM]
        w_b = w1_ref[j, :, HEAD_DIM:]
        pe = pe_ref[j]
        bias = (jnp.dot(pe[:, :CMP_HALF], w_a, preferred_element_type=jnp.float32)
                + jnp.dot(pe[:, CMP_HALF:], w_b, preferred_element_type=jnp.float32))[0:1]
        bases = [j * N_KV + g for g in range(N_KV)]
        x = jnp.concatenate([rows(base, 0, c) for base in bases], axis=0).astype(jnp.bfloat16)
        x_next = jnp.concatenate([rows(base, c, CHUNKS_PER_PAGE) for base in bases], axis=0).astype(jnp.bfloat16)
        ab = jnp.dot(x, w1_ref[j], preferred_element_type=jnp.float32)
        a, b = ab[:, :HEAD_DIM], ab[:, HEAD_DIM:]
        b_tail = jnp.dot(x_next, w_b, preferred_element_type=jnp.float32)
        for g in range(N_KV):
            b_next = jnp.where(last, b_tail[g * CHUNKS_PER_PAGE:g * CHUNKS_PER_PAGE + 1],
                               pltpu.roll(b[g * c:(g + 1) * c], c - 1, axis=0))
            h = jax.nn.gelu(a[g * c:(g + 1) * c] + b_next + bias)
            o_ref[0, j, g] = jnp.dot(h.astype(jnp.bfloat16), w2_ref[j],
                                     preferred_element_type=jnp.float32).astype(o_ref.dtype)


def compress_history(pool, page_table, cmp_pe, cmp_w1, cmp_w2):
    n, n_pages = page_table.shape
    assert n_pages % CMP_PAGES == 0
    c = n_pages * CHUNKS_PER_PAGE
    w1, pe, w2 = _compress_weights(cmp_pe, cmp_w1, cmp_w2)

    def page_spec(k):
        return pl.BlockSpec((PAGE_SIZE * KV_ROWS, HEAD_DIM),
                            lambda b, s, pt: (pt[b * n_pages + jnp.minimum(s * CMP_PAGES + k, n_pages - 1)], 0))

    grid_spec = pltpu.PrefetchScalarGridSpec(
        num_scalar_prefetch=1,
        grid=(n, n_pages // CMP_PAGES),
        in_specs=[page_spec(k) for k in range(CMP_PAGES + 1)] + [
            pl.BlockSpec(w1.shape, lambda b, s, pt: (0, 0, 0)),
            pl.BlockSpec(pe.shape, lambda b, s, pt: (0, 0, 0)),
            pl.BlockSpec(w2.shape, lambda b, s, pt: (0, 0, 0))],
        out_specs=pl.BlockSpec((1, 2, N_KV, CMP_PAGES * CHUNKS_PER_PAGE, HEAD_DIM), lambda b, s, pt: (b, 0, 0, s, 0)),
        scratch_shapes=[pltpu.VMEM((CMP_STRIDE * KV_ROWS * SLAB_PITCH, HEAD_DIM), jnp.float32)],
    )
    return pl.pallas_call(
        _hist_compress_kernel,
        grid_spec=grid_spec,
        out_shape=jax.ShapeDtypeStruct((n, 2, N_KV, c, HEAD_DIM), jnp.bfloat16),
        compiler_params=_params("parallel", "arbitrary"),
        name="compress_history",
    )(page_table.reshape(-1), *([pool] * (CMP_PAGES + 1)), w1, pe, w2)


def _nsa_sample_dense_kernel(q_ref, ckv_ref, kw_ref, ocmp_ref, owin_ref, idx_ref, val_ref, *, past, n_cmp, n_sel):
    rows = HPG * DEC_PAD
    tok = lax.bitwise_and(lax.broadcasted_iota(jnp.int32, (rows, 1), 0), DEC_PAD - 1)
    qpos = past + tok
    qpos_l = past + lax.broadcasted_iota(jnp.int32, (1, LANES), 1)
    c_all = ckv_ref.shape[3]
    n_sel_pad = -(-n_sel // 8) * 8
    wlen = kw_ref.shape[1]
    for g in range(N_KV):
        qg = q_ref[0, g]
        s = lax.dot_general(qg, ckv_ref[0, 0, g], _NT, preferred_element_type=jnp.float32)
        cidx = lax.broadcasted_iota(jnp.int32, (1, c_all), 1)
        vis = (cidx * CMP_STRIDE + (CMP_LEN - 1) <= qpos) & (cidx < n_cmp)
        p = _softmax_rows(s, vis)
        ocmp_ref[0, g] = jnp.dot(p.astype(jnp.bfloat16), ckv_ref[0, 1, g], preferred_element_type=jnp.float32)
        p_grp = p[0:DEC_PAD]
        for hh in range(1, HPG):
            p_grp = p_grp + p[hh * DEC_PAD:(hh + 1) * DEC_PAD]
        p_grp = jnp.concatenate([p_grp, jnp.zeros((LANES - DEC_PAD, c_all), jnp.float32)], axis=0)
        c0 = lax.broadcasted_iota(jnp.int32, (n_sel_pad, c_all), 1) * CMP_STRIDE
        s0 = lax.broadcasted_iota(jnp.int32, (n_sel_pad, c_all), 0) * SEL_BLOCK
        overlap_t = ((c0 < s0 + SEL_BLOCK) & (c0 + CMP_LEN > s0)).astype(jnp.bfloat16)
        score_t = lax.dot_general(overlap_t, p_grp.astype(jnp.bfloat16), _NT, preferred_element_type=jnp.float32)
        blk = lax.broadcasted_iota(jnp.int32, score_t.shape, 0)
        cur = lax.shift_right_logical(qpos_l, SEL_BLOCK.bit_length() - 1)
        avail = (blk <= cur) & (blk < n_sel)
        forced = avail & ((blk == 0) | (blk == cur) | (blk == cur - 1))
        work = jnp.where(forced, jnp.inf, jnp.where(avail, score_t, -jnp.inf))
        blk_f = blk.astype(jnp.float32)
        idxs, vals = [], []
        for _ in range(SEL_TOPK):
            m = jnp.max(work, axis=0, keepdims=True)
            first = jnp.min(jnp.where(work == m, blk_f, float(n_sel_pad)), axis=0, keepdims=True)
            idxs.append(first)
            vals.append(jnp.where(m > -jnp.inf, 1.0, 0.0))
            work = jnp.where(blk_f == first, -jnp.inf, work)
        idx_ref[0, g] = jnp.concatenate(idxs, axis=0).astype(jnp.int32)
        val_ref[0, g] = jnp.concatenate(vals, axis=0).astype(jnp.int32)

        kcol = slice(g * HEAD_DIM, (g + 1) * HEAD_DIM)
        vcol = slice((N_KV + g) * HEAD_DIM, (N_KV + g + 1) * HEAD_DIM)
        s = lax.dot_general(qg, kw_ref[0, :, kcol].astype(jnp.bfloat16), _NT, preferred_element_type=jnp.float32)
        kpos = past - WINDOW + lax.broadcasted_iota(jnp.int32, (1, wlen), 1)
        p = _softmax_rows(s, (kpos <= qpos) & (kpos > qpos - WINDOW))
        owin_ref[0, g] = jnp.dot(p.astype(jnp.bfloat16), kw_ref[0, :, vcol].astype(jnp.bfloat16),
                                 preferred_element_type=jnp.float32)


def nsa_sample_dense(qd, ckv, kw, *, past, n_cmp, n_sel):
    n = qd.shape[0]
    rows = HPG * DEC_PAD
    o_shape = jax.ShapeDtypeStruct((n, N_KV, rows, HEAD_DIM), jnp.float32)
    i_shape = jax.ShapeDtypeStruct((n, N_KV, SEL_TOPK, LANES), jnp.int32)
    o_spec = pl.BlockSpec((1, N_KV, rows, HEAD_DIM), lambda b: (b, 0, 0, 0))
    i_spec = pl.BlockSpec((1, N_KV, SEL_TOPK, LANES), lambda b: (b, 0, 0, 0))
    return pl.pallas_call(
        functools.partial(_nsa_sample_dense_kernel, past=past, n_cmp=n_cmp, n_sel=n_sel),
        grid=(n,),
        in_specs=[pl.BlockSpec((1, N_KV, rows, HEAD_DIM), lambda b: (b, 0, 0, 0)),
                  pl.BlockSpec((1,) + ckv.shape[1:], lambda b: (b, 0, 0, 0, 0)),
                  pl.BlockSpec((1,) + kw.shape[1:], lambda b: (b, 0, 0))],
        out_specs=[o_spec, o_spec, i_spec, i_spec],
        out_shape=[o_shape, o_shape, i_shape, i_shape],
        compiler_params=_params("parallel"),
        name="nsa_sample_dense",
    )(qd, ckv, kw)


def _nsa_sample_sel_kernel(idx_ref, val_ref, pt_ref, q_ref, new_ref, *refs, tq, past, n_past_blk):
    blocks, o_ref = refs[:N_KV * SEL_TOPK], refs[N_KV * SEL_TOPK]
    b, t = pl.program_id(0), pl.program_id(1)
    qpos = past + t
    lane = lax.broadcasted_iota(jnp.int32, (1, SEL_TOPK * SEL_BLOCK), 1)
    slot_of_lane = lax.shift_right_logical(lane, SEL_BLOCK.bit_length() - 1)
    for g in range(N_KV):
        base = ((b * tq + t) * N_KV + g) * SEL_TOPK
        ks, vs = [], []
        tokpos = jnp.zeros(lane.shape, jnp.int32)
        ok = jnp.zeros(lane.shape, jnp.int32)
        for s in range(SEL_TOPK):
            blk = idx_ref[base + s]
            from_pool = blk < n_past_blk
            pool_blk = blocks[g * SEL_TOPK + s]
            k_rows = pl.ds(g, SEL_BLOCK, stride=KV_ROWS)
            v_rows = pl.ds(N_KV + g, SEL_BLOCK, stride=KV_ROWS)
            ks.append(jnp.where(from_pool, pool_blk[k_rows, :], new_ref[0, k_rows, :]))
            vs.append(jnp.where(from_pool, pool_blk[v_rows, :], new_ref[0, v_rows, :]))
            here = slot_of_lane == s
            tokpos = jnp.where(here, blk * SEL_BLOCK, tokpos)
            ok = jnp.where(here, val_ref[base + s], ok)
        tokpos = tokpos + lax.bitwise_and(lane, SEL_BLOCK - 1)
        mask = (ok > 0) & (tokpos <= qpos)
        k = jnp.concatenate(ks, axis=0).astype(jnp.bfloat16)
        v = jnp.concatenate(vs, axis=0).astype(jnp.bfloat16)
        s = lax.dot_general(q_ref[0, 0, g], k, _NT, preferred_element_type=jnp.float32)
        p = _softmax_rows(s, mask)
        o_ref[0, 0, g] = jnp.dot(p.astype(jnp.bfloat16), v, preferred_element_type=jnp.float32)


def nsa_sample_sel(qs, idx, valid, page_table, pool, new_blocks, *, past):
    n, tq = qs.shape[:2]
    n_pages = page_table.shape[1]
    bpp = PAGE_SIZE // SEL_BLOCK
    n_past_blk = n_pages * bpp

    def block_spec(g, s):
        def index(b, t, idx_ref, val_ref, pt_ref):
            blk = jnp.minimum(idx_ref[((b * tq + t) * N_KV + g) * SEL_TOPK + s], n_past_blk - 1)
            return (pt_ref[b * n_pages + blk // bpp] * bpp + blk % bpp, 0)
        return pl.BlockSpec((SEL_BLOCK * KV_ROWS, HEAD_DIM), index)

    grid_spec = pltpu.PrefetchScalarGridSpec(
        num_scalar_prefetch=3,
        grid=(n, tq),
        in_specs=[pl.BlockSpec((1, 1, N_KV, 16, HEAD_DIM), lambda b, t, *_: (b, t, 0, 0, 0)),
                  pl.BlockSpec((1, SEL_BLOCK * KV_ROWS, HEAD_DIM), lambda b, t, *_: (b, 0, 0))]
                 + [block_spec(g, s) for g in range(N_KV) for s in range(SEL_TOPK)],
        out_specs=pl.BlockSpec((1, 1, N_KV, 16, HEAD_DIM), lambda b, t, *_: (b, t, 0, 0, 0)),
    )
    return pl.pallas_call(
        functools.partial(_nsa_sample_sel_kernel, tq=tq, past=past, n_past_blk=n_past_blk),
        grid_spec=grid_spec,
        out_shape=jax.ShapeDtypeStruct((n, tq, N_KV, 16, HEAD_DIM), jnp.float32),
        compiler_params=_params("parallel", "arbitrary"),
        name="nsa_sample_sel",
    )(idx.reshape(-1), valid.reshape(-1), page_table.reshape(-1), qs, new_blocks,
      *([pool] * (N_KV * SEL_TOPK)))


def nsa_sample(q, gates, kvs_new, kvw_new, pool_cmp, pool_sel, win_cache, page_table, cmp_pe, cmp_w1, cmp_w2):
    n, tq, _ = q.shape
    past = page_table.shape[1] * PAGE_SIZE
    assert tq <= DEC_PAD and tq <= SEL_BLOCK and win_cache.shape[1] == WINDOW
    assert (past + tq) // CMP_STRIDE * CMP_STRIDE == past
    n_cmp = past // CMP_STRIDE - 1
    n_sel = -(-(past + tq) // SEL_BLOCK)
    ckv = compress_history(pool_cmp, page_table, cmp_pe, cmp_w1, cmp_w2)
    q5 = q.reshape(n, tq, N_KV, HPG, HEAD_DIM)
    qd = jnp.pad(q5.transpose(0, 2, 3, 1, 4), ((0, 0), (0, 0), (0, 0), (0, DEC_PAD - tq), (0, 0)))
    qd = qd.reshape(n, N_KV, HPG * DEC_PAD, HEAD_DIM)
    kw_all = jnp.concatenate([win_cache, kvw_new], axis=1)
    kw = jnp.pad(kw_all, ((0, 0), (0, -(-(WINDOW + DEC_PAD) // LANES) * LANES - WINDOW - tq), (0, 0)))
    o_cmp, o_win, idx, valid = nsa_sample_dense(qd, ckv, kw, past=past, n_cmp=n_cmp, n_sel=n_sel)
    idx = idx[..., :tq].transpose(0, 3, 1, 2)
    valid = valid[..., :tq].transpose(0, 3, 1, 2)
    qs = jnp.pad(q5, ((0, 0), (0, 0), (0, 0), (0, 16 - HPG), (0, 0)))
    new_blocks = jnp.pad(kvs_new, ((0, 0), (0, SEL_BLOCK - tq), (0, 0))).reshape(n, SEL_BLOCK * KV_ROWS, HEAD_DIM)
    o_sel = nsa_sample_sel(qs, idx, valid, page_table, pool_sel, new_blocks, past=past)[:, :, :, :HPG]
    unpack = lambda o: o.reshape(n, N_KV, HPG, DEC_PAD, HEAD_DIM)[:, :, :, :tq].transpose(0, 3, 1, 2, 4)
    g = gates[..., :GATE_W].reshape(n, tq, 3, N_KV, HPG)[..., None]
    o = g[:, :, 0] * unpack(o_cmp) + g[:, :, 1] * o_sel + g[:, :, 2] * unpack(o_win)
    return o.reshape(n, tq, Q_W), kw_all


def prepare_weights(w_in, w_nsa_o, w_conv_o, w_out, w_xq, w_xkv, w_xo):
    bf = jnp.bfloat16
    w = w_in.astype(bf)
    d = w.shape[0]
    s = IN_SPLITS
    return dict(
        q=w[:, :s[0]], kv=w[:, s[0]:s[3]],
        gates=jnp.zeros((d, LANES), bf).at[:, :GATE_W].set(w[:, s[3]:s[4]]),
        glu_a=w[:, s[4]:s[4] + CONV_CH], glu_b=w[:, s[4] + CONV_CH:s[5]], merge=w[:, s[5]:],
        nsa_o=w_nsa_o.astype(bf), conv_o=w_conv_o.astype(bf), out=w_out.astype(bf),
        xq=w_xq.astype(bf), xkv=w_xkv.astype(bf), xo=w_xo.astype(bf))


def front(x2, g_mix, wts):
    q = fused_linear(x2, [wts['q']], gain=g_mix, out_dtype=jnp.bfloat16, name="front_q")
    kv = fused_linear(x2, [wts['kv']], gain=g_mix, name="front_kv")
    gates = fused_linear(x2, [wts['gates']], gain=g_mix, epilogue=_sigmoid_epilogue, name="front_gates")
    up = fused_linear(x2, [wts['glu_a'], wts['glu_b']], gain=g_mix, epilogue=_glu_epilogue, name="front_glu")
    mg = fused_linear(x2, [wts['merge']], gain=g_mix, epilogue=_sigmoid_epilogue, name="front_merge")
    return q, kv, gates, up, mg


def mixer_tail(x2, o, act, mg, wts):
    merged = merge_branches(o, act, wts['nsa_o'], wts['conv_o'], mg)
    return fused_linear(merged, [wts['out']], tiles=[x2], epilogue=_residual_epilogue, name="mixer_out_proj")


def cross_attn_block(x2, n, mkv, g_xattn, wts):
    m = x2.shape[0]
    qx = fused_linear(x2, [wts['xq']], gain=g_xattn, out_dtype=jnp.bfloat16, name="xattn_q")
    oc = cross_attn_core(qx.reshape(n, m // n, MEM_W), mkv)
    return fused_linear(oc.reshape(m, MEM_W), [wts['xo']], tiles=[x2], epilogue=_residual_epilogue, name="xattn_o")


def kernel(x_prompt, x_sample, cache_kv_cmp, cache_kv_sel, cache_kv_win, cache_conv, cache_mem_kv, page_table,
           mem_prompt, g_mix, w_in, cmp_pe, cmp_w1, cmp_w2, w_nsa_o, w_dw, b_dw, ln_conv_g, ln_conv_b, w_conv_o,
           w_out, g_xattn, g_mem, w_xq, w_xkv, w_xo, g_moe, w_router, b_router, w_gu, b_gu, w_down, b_down, g_final):
    nb, t, d = x_prompt.shape
    nd, tq = x_sample.shape[:2]
    l = 0
    bf = jnp.bfloat16
    wts = prepare_weights(w_in[l], w_nsa_o[l], w_conv_o[l], w_out[l], w_xq[l], w_xkv[l], w_xo[l])
    kv_shape = lambda n_, t_: (n_, t_, 2, N_KV, HEAD_DIM)

    xp2 = x_prompt.reshape(nb * t, d)
    q, kv, gates, up_new, mg = front(xp2, g_mix[l], wts)
    kvc, kvs, kvw = (kv[:, i * KV_W:(i + 1) * KV_W].reshape(nb, t, KV_W) for i in range(3))
    ckv = compress_prompt(kvc, cmp_pe[l], cmp_w1[l], cmp_w2[l])
    o = nsa_prompt(q.reshape(nb, t, Q_W), gates.reshape(nb, t, LANES), ckv, kvs.astype(bf), kvw.astype(bf))
    up_new = up_new.reshape(nb, t, CONV_CH)
    act = conv_act(jnp.zeros((nb, CONV_WIDTH - 1, CONV_CH), jnp.float32), up_new, w_dw[l], b_dw[l],
                   ln_conv_g[l], ln_conv_b[l])
    xp = mixer_tail(xp2, o.reshape(nb * t, Q_W), act.reshape(nb * t, CONV_CH), mg, wts)
    mkv = fused_linear(mem_prompt.reshape(nb * MEM_LEN, d), [wts['xkv']], gain=g_mem[l], name="mem_kv")
    xp = cross_attn_block(xp, nb, mkv.reshape(nb, MEM_LEN, 2 * MEM_W), g_xattn[l], wts)
    p_kv_cmp, p_kv_sel = kvc.reshape(kv_shape(nb, t))[None], kvs.reshape(kv_shape(nb, t))[None]
    p_kv_win = kvw.reshape(kv_shape(nb, t))[:, t - min(WINDOW, t):][None]
    p_conv = jnp.pad(up_new, ((0, 0), (CONV_WIDTH - 1, 0), (0, 0)))[:, t:][None]
    p_mem = mkv.reshape(nb, MEM_LEN, 2, MEM_HEADS, MEM_HEAD_DIM)[None]

    xs2 = x_sample.reshape(nd * tq, d)
    q, kv, gates, up_new, mg = front(xs2, g_mix[l], wts)
    kvc, kvs, kvw = (kv[:, i * KV_W:(i + 1) * KV_W].reshape(nd, tq, KV_W) for i in range(3))
    o, win_all = nsa_sample(q.reshape(nd, tq, Q_W), gates.reshape(nd, tq, LANES), kvs, kvw,
                            cache_kv_cmp[l].reshape(-1, HEAD_DIM), cache_kv_sel[l].reshape(-1, HEAD_DIM),
                            cache_kv_win[l].reshape(nd, -1, KV_W), page_table, cmp_pe[l], cmp_w1[l], cmp_w2[l])
    win_new = win_all[:, -min(WINDOW, win_all.shape[1]):].reshape(kv_shape(nd, min(WINDOW, win_all.shape[1])))
    kvc, kvs = kvc.reshape(kv_shape(nd, tq)), kvs.reshape(kv_shape(nd, tq))
    up_new = up_new.reshape(nd, tq, CONV_CH)
    act = conv_act(cache_conv[l], up_new, w_dw[l], b_dw[l], ln_conv_g[l], ln_conv_b[l])
    xs = mixer_tail(xs2, o.reshape(nd * tq, Q_W).astype(bf), act.reshape(nd * tq, CONV_CH), mg, wts)
    xs = cross_attn_block(xs, nd, cache_mem_kv[l].reshape(nd, MEM_LEN, 2 * MEM_W), g_xattn[l], wts)
    s_conv = jnp.concatenate([cache_conv[l], up_new], axis=1)[:, tq:][None]

    y_prompt, y_sample = moe_layer([xp, xs], g_moe[l], w_router[l], b_router[l], w_gu[l], b_gu[l], w_down[l],
                                   b_down[l], g_final)
    y_prompt, y_sample = y_prompt.reshape(nb, t, d), y_sample.reshape(nd, tq, d)
    return (y_prompt, y_sample, p_kv_cmp, p_kv_sel, p_kv_win, p_conv, p_mem, kvc[None], kvs[None], win_new[None],
            s_conv)
```

```python
import functools

import jax
import jax.numpy as jnp
from jax import lax
from jax.experimental import pallas as pl
from jax.experimental.pallas import tpu as pltpu

D_MODEL = 2048
PAGE_SIZE = 128

N_HEADS = 16
HEAD_DIM = 128
N_KV = 2
HPG = N_HEADS // N_KV
CMP_STRIDE = 16
CMP_LEN = 2 * CMP_STRIDE
CMP_HALF = CMP_STRIDE * HEAD_DIM
SEL_BLOCK = 64
SEL_TOPK = 16
WINDOW = 512
SCALE = HEAD_DIM ** -0.5
EXP2_SCALE = SCALE * 1.4426950408889634
CONV_CH = D_MODEL // 2
CONV_WIDTH = 31
MEM_LEN = 256
MEM_HEADS = 4
MEM_HEAD_DIM = 128
MEM_W = MEM_HEADS * MEM_HEAD_DIM
MEM_SCALE = MEM_HEAD_DIM ** -0.5
N_EXPERTS = 32
TOP_K = 4
SWIGLU_LIMIT = 7.0
SWIGLU_ALPHA = 1.702
EPS = 1e-6
Q_W = N_HEADS * HEAD_DIM
KV_W = 2 * N_KV * HEAD_DIM
KV_ROWS = KV_W // HEAD_DIM
GATE_W = 3 * N_HEADS
GLU_W = 2 * CONV_CH
MERGE_W = 2 * D_MODEL
IN_SPLITS = [Q_W, Q_W + KV_W, Q_W + 2 * KV_W, Q_W + 3 * KV_W, Q_W + 3 * KV_W + GATE_W, Q_W + 3 * KV_W + GATE_W + GLU_W]

V7X_VMEM_BYTES = 64 * 1024 * 1024
VMEM_LIMIT_BYTES = V7X_VMEM_BYTES - 8 * 1024 * 1024
LANES = 128

MOE_BLOCK_ROWS = 1152
MOE_FF_TILE = 256
NSA_Q_TILE = 128
NSA_KV_TILE = 1024
CONV_CTX = 32
CONV_ROWS = 64
CHUNKS_PER_PAGE = PAGE_SIZE // CMP_STRIDE
CMP_PAGES = 32
SLAB_PITCH = CMP_PAGES * CHUNKS_PER_PAGE + 8
DEC_PAD = 8

NEG = -1e30
_NT = (((1,), (1,)), ((), ()))


def _params(*semantics):
    return pltpu.CompilerParams(dimension_semantics=semantics, vmem_limit_bytes=VMEM_LIMIT_BYTES)


def _row_block(rows, target):
    if rows <= target:
        return rows
    for b in range(target, 15, -1):
        if rows % b == 0 and b % 16 == 0:
            return b
    return rows


def _linear_kernel(*refs, n_w, norm, n_tile, n_row, epilogue):
    x_ref = refs[0]
    k = 1
    g_ref = refs[k] if norm else None
    k += int(norm)
    w_refs = refs[k:k + n_w]
    k += n_w
    tile_refs = refs[k:k + n_tile]
    k += n_tile
    row_refs = refs[k:k + n_row]
    k += n_row
    o_ref, h_ref = refs[k], refs[k + 1]

    @pl.when(pl.program_id(1) == 0)
    def _():
        x = x_ref[...].astype(jnp.float32)
        if norm:
            x = x * lax.rsqrt(jnp.mean(x * x, axis=-1, keepdims=True) + EPS) * g_ref[...]
        h_ref[...] = x.astype(jnp.bfloat16)

    h = h_ref[...]
    zs = [jnp.dot(h, w[...], preferred_element_type=jnp.float32) for w in w_refs]
    o_ref[...] = epilogue(zs, [t[...] for t in tile_refs], [r[...] for r in row_refs]).astype(o_ref.dtype)


def fused_linear(x, ws, *, gain=None, tiles=(), rows=(), epilogue=None, out_dtype=jnp.float32,
                 block_m=1024, block_n=512, name="fused_linear"):
    m, k = x.shape
    n = ws[0].shape[1]
    bm = _row_block(m, block_m)
    bn = min(block_n, n)
    assert m % bm == 0 and n % bn == 0 and all(w.shape == (k, n) for w in ws)
    if epilogue is None:
        epilogue = _first_epilogue
    norm = gain is not None
    in_specs = [pl.BlockSpec((bm, k), lambda i, j: (i, 0))]
    args = [x]
    if norm:
        in_specs.append(pl.BlockSpec((1, k), lambda i, j: (0, 0)))
        args.append(gain.reshape(1, k).astype(jnp.float32))
    in_specs += [pl.BlockSpec((k, bn), lambda i, j: (0, j)) for _ in ws]
    args += list(ws)
    in_specs += [pl.BlockSpec((bm, bn), lambda i, j: (i, j)) for _ in tiles]
    args += list(tiles)
    in_specs += [pl.BlockSpec((1, bn), lambda i, j: (0, j)) for _ in rows]
    args += list(rows)
    return pl.pallas_call(
        functools.partial(_linear_kernel, n_w=len(ws), norm=norm, n_tile=len(tiles), n_row=len(rows),
                          epilogue=epilogue),
        grid=(m // bm, n // bn),
        in_specs=in_specs,
        out_specs=pl.BlockSpec((bm, bn), lambda i, j: (i, j)),
        out_shape=jax.ShapeDtypeStruct((m, n), out_dtype),
        scratch_shapes=[pltpu.VMEM((bm, k), jnp.bfloat16)],
        compiler_params=_params("parallel", "arbitrary"),
        name=name,
    )(*args)


def _first_epilogue(zs, ts, rs):
    return zs[0]


def _glu_epilogue(zs, ts, rs):
    return zs[0] * jax.nn.sigmoid(zs[1])


def _sigmoid_epilogue(zs, ts, rs):
    return jax.nn.sigmoid(zs[0])


def _residual_epilogue(zs, ts, rs):
    return ts[0] + zs[0]


def _merge_kernel(o_ref, a_ref, wo_ref, wc_ref, ga_ref, gc_ref, out_ref):
    y_att = jnp.dot(o_ref[...], wo_ref[...], preferred_element_type=jnp.float32)
    y_conv = jnp.dot(a_ref[...], wc_ref[...], preferred_element_type=jnp.float32)
    out_ref[...] = (ga_ref[...] * y_att + gc_ref[...] * y_conv).astype(out_ref.dtype)


def merge_branches(o, act, w_nsa_o, w_conv_o, merge_gates, *, block_m=1024, block_n=512):
    m = o.shape[0]
    d = w_nsa_o.shape[1]
    bm = _row_block(m, block_m)
    bn = min(block_n, d)
    nj = d // bn
    return pl.pallas_call(
        _merge_kernel,
        grid=(m // bm, nj),
        in_specs=[pl.BlockSpec((bm, o.shape[1]), lambda i, j: (i, 0)),
                  pl.BlockSpec((bm, act.shape[1]), lambda i, j: (i, 0)),
                  pl.BlockSpec((w_nsa_o.shape[0], bn), lambda i, j: (0, j)),
                  pl.BlockSpec((w_conv_o.shape[0], bn), lambda i, j: (0, j)),
                  pl.BlockSpec((bm, bn), lambda i, j: (i, j)),
                  pl.BlockSpec((bm, bn), lambda i, j: (i, j + nj))],
        out_specs=pl.BlockSpec((bm, bn), lambda i, j: (i, j)),
        out_shape=jax.ShapeDtypeStruct((m, d), jnp.bfloat16),
        compiler_params=_params("parallel", "arbitrary"),
        name="merge_branches",
    )(o, act, w_nsa_o, w_conv_o, merge_gates, merge_gates)


def _compress_kernel(x_ref, w1_ref, pe_ref, w2_ref, o_ref):
    c = x_ref.shape[1]
    for j in range(2):
        w_a = w1_ref[j, :, :HEAD_DIM]
        w_b = w1_ref[j, :, HEAD_DIM:]
        pe = pe_ref[j]
        bias = (jnp.dot(pe[:, :CMP_HALF], w_a, preferred_element_type=jnp.float32)
                + jnp.dot(pe[:, CMP_HALF:], w_b, preferred_element_type=jnp.float32))[0:1]
        xs = []
        for g in range(N_KV):
            base = (j * N_KV + g) * HEAD_DIM
            xs.append(jnp.concatenate(
                [x_ref[0, :, i * KV_W + base:i * KV_W + base + HEAD_DIM] for i in range(CMP_STRIDE)], axis=1))
        x = jnp.concatenate(xs, axis=0).astype(jnp.bfloat16)
        ab = jnp.dot(x, w1_ref[j], preferred_element_type=jnp.float32)
        a, b = ab[:, :HEAD_DIM], ab[:, HEAD_DIM:]
        for g in range(N_KV):
            b_next = pltpu.roll(b[g * c:(g + 1) * c], c - 1, axis=0)
            h = jax.nn.gelu(a[g * c:(g + 1) * c] + b_next + bias)
            o_ref[0, j, g] = jnp.dot(h.astype(jnp.bfloat16), w2_ref[j],
                                     preferred_element_type=jnp.float32).astype(o_ref.dtype)


def _compress_weights(cmp_pe, cmp_w1, cmp_w2):
    bf = jnp.bfloat16
    w1 = jnp.concatenate([cmp_w1[:, :CMP_STRIDE].reshape(2, CMP_HALF, HEAD_DIM),
                          cmp_w1[:, CMP_STRIDE:].reshape(2, CMP_HALF, HEAD_DIM)], axis=-1).astype(bf)
    pe = jnp.broadcast_to(cmp_pe.reshape(2, 1, CMP_LEN * HEAD_DIM), (2, 8, CMP_LEN * HEAD_DIM)).astype(bf)
    return w1, pe, cmp_w2.astype(bf)


def compress_prompt(kvc, cmp_pe, cmp_w1, cmp_w2):
    n, t, _ = kvc.shape
    c = t // CMP_STRIDE
    w1, pe, w2 = _compress_weights(cmp_pe, cmp_w1, cmp_w2)
    return pl.pallas_call(
        _compress_kernel,
        grid=(n,),
        in_specs=[pl.BlockSpec((1, c, CMP_STRIDE * KV_W), lambda b: (b, 0, 0)),
                  pl.BlockSpec(w1.shape, lambda b: (0, 0, 0)),
                  pl.BlockSpec(pe.shape, lambda b: (0, 0, 0)),
                  pl.BlockSpec(w2.shape, lambda b: (0, 0, 0))],
        out_specs=pl.BlockSpec((1, 2, N_KV, c, HEAD_DIM), lambda b: (b, 0, 0, 0, 0)),
        out_shape=jax.ShapeDtypeStruct((n, 2, N_KV, c, HEAD_DIM), jnp.bfloat16),
        compiler_params=_params("parallel"),
        name="compress_prompt",
    )(kvc.reshape(n, c, CMP_STRIDE * KV_W), w1, pe, w2)


def _softmax_rows(s, mask, every_row_valid=False):
    sm = jnp.where(mask, s, NEG)
    e = jnp.exp2((sm - jnp.max(sm, axis=-1, keepdims=True)) * EXP2_SCALE)
    if not every_row_valid:
        e = jnp.where(mask, e, 0.0)
    return e / jnp.maximum(jnp.sum(e, axis=-1, keepdims=True), 1e-30)


def _select_blocks_t(score_t, qpos_l):
    blk = lax.broadcasted_iota(jnp.int32, score_t.shape, 0)
    cur = lax.shift_right_logical(qpos_l, SEL_BLOCK.bit_length() - 1)
    avail = blk <= cur
    forced = avail & ((blk == 0) | (blk == cur) | (blk == cur - 1))
    work = jnp.where(forced, jnp.inf, jnp.where(avail, score_t, -jnp.inf))
    blk_f = blk.astype(jnp.float32)
    sel = jnp.zeros(score_t.shape, jnp.float32)
    for _ in range(SEL_TOPK):
        m = jnp.max(work, axis=0, keepdims=True)
        first = jnp.min(jnp.where(work == m, blk_f, float(score_t.shape[0])), axis=0, keepdims=True)
        hit = blk_f == first
        sel = jnp.where(hit & (m > -jnp.inf), 1.0, sel)
        work = jnp.where(hit, -jnp.inf, work)
    return sel


def _nsa_prompt_kernel(q_ref, gate_ref, ckv_ref, ks_ref, kw_ref, o_ref, m_ref, l_ref, acc_ref, *, tq, tk):
    b = pl.program_id(1)
    q0 = b * tq
    qpos = q0 + lax.broadcasted_iota(jnp.int32, (tq, 1), 0)
    qpos_l = q0 + lax.broadcasted_iota(jnp.int32, (1, tq), 1)
    n_cmp = ckv_ref.shape[3]
    gates = gate_ref[0]
    blocks_per_tile = tk // SEL_BLOCK

    for g in range(N_KV):
        kcol = slice(g * HEAD_DIM, (g + 1) * HEAD_DIM)
        vcol = slice((N_KV + g) * HEAD_DIM, (N_KV + g + 1) * HEAD_DIM)
        qg = jnp.concatenate([q_ref[0, :, (g * HPG + hh) * HEAD_DIM:(g * HPG + hh + 1) * HEAD_DIM]
                              for hh in range(HPG)], axis=0)

        s = lax.dot_general(qg, ckv_ref[0, 0, g], _NT, preferred_element_type=jnp.float32)
        cend = lax.broadcasted_iota(jnp.int32, (1, n_cmp), 1) * CMP_STRIDE + (CMP_LEN - 1)
        vis = cend <= qpos
        ps = [_softmax_rows(s[hh * tq:(hh + 1) * tq], vis) for hh in range(HPG)]
        o_cmp = jnp.dot(jnp.concatenate(ps, axis=0).astype(jnp.bfloat16), ckv_ref[0, 1, g],
                        preferred_element_type=jnp.float32)
        p_grp = ps[0]
        for p in ps[1:]:
            p_grp = p_grp + p

        c0 = lax.broadcasted_iota(jnp.int32, (LANES, n_cmp), 1) * CMP_STRIDE
        s0 = lax.broadcasted_iota(jnp.int32, (LANES, n_cmp), 0) * SEL_BLOCK
        overlap_t = ((c0 < s0 + SEL_BLOCK) & (c0 + CMP_LEN > s0)).astype(jnp.bfloat16)
        score_t = lax.dot_general(overlap_t, p_grp.astype(jnp.bfloat16), _NT, preferred_element_type=jnp.float32)
        sel = _select_blocks_t(score_t, qpos_l).T.astype(jnp.bfloat16)

        m_ref[...] = jnp.full(m_ref.shape, NEG, jnp.float32)
        l_ref[...] = jnp.zeros(l_ref.shape, jnp.float32)
        acc_ref[...] = jnp.zeros(acc_ref.shape, jnp.float32)

        def sel_tile(kt, causal):
            k0 = pl.multiple_of(kt * tk, tk)
            s = lax.dot_general(qg, ks_ref[0, pl.ds(k0, tk), kcol], _NT, preferred_element_type=jnp.float32)
            jrow = lax.broadcasted_iota(jnp.int32, (LANES, tk), 0)
            jcol = lax.shift_right_logical(lax.broadcasted_iota(jnp.int32, (LANES, tk), 1),
                                           SEL_BLOCK.bit_length() - 1)
            expand = (jrow == kt * blocks_per_tile + jcol).astype(jnp.bfloat16)
            mask = jnp.dot(sel, expand, preferred_element_type=jnp.float32) > 0.5
            if causal:
                mask = mask & (k0 + lax.broadcasted_iota(jnp.int32, (1, tk), 1) <= qpos)
            ps = []
            for hh in range(HPG):
                rows = slice(hh * tq, (hh + 1) * tq)
                sh = jnp.where(mask, s[rows], NEG)
                m_old = m_ref[rows]
                m_new = jnp.maximum(m_old, jnp.max(sh, axis=-1, keepdims=True))
                p = jnp.exp2((sh - m_new) * EXP2_SCALE)
                alpha = jnp.exp2((m_old - m_new) * EXP2_SCALE)
                l_ref[rows] = alpha * l_ref[rows] + jnp.sum(p, axis=-1, keepdims=True)
                acc_ref[rows] = alpha * acc_ref[rows]
                m_ref[rows] = m_new
                ps.append(p.astype(jnp.bfloat16))
            acc_ref[...] += jnp.dot(jnp.concatenate(ps, axis=0), ks_ref[0, pl.ds(k0, tk), vcol],
                                    preferred_element_type=jnp.float32)

        n_full = q0 // tk

        def full_tile(kt, carry):
            sel_tile(kt, False)
            return carry

        lax.fori_loop(0, n_full, full_tile, 0)
        sel_tile(n_full, True)
        o_sel = acc_ref[...] / l_ref[...]

        w0 = pl.multiple_of(jnp.maximum(q0 - WINDOW, 0), tq)
        s = lax.dot_general(qg, kw_ref[0, pl.ds(w0, WINDOW + tq), kcol], _NT, preferred_element_type=jnp.float32)
        kpos = w0 + lax.broadcasted_iota(jnp.int32, (1, WINDOW + tq), 1)
        mask = (kpos <= qpos) & (kpos > qpos - WINDOW)
        ps = [_softmax_rows(s[hh * tq:(hh + 1) * tq], mask, every_row_valid=True).astype(jnp.bfloat16)
              for hh in range(HPG)]
        o_win = jnp.dot(jnp.concatenate(ps, axis=0), kw_ref[0, pl.ds(w0, WINDOW + tq), vcol],
                        preferred_element_type=jnp.float32)

        for hh in range(HPG):
            head = g * HPG + hh
            rows = slice(hh * tq, (hh + 1) * tq)
            o = (gates[:, head:head + 1] * o_cmp[rows]
                 + gates[:, N_HEADS + head:N_HEADS + head + 1] * o_sel[rows]
                 + gates[:, 2 * N_HEADS + head:2 * N_HEADS + head + 1] * o_win[rows])
            o_ref[0, :, head * HEAD_DIM:(head + 1) * HEAD_DIM] = o.astype(o_ref.dtype)


def nsa_prompt(q, gates, ckv, kvs, kvw, *, tq=NSA_Q_TILE, tk=NSA_KV_TILE):
    n, t, _ = q.shape
    assert t % tk == 0 and tk % tq == 0 and t >= WINDOW + tq and t // SEL_BLOCK <= LANES
    rows = HPG * tq
    return pl.pallas_call(
        functools.partial(_nsa_prompt_kernel, tq=tq, tk=tk),
        grid=(n, t // tq),
        in_specs=[pl.BlockSpec((1, tq, Q_W), lambda b, i: (b, i, 0)),
                  pl.BlockSpec((1, tq, LANES), lambda b, i: (b, i, 0)),
                  pl.BlockSpec((1,) + ckv.shape[1:], lambda b, i: (b, 0, 0, 0, 0)),
                  pl.BlockSpec((1, t, KV_W), lambda b, i: (b, 0, 0)),
                  pl.BlockSpec((1, t, KV_W), lambda b, i: (b, 0, 0))],
        out_specs=pl.BlockSpec((1, tq, Q_W), lambda b, i: (b, i, 0)),
        out_shape=jax.ShapeDtypeStruct((n, t, Q_W), jnp.bfloat16),
        scratch_shapes=[pltpu.VMEM((rows, 1), jnp.float32), pltpu.VMEM((rows, 1), jnp.float32),
                        pltpu.VMEM((rows, HEAD_DIM), jnp.float32)],
        compiler_params=_params("parallel", "arbitrary"),
        name="nsa_prompt",
    )(q, gates, ckv, kvs, kvw)


def _conv_kernel(a_ref, b_ref, w_ref, bias_ref, g_ref, beta_ref, o_ref, win_ref, y_ref, *, tt):
    win_ref[0:tt, :] = a_ref[0]
    win_ref[tt:tt + CONV_CTX, :] = b_ref[0]
    first = CONV_CTX - (CONV_WIDTH - 1)
    rc = min(CONV_ROWS, tt)
    for cc in range(CONV_CH // LANES):
        cols = slice(cc * LANES, (cc + 1) * LANES)
        for r0 in range(0, tt, rc):
            acc = jnp.broadcast_to(bias_ref[:, cols], (rc, LANES))
            for k in range(CONV_WIDTH):
                acc = acc + w_ref[k:k + 1, cols] * win_ref[r0 + first + k:r0 + first + k + rc, cols]
            y_ref[r0:r0 + rc, cols] = acc
    y = y_ref[...]
    yc = y - jnp.mean(y, axis=-1, keepdims=True)
    yn = yc * lax.rsqrt(jnp.mean(yc * yc, axis=-1, keepdims=True) + EPS) * g_ref[...] + beta_ref[...]
    o_ref[0] = (yn * jax.nn.sigmoid(yn)).astype(o_ref.dtype)


def conv_act(ctx, up, w_dw, b_dw, ln_g, ln_b, *, block_t=128):
    n, t, ch = up.shape
    tp = -(-t // CONV_CTX) * CONV_CTX
    tt = _row_block(tp, block_t)
    padded = jnp.concatenate([jnp.zeros((n, CONV_CTX - ctx.shape[1], ch), up.dtype), ctx, up,
                              jnp.zeros((n, tp - t, ch), up.dtype)], axis=1)
    w = jnp.zeros((CONV_CTX, ch), jnp.float32).at[:CONV_WIDTH].set(w_dw)
    step = tt // CONV_CTX
    row = lambda v: v.reshape(1, ch).astype(jnp.float32)
    out = pl.pallas_call(
        functools.partial(_conv_kernel, tt=tt),
        grid=(n, tp // tt),
        in_specs=[pl.BlockSpec((1, tt, ch), lambda b, i: (b, i, 0)),
                  pl.BlockSpec((1, CONV_CTX, ch), lambda b, i: (b, (i + 1) * step, 0)),
                  pl.BlockSpec((CONV_CTX, ch), lambda b, i: (0, 0)),
                  pl.BlockSpec((1, ch), lambda b, i: (0, 0)),
                  pl.BlockSpec((1, ch), lambda b, i: (0, 0)),
                  pl.BlockSpec((1, ch), lambda b, i: (0, 0))],
        out_specs=pl.BlockSpec((1, tt, ch), lambda b, i: (b, i, 0)),
        out_shape=jax.ShapeDtypeStruct((n, tp, ch), jnp.bfloat16),
        scratch_shapes=[pltpu.VMEM((tt + CONV_CTX, ch), jnp.float32), pltpu.VMEM((tt, ch), jnp.float32)],
        compiler_params=_params("parallel", "arbitrary"),
        name="conv_act",
    )(padded, padded, w, row(b_dw), row(ln_g), row(ln_b))
    return out[:, :t]


def _cross_attn_kernel(q_ref, kv_ref, o_ref):
    outs = []
    for h in range(MEM_HEADS):
        lo = h * MEM_HEAD_DIM
        q = q_ref[0, :, lo:lo + MEM_HEAD_DIM]
        k = kv_ref[0, :, lo:lo + MEM_HEAD_DIM].astype(jnp.bfloat16)
        v = kv_ref[0, :, MEM_W + lo:MEM_W + lo + MEM_HEAD_DIM].astype(jnp.bfloat16)
        s = lax.dot_general(q, k, _NT, preferred_element_type=jnp.float32) * MEM_SCALE
        e = jnp.exp(s - jnp.max(s, axis=-1, keepdims=True))
        p = e / jnp.sum(e, axis=-1, keepdims=True)
        outs.append(jnp.dot(p.astype(jnp.bfloat16), v, preferred_element_type=jnp.float32))
    o_ref[0] = jnp.concatenate(outs, axis=-1).astype(o_ref.dtype)


def cross_attn_core(q, kv, *, block_q=512):
    n, t, _ = q.shape
    bq = _row_block(t, block_q)
    return pl.pallas_call(
        _cross_attn_kernel,
        grid=(n, t // bq),
        in_specs=[pl.BlockSpec((1, bq, MEM_W), lambda b, i: (b, i, 0)),
                  pl.BlockSpec((1, MEM_LEN, 2 * MEM_W), lambda b, i: (b, 0, 0))],
        out_specs=pl.BlockSpec((1, bq, MEM_W), lambda b, i: (b, i, 0)),
        out_shape=jax.ShapeDtypeStruct((n, t, MEM_W), jnp.bfloat16),
        compiler_params=_params("parallel", "arbitrary"),
        name="cross_attn_core",
    )(q, kv)


def _router_kernel(x_ref, g_ref, w_ref, b_ref, h_ref, e_ref, p_ref):
    x = x_ref[...]
    h = x * lax.rsqrt(jnp.mean(x * x, axis=-1, keepdims=True) + EPS) * g_ref[...]
    for a in range(h.shape[1] // LANES):
        h_ref[pl.ds(a, h.shape[0], stride=h.shape[1] // LANES), :] = h[:, a * LANES:(a + 1) * LANES]
    logits = jnp.dot(h.astype(jnp.bfloat16), w_ref[...], preferred_element_type=jnp.float32) + b_ref[...]
    lt = logits.T[:N_EXPERTS]
    ids = lax.broadcasted_iota(jnp.int32, lt.shape, 0).astype(jnp.float32)
    vals, idxs = [], []
    for _ in range(TOP_K):
        m = jnp.max(lt, axis=0, keepdims=True)
        idx = jnp.min(jnp.where(lt == m, ids, float(N_EXPERTS)), axis=0, keepdims=True)
        vals.append(m)
        idxs.append(idx)
        lt = jnp.where(ids == idx, -jnp.inf, lt)
    top_v = jnp.concatenate(vals, axis=0)
    e = jnp.exp(top_v - vals[0])
    e_ref[...] = jnp.concatenate(idxs, axis=0).astype(jnp.int32)
    p_ref[...] = e / jnp.sum(e, axis=0, keepdims=True)


def moe_router(x, gain, w_router, b_router, *, block_m=512):
    m, d = x.shape
    bm = _row_block(m, block_m)
    w = jnp.zeros((d, LANES), jnp.bfloat16).at[:, :N_EXPERTS].set(w_router.astype(jnp.bfloat16))
    b = jnp.zeros((1, LANES), jnp.float32).at[0, :N_EXPERTS].set(b_router.astype(jnp.float32))
    return pl.pallas_call(
        _router_kernel,
        grid=(m // bm,),
        in_specs=[pl.BlockSpec((bm, d), lambda i: (i, 0)),
                  pl.BlockSpec((1, d), lambda i: (0, 0)),
                  pl.BlockSpec((d, LANES), lambda i: (0, 0)),
                  pl.BlockSpec((1, LANES), lambda i: (0, 0))],
        out_specs=[pl.BlockSpec((bm * (d // LANES), LANES), lambda i: (i, 0)),
                   pl.BlockSpec((TOP_K, bm), lambda i: (0, i)),
                   pl.BlockSpec((TOP_K, bm), lambda i: (0, i))],
        out_shape=[jax.ShapeDtypeStruct((m * (d // LANES), LANES), jnp.float32),
                   jax.ShapeDtypeStruct((TOP_K, m), jnp.int32),
                   jax.ShapeDtypeStruct((TOP_K, m), jnp.float32)],
        compiler_params=_params("parallel"),
        name="moe_router",
    )(x, gain.reshape(1, d).astype(jnp.float32), w, b)


def _expert_kernel(blk_e_ref, blk_rows_ref, tok_cur_ref, tok_nxt_ref, h_hbm, wg_ref, wu_ref, bg_ref, bu_ref,
                   wd_ref, bd_ref, o_ref, x_stage, x_cur, sem, *, nf):
    i, j = pl.program_id(0), pl.program_id(1)
    rows = blk_rows_ref[i]
    share = MOE_BLOCK_ROWS // nf

    segs = x_stage.shape[0] // MOE_BLOCK_ROWS

    def row_copy(tok, r):
        return pltpu.make_async_copy(h_hbm.at[pl.ds(pl.multiple_of(tok * segs, segs), segs)],
                                     x_stage.at[pl.ds(pl.multiple_of(r * segs, segs), segs)], sem.at[0])

    def for_all_rows(fn):
        def body(r, carry):
            fn(r)
            return carry
        lax.fori_loop(0, MOE_BLOCK_ROWS, body, 0, unroll=8)

    @pl.when(j == 0)
    def _():
        o_ref[...] = jnp.broadcast_to(bd_ref[0], o_ref.shape)

        @pl.when(i == 0)
        def _():
            for_all_rows(lambda r: row_copy(tok_cur_ref[0, 0, r], r).start())

        @pl.when((i == 0) | (blk_rows_ref[jnp.maximum(i - 1, 0)] > 0))
        def _():
            for_all_rows(lambda r: row_copy(0, r).wait())

        @pl.when(rows > 0)
        def _():
            for a in range(segs):
                x_cur[:, a * LANES:(a + 1) * LANES] = (
                    x_stage[pl.ds(a, MOE_BLOCK_ROWS, stride=segs), :].astype(jnp.bfloat16))

    @pl.when(rows > 0)
    def _():
        for r in range(share):
            row_copy(tok_nxt_ref[0, 0, j * share + r], j * share + r).start(priority=r % 2)
        x = x_cur[...]
        a = jnp.dot(x, wg_ref[0].astype(jnp.bfloat16), preferred_element_type=jnp.float32) + bg_ref[0]
        u = jnp.dot(x, wu_ref[0].astype(jnp.bfloat16), preferred_element_type=jnp.float32) + bu_ref[0]
        a = jnp.minimum(a, SWIGLU_LIMIT)
        u = jnp.clip(u, -SWIGLU_LIMIT, SWIGLU_LIMIT)
        act = a * jax.nn.sigmoid(SWIGLU_ALPHA * a) * (u + 1.0)
        o_ref[...] += jnp.dot(act.astype(jnp.bfloat16), wd_ref[0].astype(jnp.bfloat16),
                              preferred_element_type=jnp.float32)


def moe_experts(h, slot_tok, blk_e, blk_rows, w_gu, b_gu, w_down, b_down):
    s_total = slot_tok.shape[0]
    d = w_gu.shape[1]
    n_e, _, two_ff = w_gu.shape
    ff = two_ff // 2
    nb = s_total // MOE_BLOCK_ROWS
    nf = ff // MOE_FF_TILE
    tok3 = slot_tok.reshape(nb, 1, MOE_BLOCK_ROWS)

    def col(i, j, rows_ref):
        return jnp.where(rows_ref[i] > 0, j, nf - 1)

    grid_spec = pltpu.PrefetchScalarGridSpec(
        num_scalar_prefetch=2,
        grid=(nb, nf),
        in_specs=[
            pl.BlockSpec((1, 1, MOE_BLOCK_ROWS), lambda i, j, e, r: (i, 0, 0), memory_space=pltpu.SMEM),
            pl.BlockSpec((1, 1, MOE_BLOCK_ROWS), lambda i, j, e, r: (jnp.minimum(i + 1, nb - 1), 0, 0),
                         memory_space=pltpu.SMEM),
            pl.BlockSpec(memory_space=pl.ANY),
            pl.BlockSpec((1, d, MOE_FF_TILE), lambda i, j, e, r: (e[i], 0, col(i, j, r))),
            pl.BlockSpec((1, d, MOE_FF_TILE), lambda i, j, e, r: (e[i], 0, nf + col(i, j, r))),
            pl.BlockSpec((1, 1, MOE_FF_TILE), lambda i, j, e, r: (e[i], 0, col(i, j, r))),
            pl.BlockSpec((1, 1, MOE_FF_TILE), lambda i, j, e, r: (e[i], 0, nf + col(i, j, r))),
            pl.BlockSpec((1, MOE_FF_TILE, d), lambda i, j, e, r: (e[i], col(i, j, r), 0)),
            pl.BlockSpec((1, 1, d), lambda i, j, e, r: (e[i], 0, 0)),
        ],
        out_specs=pl.BlockSpec((MOE_BLOCK_ROWS, d), lambda i, j, e, r: (i, 0)),
        scratch_shapes=[pltpu.VMEM((MOE_BLOCK_ROWS * (d // LANES), LANES), jnp.float32),
                        pltpu.VMEM((MOE_BLOCK_ROWS, d), jnp.bfloat16),
                        pltpu.SemaphoreType.DMA((1,))],
    )
    b_gu3 = b_gu.reshape(n_e, 1, two_ff)
    return pl.pallas_call(
        functools.partial(_expert_kernel, nf=nf),
        grid_spec=grid_spec,
        out_shape=jax.ShapeDtypeStruct((s_total, d), jnp.float32),
        compiler_params=_params("arbitrary", "arbitrary"),
        name="moe_experts",
    )(blk_e, blk_rows, tok3, tok3, h, w_gu, w_gu, b_gu3, b_gu3, w_down, b_down.reshape(n_e, 1, d))


def moe_layer(x_groups, gain, w_router, b_router, w_gu, b_gu, w_down, b_down, g_final):
    routed = [moe_router(x, gain, w_router, b_router) for x in x_groups]
    h = jnp.concatenate([r[0] for r in routed], axis=0)
    top_e = jnp.concatenate([r[1] for r in routed], axis=1)
    top_p = jnp.concatenate([r[2] for r in routed], axis=1)
    t = top_e.shape[1]
    bm = MOE_BLOCK_ROWS
    n_blocks = (t * TOP_K) // bm + N_EXPERTS + 1
    onehot = (top_e[:, :, None] == jnp.arange(N_EXPERTS)[None, None, :]).astype(jnp.int32)
    per_tok = jnp.sum(onehot, axis=0)
    before = jnp.cumsum(per_tok, axis=0) - per_tok
    counts = jnp.sum(per_tok, axis=0)
    blocks_e = (counts + bm - 1) // bm
    blk_end = jnp.cumsum(blocks_e)
    blk_start = blk_end - blocks_e
    first_slot = before + (blk_start * bm)[None, :]
    slot = jnp.sum(onehot * first_slot[None], axis=-1).astype(jnp.int32)
    tok_ids = jnp.broadcast_to(jnp.arange(t, dtype=jnp.int32)[None], slot.shape)
    slot_tok = jnp.zeros((n_blocks * bm,), jnp.int32).at[slot.reshape(-1)].set(tok_ids.reshape(-1))
    blk = jnp.arange(n_blocks)
    blk_e = jnp.minimum(jnp.searchsorted(blk_end, blk, side='right'), N_EXPERTS - 1).astype(jnp.int32)
    used = blk < blk_end[-1]
    last_e = blk_e[jnp.maximum(blk_end[-1] - 1, 0)]
    blk_e = jnp.where(used, blk_e, last_e).astype(jnp.int32)
    blk_rows = jnp.where(used, jnp.clip(counts[blk_e] - (blk - blk_start[blk_e]) * bm, 0, bm), 0).astype(jnp.int32)
    outs = moe_experts(h, slot_tok, blk_e, blk_rows, w_gu, b_gu, w_down, b_down)
    ys, start = [], 0
    for x in x_groups:
        m = x.shape[0]
        ys.append(moe_combine_norm(x, outs, slot[:, start:start + m], top_p[:, start:start + m], g_final))
        start += m
    return ys


def _combine_kernel(idx_cur_ref, idx_nxt_ref, x_ref, p_ref, g_ref, outs_hbm, o_ref, buf, sem, *, bm):
    i = pl.program_id(0)
    cur = lax.rem(i, 2)
    n_rows = TOP_K * bm

    def row_copy(slot, dst, r):
        return pltpu.make_async_copy(outs_hbm.at[pl.ds(slot, 1)], buf.at[dst, pl.ds(r, 1)], sem.at[dst])

    def for_rows(fn):
        def body(g, carry):
            for k in range(8):
                fn(g * 8 + k, k)
            return carry
        lax.fori_loop(0, n_rows // 8, body, 0)

    @pl.when(i == 0)
    def _():
        for_rows(lambda r, k: row_copy(idx_cur_ref[0, 0, r], cur, r).start(priority=k % 2))

    for_rows(lambda r, k: row_copy(0, cur, r).wait())

    @pl.when(i + 1 < pl.num_programs(0))
    def _():
        for_rows(lambda r, k: row_copy(idx_nxt_ref[0, 0, r], 1 - cur, r).start(priority=k % 2))

    x = x_ref[...]
    for k in range(TOP_K):
        x = x + p_ref[:, k:k + 1] * buf[cur, k * bm:(k + 1) * bm, :]
    o_ref[...] = x * lax.rsqrt(jnp.mean(x * x, axis=-1, keepdims=True) + EPS) * g_ref[...]


def moe_combine_norm(x, outs, slot, gate, gain, *, block_m=128):
    m, d = x.shape
    bm = _row_block(m, block_m)
    nt = m // bm
    idx = slot.reshape(TOP_K, nt, bm).transpose(1, 0, 2).reshape(nt, 1, TOP_K * bm)
    return pl.pallas_call(
        functools.partial(_combine_kernel, bm=bm),
        grid=(nt,),
        in_specs=[pl.BlockSpec((1, 1, TOP_K * bm), lambda i: (i, 0, 0), memory_space=pltpu.SMEM),
                  pl.BlockSpec((1, 1, TOP_K * bm), lambda i: (jnp.minimum(i + 1, nt - 1), 0, 0),
                               memory_space=pltpu.SMEM),
                  pl.BlockSpec((bm, d), lambda i: (i, 0)),
                  pl.BlockSpec((bm, TOP_K), lambda i: (i, 0)),
                  pl.BlockSpec((1, d), lambda i: (0, 0)),
                  pl.BlockSpec(memory_space=pl.ANY)],
        out_specs=pl.BlockSpec((bm, d), lambda i: (i, 0)),
        out_shape=jax.ShapeDtypeStruct((m, d), jnp.float32),
        scratch_shapes=[pltpu.VMEM((2, TOP_K * bm, d), jnp.float32), pltpu.SemaphoreType.DMA((2,))],
        compiler_params=_params("arbitrary"),
        name="moe_combine_norm",
    )(idx, idx, x, gate.T, gain.reshape(1, d).astype(jnp.float32), outs)


def _hist_compress_kernel(pt_ref, *refs):
    pages, (w1_ref, pe_ref, w2_ref, o_ref, slab_ref) = refs[:CMP_PAGES + 1], refs[CMP_PAGES + 1:]
    c = CMP_PAGES * CHUNKS_PER_PAGE
    last = lax.broadcasted_iota(jnp.int32, (c, 1), 0) == c - 1

    slab_rows = CMP_STRIDE * KV_ROWS
    regs_per_chunk = slab_rows // 8
    for k in range(CMP_PAGES + 1):
        chunks = CHUNKS_PER_PAGE if k < CMP_PAGES else 1
        if k == CMP_PAGES:
            for s in range(slab_rows):
                slab_ref[s * SLAB_PITCH + c:s * SLAB_PITCH + c + 8, :] = jnp.zeros((8, HEAD_DIM), jnp.float32)
        for cc in range(chunks):
            for sg in range(regs_per_chunk):
                v = cc * regs_per_chunk + sg
                slab_ref[pl.ds(8 * sg * SLAB_PITCH + k * CHUNKS_PER_PAGE + cc, 8, stride=SLAB_PITCH), :] = (
                    pages[k][v * 8:(v + 1) * 8, :])

    def rows(sub, lo, n):
        return jnp.concatenate([slab_ref[(i * KV_ROWS + sub) * SLAB_PITCH + lo:(i * KV_ROWS + sub) * SLAB_PITCH + lo + n, :]
                                for i in range(CMP_STRIDE)], axis=1)

    for j in range(2):
        w_a = w1_ref[j, :, :HEAD_DIM]
        w_b = w1_ref[j, :, HEAD_DIM:]
        pe = pe_ref[j]
        bias = (jnp.dot(pe[:, :CMP_HALF], w_a, preferred_element_type=jnp.float32)
                + jnp.dot(pe[:, CMP_HALF:], w_b, preferred_element_type=jnp.float32))[0:1]
        bases = [j * N_KV + g for g in range(N_KV)]
        x = jnp.concatenate([rows(base, 0, c) for base in bases], axis=0).astype(jnp.bfloat16)
        x_next = jnp.concatenate([rows(base, c, CHUNKS_PER_PAGE) for base in bases], axis=0).astype(jnp.bfloat16)
        ab = jnp.dot(x, w1_ref[j], preferred_element_type=jnp.float32)
        a, b = ab[:, :HEAD_DIM], ab[:, HEAD_DIM:]
        b_tail = jnp.dot(x_next, w_b, preferred_element_type=jnp.float32)
        for g in range(N_KV):
            b_next = jnp.where(last, b_tail[g * CHUNKS_PER_PAGE:g * CHUNKS_PER_PAGE + 1],
                               pltpu.roll(b[g * c:(g + 1) * c], c - 1, axis=0))
            h = jax.nn.gelu(a[g * c:(g + 1) * c] + b_next + bias)
            o_ref[0, j, g] = jnp.dot(h.astype(jnp.bfloat16), w2_ref[j],
                                     preferred_element_type=jnp.float32).astype(o_ref.dtype)


def compress_history(pool, page_table, cmp_pe, cmp_w1, cmp_w2):
    n, n_pages = page_table.shape
    assert n_pages % CMP_PAGES == 0
    c = n_pages * CHUNKS_PER_PAGE
    w1, pe, w2 = _compress_weights(cmp_pe, cmp_w1, cmp_w2)

    def page_spec(k):
        return pl.BlockSpec((PAGE_SIZE * KV_ROWS, HEAD_DIM),
                            lambda b, s, pt: (pt[b * n_pages + jnp.minimum(s * CMP_PAGES + k, n_pages - 1)], 0))

    grid_spec = pltpu.PrefetchScalarGridSpec(
        num_scalar_prefetch=1,
        grid=(n, n_pages // CMP_PAGES),
        in_specs=[page_spec(k) for k in range(CMP_PAGES + 1)] + [
            pl.BlockSpec(w1.shape, lambda b, s, pt: (0, 0, 0)),
            pl.BlockSpec(pe.shape, lambda b, s, pt: (0, 0, 0)),
            pl.BlockSpec(w2.shape, lambda b, s, pt: (0, 0, 0))],
        out_specs=pl.BlockSpec((1, 2, N_KV, CMP_PAGES * CHUNKS_PER_PAGE, HEAD_DIM), lambda b, s, pt: (b, 0, 0, s, 0)),
        scratch_shapes=[pltpu.VMEM((CMP_STRIDE * KV_ROWS * SLAB_PITCH, HEAD_DIM), jnp.float32)],
    )
    return pl.pallas_call(
        _hist_compress_kernel,
        grid_spec=grid_spec,
        out_shape=jax.ShapeDtypeStruct((n, 2, N_KV, c, HEAD_DIM), jnp.bfloat16),
        compiler_params=_params("parallel", "arbitrary"),
        name="compress_history",
    )(page_table.reshape(-1), *([pool] * (CMP_PAGES + 1)), w1, pe, w2)


def _nsa_sample_dense_kernel(q_ref, ckv_ref, kw_ref, ocmp_ref, owin_ref, idx_ref, val_ref, *, past, n_cmp, n_sel):
    rows = HPG * DEC_PAD
    tok = lax.bitwise_and(lax.broadcasted_iota(jnp.int32, (rows, 1), 0), DEC_PAD - 1)
    qpos = past + tok
    qpos_l = past + lax.broadcasted_iota(jnp.int32, (1, LANES), 1)
    c_all = ckv_ref.shape[3]
    n_sel_pad = -(-n_sel // 8) * 8
    wlen = kw_ref.shape[1]
    for g in range(N_KV):
        qg = q_ref[0, g]
        s = lax.dot_general(qg, ckv_ref[0, 0, g], _NT, preferred_element_type=jnp.float32)
        cidx = lax.broadcasted_iota(jnp.int32, (1, c_all), 1)
        vis = (cidx * CMP_STRIDE + (CMP_LEN - 1) <= qpos) & (cidx < n_cmp)
        p = _softmax_rows(s, vis)
        ocmp_ref[0, g] = jnp.dot(p.astype(jnp.bfloat16), ckv_ref[0, 1, g], preferred_element_type=jnp.float32)
        p_grp = p[0:DEC_PAD]
        for hh in range(1, HPG):
            p_grp = p_grp + p[hh * DEC_PAD:(hh + 1) * DEC_PAD]
        p_grp = jnp.concatenate([p_grp, jnp.zeros((LANES - DEC_PAD, c_all), jnp.float32)], axis=0)
        c0 = lax.broadcasted_iota(jnp.int32, (n_sel_pad, c_all), 1) * CMP_STRIDE
        s0 = lax.broadcasted_iota(jnp.int32, (n_sel_pad, c_all), 0) * SEL_BLOCK
        overlap_t = ((c0 < s0 + SEL_BLOCK) & (c0 + CMP_LEN > s0)).astype(jnp.bfloat16)
        score_t = lax.dot_general(overlap_t, p_grp.astype(jnp.bfloat16), _NT, preferred_element_type=jnp.float32)
        blk = lax.broadcasted_iota(jnp.int32, score_t.shape, 0)
        cur = lax.shift_right_logical(qpos_l, SEL_BLOCK.bit_length() - 1)
        avail = (blk <= cur) & (blk < n_sel)
        forced = avail & ((blk == 0) | (blk == cur) | (blk == cur - 1))
        work = jnp.where(forced, jnp.inf, jnp.where(avail, score_t, -jnp.inf))
        blk_f = blk.astype(jnp.float32)
        idxs, vals = [], []
        for _ in range(SEL_TOPK):
            m = jnp.max(work, axis=0, keepdims=True)
            first = jnp.min(jnp.where(work == m, blk_f, float(n_sel_pad)), axis=0, keepdims=True)
            idxs.append(first)
            vals.append(jnp.where(m > -jnp.inf, 1.0, 0.0))
            work = jnp.where(blk_f == first, -jnp.inf, work)
        idx_ref[0, g] = jnp.concatenate(idxs, axis=0).astype(jnp.int32)
        val_ref[0, g] = jnp.concatenate(vals, axis=0).astype(jnp.int32)

        kcol = slice(g * HEAD_DIM, (g + 1) * HEAD_DIM)
        vcol = slice((N_KV + g) * HEAD_DIM, (N_KV + g + 1) * HEAD_DIM)
        s = lax.dot_general(qg, kw_ref[0, :, kcol].astype(jnp.bfloat16), _NT, preferred_element_type=jnp.float32)
        kpos = past - WINDOW + lax.broadcasted_iota(jnp.int32, (1, wlen), 1)
        p = _softmax_rows(s, (kpos <= qpos) & (kpos > qpos - WINDOW))
        owin_ref[0, g] = jnp.dot(p.astype(jnp.bfloat16), kw_ref[0, :, vcol].astype(jnp.bfloat16),
                                 preferred_element_type=jnp.float32)


def nsa_sample_dense(qd, ckv, kw, *, past, n_cmp, n_sel):
    n = qd.shape[0]
    rows = HPG * DEC_PAD
    o_shape = jax.ShapeDtypeStruct((n, N_KV, rows, HEAD_DIM), jnp.float32)
    i_shape = jax.ShapeDtypeStruct((n, N_KV, SEL_TOPK, LANES), jnp.int32)
    o_spec = pl.BlockSpec((1, N_KV, rows, HEAD_DIM), lambda b: (b, 0, 0, 0))
    i_spec = pl.BlockSpec((1, N_KV, SEL_TOPK, LANES), lambda b: (b, 0, 0, 0))
    return pl.pallas_call(
        functools.partial(_nsa_sample_dense_kernel, past=past, n_cmp=n_cmp, n_sel=n_sel),
        grid=(n,),
        in_specs=[pl.BlockSpec((1, N_KV, rows, HEAD_DIM), lambda b: (b, 0, 0, 0)),
                  pl.BlockSpec((1,) + ckv.shape[1:], lambda b: (b, 0, 0, 0, 0)),
                  pl.BlockSpec((1,) + kw.shape[1:], lambda b: (b, 0, 0))],
        out_specs=[o_spec, o_spec, i_spec, i_spec],
        out_shape=[o_shape, o_shape, i_shape, i_shape],
        compiler_params=_params("parallel"),
        name="nsa_sample_dense",
    )(qd, ckv, kw)


def _nsa_sample_sel_kernel(idx_ref, val_ref, pt_ref, q_ref, new_ref, *refs, tq, past, n_past_blk):
    blocks, o_ref = refs[:N_KV * SEL_TOPK], refs[N_KV * SEL_TOPK]
    b, t = pl.program_id(0), pl.program_id(1)
    qpos = past + t
    lane = lax.broadcasted_iota(jnp.int32, (1, SEL_TOPK * SEL_BLOCK), 1)
    slot_of_lane = lax.shift_right_logical(lane, SEL_BLOCK.bit_length() - 1)
    for g in range(N_KV):
        base = ((b * tq + t) * N_KV + g) * SEL_TOPK
        ks, vs = [], []
        tokpos = jnp.zeros(lane.shape, jnp.int32)
        ok = jnp.zeros(lane.shape, jnp.int32)
        for s in range(SEL_TOPK):
            blk = idx_ref[base + s]
            from_pool = blk < n_past_blk
            pool_blk = blocks[g * SEL_TOPK + s]
            k_rows = pl.ds(g, SEL_BLOCK, stride=KV_ROWS)
            v_rows = pl.ds(N_KV + g, SEL_BLOCK, stride=KV_ROWS)
            ks.append(jnp.where(from_pool, pool_blk[k_rows, :], new_ref[0, k_rows, :]))
            vs.append(jnp.where(from_pool, pool_blk[v_rows, :], new_ref[0, v_rows, :]))
            here = slot_of_lane == s
            tokpos = jnp.where(here, blk * SEL_BLOCK, tokpos)
            ok = jnp.where(here, val_ref[base + s], ok)
        tokpos = tokpos + lax.bitwise_and(lane, SEL_BLOCK - 1)
        mask = (ok > 0) & (tokpos <= qpos)
        k = jnp.concatenate(ks, axis=0).astype(jnp.bfloat16)
        v = jnp.concatenate(vs, axis=0).astype(jnp.bfloat16)
        s = lax.dot_general(q_ref[0, 0, g], k, _NT, preferred_element_type=jnp.float32)
        p = _softmax_rows(s, mask)
        o_ref[0, 0, g] = jnp.dot(p.astype(jnp.bfloat16), v, preferred_element_type=jnp.float32)


def nsa_sample_sel(qs, idx, valid, page_table, pool, new_blocks, *, past):
    n, tq = qs.shape[:2]
    n_pages = page_table.shape[1]
    bpp = PAGE_SIZE // SEL_BLOCK
    n_past_blk = n_pages * bpp

    def block_spec(g, s):
        def index(b, t, idx_ref, val_ref, pt_ref):
            blk = jnp.minimum(idx_ref[((b * tq + t) * N_KV + g) * SEL_TOPK + s], n_past_blk - 1)
            return (pt_ref[b * n_pages + blk // bpp] * bpp + blk % bpp, 0)
        return pl.BlockSpec((SEL_BLOCK * KV_ROWS, HEAD_DIM), index)

    grid_spec = pltpu.PrefetchScalarGridSpec(
        num_scalar_prefetch=3,
        grid=(n, tq),
        in_specs=[pl.BlockSpec((1, 1, N_KV, 16, HEAD_DIM), lambda b, t, *_: (b, t, 0, 0, 0)),
                  pl.BlockSpec((1, SEL_BLOCK * KV_ROWS, HEAD_DIM), lambda b, t, *_: (b, 0, 0))]
                 + [block_spec(g, s) for g in range(N_KV) for s in range(SEL_TOPK)],
        out_specs=pl.BlockSpec((1, 1, N_KV, 16, HEAD_DIM), lambda b, t, *_: (b, t, 0, 0, 0)),
    )
    return pl.pallas_call(
        functools.partial(_nsa_sample_sel_kernel, tq=tq, past=past, n_past_blk=n_past_blk),
        grid_spec=grid_spec,
        out_shape=jax.ShapeDtypeStruct((n, tq, N_KV, 16, HEAD_DIM), jnp.float32),
        compiler_params=_params("parallel", "arbitrary"),
        name="nsa_sample_sel",
    )(idx.reshape(-1), valid.reshape(-1), page_table.reshape(-1), qs, new_blocks,
      *([pool] * (N_KV * SEL_TOPK)))


def nsa_sample(q, gates, kvs_new, kvw_new, pool_cmp, pool_sel, win_cache, page_table, cmp_pe, cmp_w1, cmp_w2):
    n, tq, _ = q.shape
    past = page_table.shape[1] * PAGE_SIZE
    assert tq <= DEC_PAD and tq <= SEL_BLOCK and win_cache.shape[1] == WINDOW
    assert (past + tq) // CMP_STRIDE * CMP_STRIDE == past
    n_cmp = past // CMP_STRIDE - 1
    n_sel = -(-(past + tq) // SEL_BLOCK)
    ckv = compress_history(pool_cmp, page_table, cmp_pe, cmp_w1, cmp_w2)
    q5 = q.reshape(n, tq, N_KV, HPG, HEAD_DIM)
    qd = jnp.pad(q5.transpose(0, 2, 3, 1, 4), ((0, 0), (0, 0), (0, 0), (0, DEC_PAD - tq), (0, 0)))
    qd = qd.reshape(n, N_KV, HPG * DEC_PAD, HEAD_DIM)
    kw_all = jnp.concatenate([win_cache, kvw_new], axis=1)
    kw = jnp.pad(kw_all, ((0, 0), (0, -(-(WINDOW + DEC_PAD) // LANES) * LANES - WINDOW - tq), (0, 0)))
    o_cmp, o_win, idx, valid = nsa_sample_dense(qd, ckv, kw, past=past, n_cmp=n_cmp, n_sel=n_sel)
    idx = idx[..., :tq].transpose(0, 3, 1, 2)
    valid = valid[..., :tq].transpose(0, 3, 1, 2)
    qs = jnp.pad(q5, ((0, 0), (0, 0), (0, 0), (0, 16 - HPG), (0, 0)))
    new_blocks = jnp.pad(kvs_new, ((0, 0), (0, SEL_BLOCK - tq), (0, 0))).reshape(n, SEL_BLOCK * KV_ROWS, HEAD_DIM)
    o_sel = nsa_sample_sel(qs, idx, valid, page_table, pool_sel, new_blocks, past=past)[:, :, :, :HPG]
    unpack = lambda o: o.reshape(n, N_KV, HPG, DEC_PAD, HEAD_DIM)[:, :, :, :tq].transpose(0, 3, 1, 2, 4)
    g = gates[..., :GATE_W].reshape(n, tq, 3, N_KV, HPG)[..., None]
    o = g[:, :, 0] * unpack(o_cmp) + g[:, :, 1] * o_sel + g[:, :, 2] * unpack(o_win)
    return o.reshape(n, tq, Q_W), kw_all


def prepare_weights(w_in, w_nsa_o, w_conv_o, w_out, w_xq, w_xkv, w_xo):
    bf = jnp.bfloat16
    w = w_in.astype(bf)
    d = w.shape[0]
    s = IN_SPLITS
    return dict(
        q=w[:, :s[0]], kv=w[:, s[0]:s[3]],
        gates=jnp.zeros((d, LANES), bf).at[:, :GATE_W].set(w[:, s[3]:s[4]]),
        glu_a=w[:, s[4]:s[4] + CONV_CH], glu_b=w[:, s[4] + CONV_CH:s[5]], merge=w[:, s[5]:],
        nsa_o=w_nsa_o.astype(bf), conv_o=w_conv_o.astype(bf), out=w_out.astype(bf),
        xq=w_xq.astype(bf), xkv=w_xkv.astype(bf), xo=w_xo.astype(bf))


def front(x2, g_mix, wts):
    q = fused_linear(x2, [wts['q']], gain=g_mix, out_dtype=jnp.bfloat16, name="front_q")
    kv = fused_linear(x2, [wts['kv']], gain=g_mix, name="front_kv")
    gates = fused_linear(x2, [wts['gates']], gain=g_mix, epilogue=_sigmoid_epilogue, name="front_gates")
    up = fused_linear(x2, [wts['glu_a'], wts['glu_b']], gain=g_mix, epilogue=_glu_epilogue, name="front_glu")
    mg = fused_linear(x2, [wts['merge']], gain=g_mix, epilogue=_sigmoid_epilogue, name="front_merge")
    return q, kv, gates, up, mg


def mixer_tail(x2, o, act, mg, wts):
    merged = merge_branches(o, act, wts['nsa_o'], wts['conv_o'], mg)
    return fused_linear(merged, [wts['out']], tiles=[x2], epilogue=_residual_epilogue, name="mixer_out_proj")


def cross_attn_block(x2, n, mkv, g_xattn, wts):
    m = x2.shape[0]
    qx = fused_linear(x2, [wts['xq']], gain=g_xattn, out_dtype=jnp.bfloat16, name="xattn_q")
    oc = cross_attn_core(qx.reshape(n, m // n, MEM_W), mkv)
    return fused_linear(oc.reshape(m, MEM_W), [wts['xo']], tiles=[x2], epilogue=_residual_epilogue, name="xattn_o")


def kernel(x_prompt, x_sample, cache_kv_cmp, cache_kv_sel, cache_kv_win, cache_conv, cache_mem_kv, page_table,
           mem_prompt, g_mix, w_in, cmp_pe, cmp_w1, cmp_w2, w_nsa_o, w_dw, b_dw, ln_conv_g, ln_conv_b, w_conv_o,
           w_out, g_xattn, g_mem, w_xq, w_xkv, w_xo, g_moe, w_router, b_router, w_gu, b_gu, w_down, b_down, g_final):
    nb, t, d = x_prompt.shape
    nd, tq = x_sample.shape[:2]
    l = 0
    bf = jnp.bfloat16
    wts = prepare_weights(w_in[l], w_nsa_o[l], w_conv_o[l], w_out[l], w_xq[l], w_xkv[l], w_xo[l])
    kv_shape = lambda n_, t_: (n_, t_, 2, N_KV, HEAD_DIM)

    xp2 = x_prompt.reshape(nb * t, d)
    q, kv, gates, up_new, mg = front(xp2, g_mix[l], wts)
    kvc, kvs, kvw = (kv[:, i * KV_W:(i + 1) * KV_W].reshape(nb, t, KV_W) for i in range(3))
    ckv = compress_prompt(kvc, cmp_pe[l], cmp_w1[l], cmp_w2[l])
    o = nsa_prompt(q.reshape(nb, t, Q_W), gates.reshape(nb, t, LANES), ckv, kvs.astype(bf), kvw.astype(bf))
    up_new = up_new.reshape(nb, t, CONV_CH)
    act = conv_act(jnp.zeros((nb, CONV_WIDTH - 1, CONV_CH), jnp.float32), up_new, w_dw[l], b_dw[l],
                   ln_conv_g[l], ln_conv_b[l])
    xp = mixer_tail(xp2, o.reshape(nb * t, Q_W), act.reshape(nb * t, CONV_CH), mg, wts)
    mkv = fused_linear(mem_prompt.reshape(nb * MEM_LEN, d), [wts['xkv']], gain=g_mem[l], name="mem_kv")
    xp = cross_attn_block(xp, nb, mkv.reshape(nb, MEM_LEN, 2 * MEM_W), g_xattn[l], wts)
    p_kv_cmp, p_kv_sel = kvc.reshape(kv_shape(nb, t))[None], kvs.reshape(kv_shape(nb, t))[None]
    p_kv_win = kvw.reshape(kv_shape(nb, t))[:, t - min(WINDOW, t):][None]
    p_conv = jnp.pad(up_new, ((0, 0), (CONV_WIDTH - 1, 0), (0, 0)))[:, t:][None]
    p_mem = mkv.reshape(nb, MEM_LEN, 2, MEM_HEADS, MEM_HEAD_DIM)[None]

    xs2 = x_sample.reshape(nd * tq, d)
    q, kv, gates, up_new, mg = front(xs2, g_mix[l], wts)
    kvc, kvs, kvw = (kv[:, i * KV_W:(i + 1) * KV_W].reshape(nd, tq, KV_W) for i in range(3))
    o, win_all = nsa_sample(q.reshape(nd, tq, Q_W), gates.reshape(nd, tq, LANES), kvs, kvw,
                            cache_kv_cmp[l].reshape(-1, HEAD_DIM), cache_kv_sel[l].reshape(-1, HEAD_DIM),
                            cache_kv_win[l].reshape(nd, -1, KV_W), page_table, cmp_pe[l], cmp_w1[l], cmp_w2[l])
    win_new = win_all[:, -min(WINDOW, win_all.shape[1]):].reshape(kv_shape(nd, min(WINDOW, win_all.shape[1])))
    kvc, kvs = kvc.reshape(kv_shape(nd, tq)), kvs.reshape(kv_shape(nd, tq))
    up_new = up_new.reshape(nd, tq, CONV_CH)
    act = conv_act(cache_conv[l], up_new, w_dw[l], b_dw[l], ln_conv_g[l], ln_conv_b[l])
    xs = mixer_tail(xs2, o.reshape(nd * tq, Q_W).astype(bf), act.reshape(nd * tq, CONV_CH), mg, wts)
    xs = cross_attn_block(xs, nd, cache_mem_kv[l].reshape(nd, MEM_LEN, 2 * MEM_W), g_xattn[l], wts)
    s_conv = jnp.concatenate([cache_conv[l], up_new], axis=1)[:, tq:][None]

    y_prompt, y_sample = moe_layer([xp, xs], g_moe[l], w_router[l], b_router[l], w_gu[l], b_gu[l], w_down[l],
                                   b_down[l], g_final)
    y_prompt, y_sample = y_prompt.reshape(nb, t, d), y_sample.reshape(nd, tq, d)
    return (y_prompt, y_sample, p_kv_cmp, p_kv_sel, p_kv_win, p_conv, p_mem, kvc[None], kvs[None], win_new[None],
            s_conv)
```

```python
import functools

import jax
import jax.numpy as jnp
from jax import lax
from jax.experimental import pallas as pl
from jax.experimental.pallas import tpu as pltpu

D_MODEL = 2048
PAGE_SIZE = 128

N_HEADS = 16
HEAD_DIM = 128
N_KV = 2
HPG = N_HEADS // N_KV
CMP_STRIDE = 16
CMP_LEN = 2 * CMP_STRIDE
CMP_HALF = CMP_STRIDE * HEAD_DIM
SEL_BLOCK = 64
SEL_TOPK = 16
WINDOW = 512
SCALE = HEAD_DIM ** -0.5
EXP2_SCALE = SCALE * 1.4426950408889634
CONV_CH = D_MODEL // 2
CONV_WIDTH = 31
MEM_LEN = 256
MEM_HEADS = 4
MEM_HEAD_DIM = 128
MEM_W = MEM_HEADS * MEM_HEAD_DIM
MEM_SCALE = MEM_HEAD_DIM ** -0.5
N_EXPERTS = 32
TOP_K = 4
SWIGLU_LIMIT = 7.0
SWIGLU_ALPHA = 1.702
EPS = 1e-6
Q_W = N_HEADS * HEAD_DIM
KV_W = 2 * N_KV * HEAD_DIM
KV_ROWS = KV_W // HEAD_DIM
GATE_W = 3 * N_HEADS
GLU_W = 2 * CONV_CH
MERGE_W = 2 * D_MODEL
IN_SPLITS = [Q_W, Q_W + KV_W, Q_W + 2 * KV_W, Q_W + 3 * KV_W, Q_W + 3 * KV_W + GATE_W, Q_W + 3 * KV_W + GATE_W + GLU_W]

V7X_VMEM_BYTES = 64 * 1024 * 1024
VMEM_LIMIT_BYTES = V7X_VMEM_BYTES - 8 * 1024 * 1024
LANES = 128

MOE_BLOCK_ROWS = 1152
MOE_FF_TILE = 256
NSA_Q_TILE = 128
NSA_KV_TILE = 1024
CONV_CTX = 32
CONV_ROWS = 64
CHUNKS_PER_PAGE = PAGE_SIZE // CMP_STRIDE
CMP_PAGES = 32
SLAB_PITCH = CMP_PAGES * CHUNKS_PER_PAGE + 8
DEC_PAD = 8

NEG = -1e30
_NT = (((1,), (1,)), ((), ()))


def _params(*semantics):
    return pltpu.CompilerParams(dimension_semantics=semantics, vmem_limit_bytes=VMEM_LIMIT_BYTES)


def _row_block(rows, target):
    if rows <= target:
        return rows
    for b in range(target, 15, -1):
        if rows % b == 0 and b % 16 == 0:
            return b
    return rows


def _linear_kernel(*refs, n_w, norm, n_tile, n_row, epilogue):
    x_ref = refs[0]
    k = 1
    g_ref = refs[k] if norm else None
    k += int(norm)
    w_refs = refs[k:k + n_w]
    k += n_w
    tile_refs = refs[k:k + n_tile]
    k += n_tile
    row_refs = refs[k:k + n_row]
    k += n_row
    o_ref, h_ref = refs[k], refs[k + 1]

    @pl.when(pl.program_id(1) == 0)
    def _():
        x = x_ref[...].astype(jnp.float32)
        if norm:
            x = x * lax.rsqrt(jnp.mean(x * x, axis=-1, keepdims=True) + EPS) * g_ref[...]
        h_ref[...] = x.astype(jnp.bfloat16)

    h = h_ref[...]
    zs = [jnp.dot(h, w[...], preferred_element_type=jnp.float32) for w in w_refs]
    o_ref[...] = epilogue(zs, [t[...] for t in tile_refs], [r[...] for r in row_refs]).astype(o_ref.dtype)


def fused_linear(x, ws, *, gain=None, tiles=(), rows=(), epilogue=None, out_dtype=jnp.float32,
                 block_m=1024, block_n=512, name="fused_linear"):
    m, k = x.shape
    n = ws[0].shape[1]
    bm = _row_block(m, block_m)
    bn = min(block_n, n)
    assert m % bm == 0 and n % bn == 0 and all(w.shape == (k, n) for w in ws)
    if epilogue is None:
        epilogue = _first_epilogue
    norm = gain is not None
    in_specs = [pl.BlockSpec((bm, k), lambda i, j: (i, 0))]
    args = [x]
    if norm:
        in_specs.append(pl.BlockSpec((1, k), lambda i, j: (0, 0)))
        args.append(gain.reshape(1, k).astype(jnp.float32))
    in_specs += [pl.BlockSpec((k, bn), lambda i, j: (0, j)) for _ in ws]
    args += list(ws)
    in_specs += [pl.BlockSpec((bm, bn), lambda i, j: (i, j)) for _ in tiles]
    args += list(tiles)
    in_specs += [pl.BlockSpec((1, bn), lambda i, j: (0, j)) for _ in rows]
    args += list(rows)
    return pl.pallas_call(
        functools.partial(_linear_kernel, n_w=len(ws), norm=norm, n_tile=len(tiles), n_row=len(rows),
                          epilogue=epilogue),
        grid=(m // bm, n // bn),
        in_specs=in_specs,
        out_specs=pl.BlockSpec((bm, bn), lambda i, j: (i, j)),
        out_shape=jax.ShapeDtypeStruct((m, n), out_dtype),
        scratch_shapes=[pltpu.VMEM((bm, k), jnp.bfloat16)],
        compiler_params=_params("parallel", "arbitrary"),
        name=name,
    )(*args)


def _first_epilogue(zs, ts, rs):
    return zs[0]


def _glu_epilogue(zs, ts, rs):
    return zs[0] * jax.nn.sigmoid(zs[1])


def _sigmoid_epilogue(zs, ts, rs):
    return jax.nn.sigmoid(zs[0])


def _residual_epilogue(zs, ts, rs):
    return ts[0] + zs[0]


def _merge_kernel(o_ref, a_ref, wo_ref, wc_ref, ga_ref, gc_ref, out_ref):
    y_att = jnp.dot(o_ref[...], wo_ref[...], preferred_element_type=jnp.float32)
    y_conv = jnp.dot(a_ref[...], wc_ref[...], preferred_element_type=jnp.float32)
    out_ref[...] = (ga_ref[...] * y_att + gc_ref[...] * y_conv).astype(out_ref.dtype)


def merge_branches(o, act, w_nsa_o, w_conv_o, merge_gates, *, block_m=1024, block_n=512):
    m = o.shape[0]
    d = w_nsa_o.shape[1]
    bm = _row_block(m, block_m)
    bn = min(block_n, d)
    nj = d // bn
    return pl.pallas_call(
        _merge_kernel,
        grid=(m // bm, nj),
        in_specs=[pl.BlockSpec((bm, o.shape[1]), lambda i, j: (i, 0)),
                  pl.BlockSpec((bm, act.shape[1]), lambda i, j: (i, 0)),
                  pl.BlockSpec((w_nsa_o.shape[0], bn), lambda i, j: (0, j)),
                  pl.BlockSpec((w_conv_o.shape[0], bn), lambda i, j: (0, j)),
                  pl.BlockSpec((bm, bn), lambda i, j: (i, j)),
                  pl.BlockSpec((bm, bn), lambda i, j: (i, j + nj))],
        out_specs=pl.BlockSpec((bm, bn), lambda i, j: (i, j)),
        out_shape=jax.ShapeDtypeStruct((m, d), jnp.bfloat16),
        compiler_params=_params("parallel", "arbitrary"),
        name="merge_branches",
    )(o, act, w_nsa_o, w_conv_o, merge_gates, merge_gates)


def _compress_kernel(x_ref, w1_ref, pe_ref, w2_ref, o_ref):
    c = x_ref.shape[1]
    for j in range(2):
        w_a = w1_ref[j, :, :HEAD_DIM]
        w_b = w1_ref[j, :, HEAD_DIM:]
        pe = pe_ref[j]
        bias = (jnp.dot(pe[:, :CMP_HALF], w_a, preferred_element_type=jnp.float32)
                + jnp.dot(pe[:, CMP_HALF:], w_b, preferred_element_type=jnp.float32))[0:1]
        xs = []
        for g in range(N_KV):
            base = (j * N_KV + g) * HEAD_DIM
            xs.append(jnp.concatenate(
                [x_ref[0, :, i * KV_W + base:i * KV_W + base + HEAD_DIM] for i in range(CMP_STRIDE)], axis=1))
        x = jnp.concatenate(xs, axis=0).astype(jnp.bfloat16)
        ab = jnp.dot(x, w1_ref[j], preferred_element_type=jnp.float32)
        a, b = ab[:, :HEAD_DIM], ab[:, HEAD_DIM:]
        for g in range(N_KV):
            b_next = pltpu.roll(b[g * c:(g + 1) * c], c - 1, axis=0)
            h = jax.nn.gelu(a[g * c:(g + 1) * c] + b_next + bias)
            o_ref[0, j, g] = jnp.dot(h.astype(jnp.bfloat16), w2_ref[j],
                                     preferred_element_type=jnp.float32).astype(o_ref.dtype)


def _compress_weights(cmp_pe, cmp_w1, cmp_w2):
    bf = jnp.bfloat16
    w1 = jnp.concatenate([cmp_w1[:, :CMP_STRIDE].reshape(2, CMP_HALF, HEAD_DIM),
                          cmp_w1[:, CMP_STRIDE:].reshape(2, CMP_HALF, HEAD_DIM)], axis=-1).astype(bf)
    pe = jnp.broadcast_to(cmp_pe.reshape(2, 1, CMP_LEN * HEAD_DIM), (2, 8, CMP_LEN * HEAD_DIM)).astype(bf)
    return w1, pe, cmp_w2.astype(bf)


def compress_prompt(kvc, cmp_pe, cmp_w1, cmp_w2):
    n, t, _ = kvc.shape
    c = t // CMP_STRIDE
    w1, pe, w2 = _compress_weights(cmp_pe, cmp_w1, cmp_w2)
    return pl.pallas_call(
        _compress_kernel,
        grid=(n,),
        in_specs=[pl.BlockSpec((1, c, CMP_STRIDE * KV_W), lambda b: (b, 0, 0)),
                  pl.BlockSpec(w1.shape, lambda b: (0, 0, 0)),
                  pl.BlockSpec(pe.shape, lambda b: (0, 0, 0)),
                  pl.BlockSpec(w2.shape, lambda b: (0, 0, 0))],
        out_specs=pl.BlockSpec((1, 2, N_KV, c, HEAD_DIM), lambda b: (b, 0, 0, 0, 0)),
        out_shape=jax.ShapeDtypeStruct((n, 2, N_KV, c, HEAD_DIM), jnp.bfloat16),
        compiler_params=_params("parallel"),
        name="compress_prompt",
    )(kvc.reshape(n, c, CMP_STRIDE * KV_W), w1, pe, w2)


def _softmax_rows(s, mask, every_row_valid=False):
    sm = jnp.where(mask, s, NEG)
    e = jnp.exp2((sm - jnp.max(sm, axis=-1, keepdims=True)) * EXP2_SCALE)
    if not every_row_valid:
        e = jnp.where(mask, e, 0.0)
    return e / jnp.maximum(jnp.sum(e, axis=-1, keepdims=True), 1e-30)


def _select_blocks_t(score_t, qpos_l):
    blk = lax.broadcasted_iota(jnp.int32, score_t.shape, 0)
    cur = lax.shift_right_logical(qpos_l, SEL_BLOCK.bit_length() - 1)
    avail = blk <= cur
    forced = avail & ((blk == 0) | (blk == cur) | (blk == cur - 1))
    work = jnp.where(forced, jnp.inf, jnp.where(avail, score_t, -jnp.inf))
    blk_f = blk.astype(jnp.float32)
    sel = jnp.zeros(score_t.shape, jnp.float32)
    for _ in range(SEL_TOPK):
        m = jnp.max(work, axis=0, keepdims=True)
        first = jnp.min(jnp.where(work == m, blk_f, float(score_t.shape[0])), axis=0, keepdims=True)
        hit = blk_f == first
        sel = jnp.where(hit & (m > -jnp.inf), 1.0, sel)
        work = jnp.where(hit, -jnp.inf, work)
    return sel


def _nsa_prompt_kernel(q_ref, gate_ref, ckv_ref, ks_ref, kw_ref, o_ref, m_ref, l_ref, acc_ref, *, tq, tk):
    b = pl.program_id(1)
    q0 = b * tq
    qpos = q0 + lax.broadcasted_iota(jnp.int32, (tq, 1), 0)
    qpos_l = q0 + lax.broadcasted_iota(jnp.int32, (1, tq), 1)
    n_cmp = ckv_ref.shape[3]
    gates = gate_ref[0]
    blocks_per_tile = tk // SEL_BLOCK

    for g in range(N_KV):
        kcol = slice(g * HEAD_DIM, (g + 1) * HEAD_DIM)
        vcol = slice((N_KV + g) * HEAD_DIM, (N_KV + g + 1) * HEAD_DIM)
        qg = jnp.concatenate([q_ref[0, :, (g * HPG + hh) * HEAD_DIM:(g * HPG + hh + 1) * HEAD_DIM]
                              for hh in range(HPG)], axis=0)

        s = lax.dot_general(qg, ckv_ref[0, 0, g], _NT, preferred_element_type=jnp.float32)
        cend = lax.broadcasted_iota(jnp.int32, (1, n_cmp), 1) * CMP_STRIDE + (CMP_LEN - 1)
        vis = cend <= qpos
        ps = [_softmax_rows(s[hh * tq:(hh + 1) * tq], vis) for hh in range(HPG)]
        o_cmp = jnp.dot(jnp.concatenate(ps, axis=0).astype(jnp.bfloat16), ckv_ref[0, 1, g],
                        preferred_element_type=jnp.float32)
        p_grp = ps[0]
        for p in ps[1:]:
            p_grp = p_grp + p

        c0 = lax.broadcasted_iota(jnp.int32, (LANES, n_cmp), 1) * CMP_STRIDE
        s0 = lax.broadcasted_iota(jnp.int32, (LANES, n_cmp), 0) * SEL_BLOCK
        overlap_t = ((c0 < s0 + SEL_BLOCK) & (c0 + CMP_LEN > s0)).astype(jnp.bfloat16)
        score_t = lax.dot_general(overlap_t, p_grp.astype(jnp.bfloat16), _NT, preferred_element_type=jnp.float32)
        sel = _select_blocks_t(score_t, qpos_l).T.astype(jnp.bfloat16)

        m_ref[...] = jnp.full(m_ref.shape, NEG, jnp.float32)
        l_ref[...] = jnp.zeros(l_ref.shape, jnp.float32)
        acc_ref[...] = jnp.zeros(acc_ref.shape, jnp.float32)

        def sel_tile(kt, causal):
            k0 = pl.multiple_of(kt * tk, tk)
            s = lax.dot_general(qg, ks_ref[0, pl.ds(k0, tk), kcol], _NT, preferred_element_type=jnp.float32)
            jrow = lax.broadcasted_iota(jnp.int32, (LANES, tk), 0)
            jcol = lax.shift_right_logical(lax.broadcasted_iota(jnp.int32, (LANES, tk), 1),
                                           SEL_BLOCK.bit_length() - 1)
            expand = (jrow == kt * blocks_per_tile + jcol).astype(jnp.bfloat16)
            mask = jnp.dot(sel, expand, preferred_element_type=jnp.float32) > 0.5
            if causal:
                mask = mask & (k0 + lax.broadcasted_iota(jnp.int32, (1, tk), 1) <= qpos)
            ps = []
            for hh in range(HPG):
                rows = slice(hh * tq, (hh + 1) * tq)
                sh = jnp.where(mask, s[rows], NEG)
                m_old = m_ref[rows]
                m_new = jnp.maximum(m_old, jnp.max(sh, axis=-1, keepdims=True))
                p = jnp.exp2((sh - m_new) * EXP2_SCALE)
                alpha = jnp.exp2((m_old - m_new) * EXP2_SCALE)
                l_ref[rows] = alpha * l_ref[rows] + jnp.sum(p, axis=-1, keepdims=True)
                acc_ref[rows] = alpha * acc_ref[rows]
                m_ref[rows] = m_new
                ps.append(p.astype(jnp.bfloat16))
            acc_ref[...] += jnp.dot(jnp.concatenate(ps, axis=0), ks_ref[0, pl.ds(k0, tk), vcol],
                                    preferred_element_type=jnp.float32)

        n_full = q0 // tk

        def full_tile(kt, carry):
            sel_tile(kt, False)
            return carry

        lax.fori_loop(0, n_full, full_tile, 0)
        sel_tile(n_full, True)
        o_sel = acc_ref[...] / l_ref[...]

        w0 = pl.multiple_of(jnp.maximum(q0 - WINDOW, 0), tq)
        s = lax.dot_general(qg, kw_ref[0, pl.ds(w0, WINDOW + tq), kcol], _NT, preferred_element_type=jnp.float32)
        kpos = w0 + lax.broadcasted_iota(jnp.int32, (1, WINDOW + tq), 1)
        mask = (kpos <= qpos) & (kpos > qpos - WINDOW)
        ps = [_softmax_rows(s[hh * tq:(hh + 1) * tq], mask, every_row_valid=True).astype(jnp.bfloat16)
              for hh in range(HPG)]
        o_win = jnp.dot(jnp.concatenate(ps, axis=0), kw_ref[0, pl.ds(w0, WINDOW + tq), vcol],
                        preferred_element_type=jnp.float32)

        for hh in range(HPG):
            head = g * HPG + hh
            rows = slice(hh * tq, (hh + 1) * tq)
            o = (gates[:, head:head + 1] * o_cmp[rows]
                 + gates[:, N_HEADS + head:N_HEADS + head + 1] * o_sel[rows]
                 + gates[:, 2 * N_HEADS + head:2 * N_HEADS + head + 1] * o_win[rows])
            o_ref[0, :, head * HEAD_DIM:(head + 1) * HEAD_DIM] = o.astype(o_ref.dtype)


def nsa_prompt(q, gates, ckv, kvs, kvw, *, tq=NSA_Q_TILE, tk=NSA_KV_TILE):
    n, t, _ = q.shape
    assert t % tk == 0 and tk % tq == 0 and t >= WINDOW + tq and t // SEL_BLOCK <= LANES
    rows = HPG * tq
    return pl.pallas_call(
        functools.partial(_nsa_prompt_kernel, tq=tq, tk=tk),
        grid=(n, t // tq),
        in_specs=[pl.BlockSpec((1, tq, Q_W), lambda b, i: (b, i, 0)),
                  pl.BlockSpec((1, tq, LANES), lambda b, i: (b, i, 0)),
                  pl.BlockSpec((1,) + ckv.shape[1:], lambda b, i: (b, 0, 0, 0, 0)),
                  pl.BlockSpec((1, t, KV_W), lambda b, i: (b, 0, 0)),
                  pl.BlockSpec((1, t, KV_W), lambda b, i: (b, 0, 0))],
        out_specs=pl.BlockSpec((1, tq, Q_W), lambda b, i: (b, i, 0)),
        out_shape=jax.ShapeDtypeStruct((n, t, Q_W), jnp.bfloat16),
        scratch_shapes=[pltpu.VMEM((rows, 1), jnp.float32), pltpu.VMEM((rows, 1), jnp.float32),
                        pltpu.VMEM((rows, HEAD_DIM), jnp.float32)],
        compiler_params=_params("parallel", "arbitrary"),
        name="nsa_prompt",
    )(q, gates, ckv, kvs, kvw)


def _conv_kernel(a_ref, p_ref, c_ref, w_ref, bias_ref, g_ref, beta_ref, o_ref, win_ref, y_ref, *, tt):
    win_ref[0:CONV_CTX, :] = jnp.where(pl.program_id(1) == 0, c_ref[0], p_ref[0])
    win_ref[CONV_CTX:CONV_CTX + tt, :] = a_ref[0]
    first = CONV_CTX - (CONV_WIDTH - 1)
    rc = min(CONV_ROWS, tt)
    for cc in range(CONV_CH // LANES):
        cols = slice(cc * LANES, (cc + 1) * LANES)
        for r0 in range(0, tt, rc):
            acc = jnp.broadcast_to(bias_ref[:, cols], (rc, LANES))
            for k in range(CONV_WIDTH):
                acc = acc + w_ref[k:k + 1, cols] * win_ref[r0 + first + k:r0 + first + k + rc, cols]
            y_ref[r0:r0 + rc, cols] = acc
    y = y_ref[...]
    yc = y - jnp.mean(y, axis=-1, keepdims=True)
    yn = yc * lax.rsqrt(jnp.mean(yc * yc, axis=-1, keepdims=True) + EPS) * g_ref[...] + beta_ref[...]
    o_ref[0] = (yn * jax.nn.sigmoid(yn)).astype(o_ref.dtype)


def conv_act(ctx, up, w_dw, b_dw, ln_g, ln_b, *, block_t=128):
    n, t, ch = up.shape
    tp = -(-t // CONV_CTX) * CONV_CTX
    tt = _row_block(tp, block_t)
    up_p = up if tp == t else jnp.pad(up, ((0, 0), (0, tp - t), (0, 0)))
    ctx_p = jnp.pad(ctx, ((0, 0), (CONV_CTX - ctx.shape[1], 0), (0, 0)))
    w = jnp.zeros((CONV_CTX, ch), jnp.float32).at[:CONV_WIDTH].set(w_dw)
    step = tt // CONV_CTX
    row = lambda v: v.reshape(1, ch).astype(jnp.float32)
    out = pl.pallas_call(
        functools.partial(_conv_kernel, tt=tt),
        grid=(n, tp // tt),
        in_specs=[pl.BlockSpec((1, tt, ch), lambda b, i: (b, i, 0)),
                  pl.BlockSpec((1, CONV_CTX, ch), lambda b, i: (b, jnp.maximum(i * step - 1, 0), 0)),
                  pl.BlockSpec((1, CONV_CTX, ch), lambda b, i: (b, 0, 0)),
                  pl.BlockSpec((CONV_CTX, ch), lambda b, i: (0, 0)),
                  pl.BlockSpec((1, ch), lambda b, i: (0, 0)),
                  pl.BlockSpec((1, ch), lambda b, i: (0, 0)),
                  pl.BlockSpec((1, ch), lambda b, i: (0, 0))],
        out_specs=pl.BlockSpec((1, tt, ch), lambda b, i: (b, i, 0)),
        out_shape=jax.ShapeDtypeStruct((n, tp, ch), jnp.bfloat16),
        scratch_shapes=[pltpu.VMEM((tt + CONV_CTX, ch), jnp.float32), pltpu.VMEM((tt, ch), jnp.float32)],
        compiler_params=_params("parallel", "arbitrary"),
        name="conv_act",
    )(up_p, up_p, ctx_p, w, row(b_dw), row(ln_g), row(ln_b))
    return out[:, :t]


def _cross_attn_kernel(q_ref, kv_ref, o_ref):
    outs = []
    for h in range(MEM_HEADS):
        lo = h * MEM_HEAD_DIM
        q = q_ref[0, :, lo:lo + MEM_HEAD_DIM]
        k = kv_ref[0, :, lo:lo + MEM_HEAD_DIM].astype(jnp.bfloat16)
        v = kv_ref[0, :, MEM_W + lo:MEM_W + lo + MEM_HEAD_DIM].astype(jnp.bfloat16)
        s = lax.dot_general(q, k, _NT, preferred_element_type=jnp.float32) * MEM_SCALE
        e = jnp.exp(s - jnp.max(s, axis=-1, keepdims=True))
        p = e / jnp.sum(e, axis=-1, keepdims=True)
        outs.append(jnp.dot(p.astype(jnp.bfloat16), v, preferred_element_type=jnp.float32))
    o_ref[0] = jnp.concatenate(outs, axis=-1).astype(o_ref.dtype)


def cross_attn_core(q, kv, *, block_q=512):
    n, t, _ = q.shape
    bq = _row_block(t, block_q)
    return pl.pallas_call(
        _cross_attn_kernel,
        grid=(n, t // bq),
        in_specs=[pl.BlockSpec((1, bq, MEM_W), lambda b, i: (b, i, 0)),
                  pl.BlockSpec((1, MEM_LEN, 2 * MEM_W), lambda b, i: (b, 0, 0))],
        out_specs=pl.BlockSpec((1, bq, MEM_W), lambda b, i: (b, i, 0)),
        out_shape=jax.ShapeDtypeStruct((n, t, MEM_W), jnp.bfloat16),
        compiler_params=_params("parallel", "arbitrary"),
        name="cross_attn_core",
    )(q, kv)


def _router_kernel(x_ref, g_ref, w_ref, b_ref, h_ref, e_ref, p_ref):
    x = x_ref[...]
    h = x * lax.rsqrt(jnp.mean(x * x, axis=-1, keepdims=True) + EPS) * g_ref[...]
    h_ref[...] = h
    logits = jnp.dot(h.astype(jnp.bfloat16), w_ref[...], preferred_element_type=jnp.float32) + b_ref[...]
    lt = logits.T[:N_EXPERTS]
    ids = lax.broadcasted_iota(jnp.int32, lt.shape, 0).astype(jnp.float32)
    vals, idxs = [], []
    for _ in range(TOP_K):
        m = jnp.max(lt, axis=0, keepdims=True)
        idx = jnp.min(jnp.where(lt == m, ids, float(N_EXPERTS)), axis=0, keepdims=True)
        vals.append(m)
        idxs.append(idx)
        lt = jnp.where(ids == idx, -jnp.inf, lt)
    top_v = jnp.concatenate(vals, axis=0)
    e = jnp.exp(top_v - vals[0])
    e_ref[...] = jnp.concatenate(idxs, axis=0).astype(jnp.int32)
    p_ref[...] = e / jnp.sum(e, axis=0, keepdims=True)


def moe_router(x, gain, w_router, b_router, *, block_m=512):
    m, d = x.shape
    bm = _row_block(m, block_m)
    w = jnp.zeros((d, LANES), jnp.bfloat16).at[:, :N_EXPERTS].set(w_router.astype(jnp.bfloat16))
    b = jnp.zeros((1, LANES), jnp.float32).at[0, :N_EXPERTS].set(b_router.astype(jnp.float32))
    return pl.pallas_call(
        _router_kernel,
        grid=(m // bm,),
        in_specs=[pl.BlockSpec((bm, d), lambda i: (i, 0)),
                  pl.BlockSpec((1, d), lambda i: (0, 0)),
                  pl.BlockSpec((d, LANES), lambda i: (0, 0)),
                  pl.BlockSpec((1, LANES), lambda i: (0, 0))],
        out_specs=[pl.BlockSpec((bm, d), lambda i: (i, 0)),
                   pl.BlockSpec((TOP_K, bm), lambda i: (0, i)),
                   pl.BlockSpec((TOP_K, bm), lambda i: (0, i))],
        out_shape=[jax.ShapeDtypeStruct((m, d), jnp.float32),
                   jax.ShapeDtypeStruct((TOP_K, m), jnp.int32),
                   jax.ShapeDtypeStruct((TOP_K, m), jnp.float32)],
        compiler_params=_params("parallel"),
        name="moe_router",
    )(x, gain.reshape(1, d).astype(jnp.float32), w, b)


def _expert_kernel(blk_e_ref, blk_rows_ref, tok_cur_ref, tok_nxt_ref, h_hbm, wg_ref, wu_ref, bg_ref, bu_ref,
                   wd_ref, bd_ref, o_ref, x_stage, x_cur, sem, *, nf):
    i, j = pl.program_id(0), pl.program_id(1)
    rows = blk_rows_ref[i]
    share = MOE_BLOCK_ROWS // nf

    def row_copy(tok, r):
        return pltpu.make_async_copy(h_hbm.at[pl.ds(tok, 1)], x_stage.at[pl.ds(r, 1)], sem.at[0])

    def for_all_rows(fn):
        def body(r, carry):
            fn(r)
            return carry
        lax.fori_loop(0, MOE_BLOCK_ROWS, body, 0, unroll=8)

    @pl.when(j == 0)
    def _():
        o_ref[...] = jnp.broadcast_to(bd_ref[0], o_ref.shape)

        @pl.when(i == 0)
        def _():
            for_all_rows(lambda r: row_copy(tok_cur_ref[0, 0, r], r).start())

        @pl.when((i == 0) | (blk_rows_ref[jnp.maximum(i - 1, 0)] > 0))
        def _():
            for_all_rows(lambda r: row_copy(0, r).wait())

        @pl.when(rows > 0)
        def _():
            x_cur[...] = x_stage[...].astype(jnp.bfloat16)

    @pl.when(rows > 0)
    def _():
        for r in range(share):
            row_copy(tok_nxt_ref[0, 0, j * share + r], j * share + r).start(priority=r % 2)
        x = x_cur[...]
        a = jnp.dot(x, wg_ref[0].astype(jnp.bfloat16), preferred_element_type=jnp.float32) + bg_ref[0]
        u = jnp.dot(x, wu_ref[0].astype(jnp.bfloat16), preferred_element_type=jnp.float32) + bu_ref[0]
        a = jnp.minimum(a, SWIGLU_LIMIT)
        u = jnp.clip(u, -SWIGLU_LIMIT, SWIGLU_LIMIT)
        act = a * jax.nn.sigmoid(SWIGLU_ALPHA * a) * (u + 1.0)
        o_ref[...] += jnp.dot(act.astype(jnp.bfloat16), wd_ref[0].astype(jnp.bfloat16),
                              preferred_element_type=jnp.float32)


def moe_experts(h, slot_tok, blk_e, blk_rows, w_gu, b_gu, w_down, b_down):
    s_total = slot_tok.shape[0]
    d = w_gu.shape[1]
    n_e, _, two_ff = w_gu.shape
    ff = two_ff // 2
    nb = s_total // MOE_BLOCK_ROWS
    nf = ff // MOE_FF_TILE
    tok3 = slot_tok.reshape(nb, 1, MOE_BLOCK_ROWS)

    def col(i, j, rows_ref):
        return jnp.where(rows_ref[i] > 0, j, nf - 1)

    grid_spec = pltpu.PrefetchScalarGridSpec(
        num_scalar_prefetch=2,
        grid=(nb, nf),
        in_specs=[
            pl.BlockSpec((1, 1, MOE_BLOCK_ROWS), lambda i, j, e, r: (i, 0, 0), memory_space=pltpu.SMEM),
            pl.BlockSpec((1, 1, MOE_BLOCK_ROWS), lambda i, j, e, r: (jnp.minimum(i + 1, nb - 1), 0, 0),
                         memory_space=pltpu.SMEM),
            pl.BlockSpec(memory_space=pl.ANY),
            pl.BlockSpec((1, d, MOE_FF_TILE), lambda i, j, e, r: (e[i], 0, col(i, j, r))),
            pl.BlockSpec((1, d, MOE_FF_TILE), lambda i, j, e, r: (e[i], 0, nf + col(i, j, r))),
            pl.BlockSpec((1, 1, MOE_FF_TILE), lambda i, j, e, r: (e[i], 0, col(i, j, r))),
            pl.BlockSpec((1, 1, MOE_FF_TILE), lambda i, j, e, r: (e[i], 0, nf + col(i, j, r))),
            pl.BlockSpec((1, MOE_FF_TILE, d), lambda i, j, e, r: (e[i], col(i, j, r), 0)),
            pl.BlockSpec((1, 1, d), lambda i, j, e, r: (e[i], 0, 0)),
        ],
        out_specs=pl.BlockSpec((MOE_BLOCK_ROWS, d), lambda i, j, e, r: (i, 0)),
        scratch_shapes=[pltpu.VMEM((MOE_BLOCK_ROWS, d), jnp.float32),
                        pltpu.VMEM((MOE_BLOCK_ROWS, d), jnp.bfloat16),
                        pltpu.SemaphoreType.DMA((1,))],
    )
    b_gu3 = b_gu.reshape(n_e, 1, two_ff)
    return pl.pallas_call(
        functools.partial(_expert_kernel, nf=nf),
        grid_spec=grid_spec,
        out_shape=jax.ShapeDtypeStruct((s_total, d), jnp.float32),
        compiler_params=_params("arbitrary", "arbitrary"),
        name="moe_experts",
    )(blk_e, blk_rows, tok3, tok3, h, w_gu, w_gu, b_gu3, b_gu3, w_down, b_down.reshape(n_e, 1, d))


def moe_layer(x_groups, gain, w_router, b_router, w_gu, b_gu, w_down, b_down, g_final):
    routed = [moe_router(x, gain, w_router, b_router) for x in x_groups]
    h = jnp.concatenate([r[0] for r in routed], axis=0)
    top_e = jnp.concatenate([r[1] for r in routed], axis=1)
    top_p = jnp.concatenate([r[2] for r in routed], axis=1)
    t = top_e.shape[1]
    bm = MOE_BLOCK_ROWS
    n_blocks = (t * TOP_K) // bm + N_EXPERTS + 1
    onehot = (top_e[:, :, None] == jnp.arange(N_EXPERTS)[None, None, :]).astype(jnp.int32)
    per_tok = jnp.sum(onehot, axis=0)
    before = jnp.cumsum(per_tok, axis=0) - per_tok
    counts = jnp.sum(per_tok, axis=0)
    blocks_e = (counts + bm - 1) // bm
    blk_end = jnp.cumsum(blocks_e)
    blk_start = blk_end - blocks_e
    first_slot = before + (blk_start * bm)[None, :]
    slot = jnp.sum(onehot * first_slot[None], axis=-1).astype(jnp.int32)
    tok_ids = jnp.broadcast_to(jnp.arange(t, dtype=jnp.int32)[None], slot.shape)
    slot_tok = jnp.zeros((n_blocks * bm,), jnp.int32).at[slot.reshape(-1)].set(tok_ids.reshape(-1))
    blk = jnp.arange(n_blocks)
    blk_e = jnp.minimum(jnp.searchsorted(blk_end, blk, side='right'), N_EXPERTS - 1).astype(jnp.int32)
    used = blk < blk_end[-1]
    last_e = blk_e[jnp.maximum(blk_end[-1] - 1, 0)]
    blk_e = jnp.where(used, blk_e, last_e).astype(jnp.int32)
    blk_rows = jnp.where(used, jnp.clip(counts[blk_e] - (blk - blk_start[blk_e]) * bm, 0, bm), 0).astype(jnp.int32)
    outs = moe_experts(h, slot_tok, blk_e, blk_rows, w_gu, b_gu, w_down, b_down)
    ys, start = [], 0
    for x in x_groups:
        m = x.shape[0]
        ys.append(moe_combine_norm(x, outs, slot[:, start:start + m], top_p[:, start:start + m], g_final))
        start += m
    return ys


def _combine_kernel(idx_cur_ref, idx_nxt_ref, x_ref, p_ref, g_ref, outs_hbm, o_ref, buf, sem, *, bm):
    i = pl.program_id(0)
    cur = lax.rem(i, 2)
    n_rows = TOP_K * bm

    def row_copy(slot, dst, r):
        return pltpu.make_async_copy(outs_hbm.at[pl.ds(slot, 1)], buf.at[dst, pl.ds(r, 1)], sem.at[dst])

    def for_rows(fn):
        for r in range(n_rows):
            fn(r, r)

    @pl.when(i == 0)
    def _():
        for_rows(lambda r, k: row_copy(idx_cur_ref[0, 0, r], cur, r).start(priority=k % 2))

    for_rows(lambda r, k: row_copy(0, cur, r).wait())

    @pl.when(i + 1 < pl.num_programs(0))
    def _():
        for_rows(lambda r, k: row_copy(idx_nxt_ref[0, 0, r], 1 - cur, r).start(priority=k % 2))

    x = x_ref[...]
    for k in range(TOP_K):
        x = x + p_ref[:, k:k + 1] * buf[cur, k * bm:(k + 1) * bm, :]
    o_ref[...] = x * lax.rsqrt(jnp.mean(x * x, axis=-1, keepdims=True) + EPS) * g_ref[...]


def moe_combine_norm(x, outs, slot, gate, gain, *, block_m=128):
    m, d = x.shape
    bm = _row_block(m, block_m)
    nt = m // bm
    idx = slot.reshape(TOP_K, nt, bm).transpose(1, 0, 2).reshape(nt, 1, TOP_K * bm)
    return pl.pallas_call(
        functools.partial(_combine_kernel, bm=bm),
        grid=(nt,),
        in_specs=[pl.BlockSpec((1, 1, TOP_K * bm), lambda i: (i, 0, 0), memory_space=pltpu.SMEM),
                  pl.BlockSpec((1, 1, TOP_K * bm), lambda i: (jnp.minimum(i + 1, nt - 1), 0, 0),
                               memory_space=pltpu.SMEM),
                  pl.BlockSpec((bm, d), lambda i: (i, 0)),
                  pl.BlockSpec((bm, TOP_K), lambda i: (i, 0)),
                  pl.BlockSpec((1, d), lambda i: (0, 0)),
                  pl.BlockSpec(memory_space=pl.ANY)],
        out_specs=pl.BlockSpec((bm, d), lambda i: (i, 0)),
        out_shape=jax.ShapeDtypeStruct((m, d), jnp.float32),
        scratch_shapes=[pltpu.VMEM((2, TOP_K * bm, d), jnp.float32), pltpu.SemaphoreType.DMA((2,))],
        compiler_params=_params("arbitrary"),
        name="moe_combine_norm",
    )(idx, idx, x, gate.T, gain.reshape(1, d).astype(jnp.float32), outs)


def _hist_compress_kernel(pt_ref, *refs):
    pages, (w1_ref, pe_ref, w2_ref, o_ref, slab_ref) = refs[:CMP_PAGES + 1], refs[CMP_PAGES + 1:]
    c = CMP_PAGES * CHUNKS_PER_PAGE
    last = lax.broadcasted_iota(jnp.int32, (c, 1), 0) == c - 1

    slab_rows = CMP_STRIDE * KV_ROWS
    regs_per_chunk = slab_rows // 8
    for k in range(CMP_PAGES + 1):
        chunks = CHUNKS_PER_PAGE if k < CMP_PAGES else 1
        if k == CMP_PAGES:
            for s in range(slab_rows):
                slab_ref[s * SLAB_PITCH + c:s * SLAB_PITCH + c + 8, :] = jnp.zeros((8, HEAD_DIM), jnp.float32)
        for cc in range(chunks):
            for sg in range(regs_per_chunk):
                v = cc * regs_per_chunk + sg
                slab_ref[pl.ds(8 * sg * SLAB_PITCH + k * CHUNKS_PER_PAGE + cc, 8, stride=SLAB_PITCH), :] = (
                    pages[k][v * 8:(v + 1) * 8, :])

    def rows(sub, lo, n):
        return jnp.concatenate([slab_ref[(i * KV_ROWS + sub) * SLAB_PITCH + lo:(i * KV_ROWS + sub) * SLAB_PITCH + lo + n, :]
                                for i in range(CMP_STRIDE)], axis=1)

    for j in range(2):
        w_a = w1_ref[j, :, :HEAD_DIM]
        w_b = w1_ref[j, :, HEAD_DIM:]
        pe = pe_ref[j]
        bias = (jnp.dot(pe[:, :CMP_HALF], w_a, preferred_element_type=jnp.float32)
                + jnp.dot(pe[:, CMP_HALF:], w_b, preferred_element_type=jnp.float32))[0:1]
        bases = [j * N_KV + g for g in range(N_KV)]
        x = jnp.concatenate([rows(base, 0, c) for base in bases], axis=0).astype(jnp.bfloat16)
        x_next = jnp.concatenate([rows(base, c, CHUNKS_PER_PAGE) for base in bases], axis=0).astype(jnp.bfloat16)
        ab = jnp.dot(x, w1_ref[j], preferred_element_type=jnp.float32)
        a, b = ab[:, :HEAD_DIM], ab[:, HEAD_DIM:]
        b_tail = jnp.dot(x_next, w_b, preferred_element_type=jnp.float32)
        for g in range(N_KV):
            b_next = jnp.where(last, b_tail[g * CHUNKS_PER_PAGE:g * CHUNKS_PER_PAGE + 1],
                               pltpu.roll(b[g * c:(g + 1) * c], c - 1, axis=0))
            h = jax.nn.gelu(a[g * c:(g + 1) * c] + b_next + bias)
            o_ref[0, j, g] = jnp.dot(h.astype(jnp.bfloat16), w2_ref[j],
                                     preferred_element_type=jnp.float32).astype(o_ref.dtype)


def compress_history(pool, page_table, cmp_pe, cmp_w1, cmp_w2):
    n, n_pages = page_table.shape
    assert n_pages % CMP_PAGES == 0
    c = n_pages * CHUNKS_PER_PAGE
    w1, pe, w2 = _compress_weights(cmp_pe, cmp_w1, cmp_w2)

    def page_spec(k):
        return pl.BlockSpec((PAGE_SIZE * KV_ROWS, HEAD_DIM),
                            lambda b, s, pt: (pt[b * n_pages + jnp.minimum(s * CMP_PAGES + k, n_pages - 1)], 0))

    grid_spec = pltpu.PrefetchScalarGridSpec(
        num_scalar_prefetch=1,
        grid=(n, n_pages // CMP_PAGES),
        in_specs=[page_spec(k) for k in range(CMP_PAGES + 1)] + [
            pl.BlockSpec(w1.shape, lambda b, s, pt: (0, 0, 0)),
            pl.BlockSpec(pe.shape, lambda b, s, pt: (0, 0, 0)),
            pl.BlockSpec(w2.shape, lambda b, s, pt: (0, 0, 0))],
        out_specs=pl.BlockSpec((1, 2, N_KV, CMP_PAGES * CHUNKS_PER_PAGE, HEAD_DIM), lambda b, s, pt: (b, 0, 0, s, 0)),
        scratch_shapes=[pltpu.VMEM((CMP_STRIDE * KV_ROWS * SLAB_PITCH, HEAD_DIM), jnp.float32)],
    )
    return pl.pallas_call(
        _hist_compress_kernel,
        grid_spec=grid_spec,
        out_shape=jax.ShapeDtypeStruct((n, 2, N_KV, c, HEAD_DIM), jnp.bfloat16),
        compiler_params=_params("parallel", "arbitrary"),
        name="compress_history",
    )(page_table.reshape(-1), *([pool] * (CMP_PAGES + 1)), w1, pe, w2)


def _nsa_sample_dense_kernel(q_ref, ckv_ref, win_ref, new_ref, ocmp_ref, owin_ref, idx_ref, val_ref, *,
                             past, n_cmp, n_sel):
    rows = HPG * DEC_PAD
    tok = lax.bitwise_and(lax.broadcasted_iota(jnp.int32, (rows, 1), 0), DEC_PAD - 1)
    qpos = past + tok
    qpos_l = past + lax.broadcasted_iota(jnp.int32, (1, LANES), 1)
    c_all = ckv_ref.shape[3]
    n_sel_pad = -(-n_sel // 8) * 8
    wlen = WINDOW + LANES

    def window_rows(sub):
        return jnp.concatenate([win_ref[pl.ds(sub, WINDOW, stride=KV_ROWS), :],
                                new_ref[0, pl.ds(sub, DEC_PAD, stride=KV_ROWS), :],
                                jnp.zeros((LANES - DEC_PAD, HEAD_DIM), jnp.float32)], axis=0).astype(jnp.bfloat16)

    for g in range(N_KV):
        qg = q_ref[0, g]
        s = lax.dot_general(qg, ckv_ref[0, 0, g], _NT, preferred_element_type=jnp.float32)
        cidx = lax.broadcasted_iota(jnp.int32, (1, c_all), 1)
        vis = (cidx * CMP_STRIDE + (CMP_LEN - 1) <= qpos) & (cidx < n_cmp)
        p = _softmax_rows(s, vis)
        ocmp_ref[0, g] = jnp.dot(p.astype(jnp.bfloat16), ckv_ref[0, 1, g], preferred_element_type=jnp.float32)
        p_grp = p[0:DEC_PAD]
        for hh in range(1, HPG):
            p_grp = p_grp + p[hh * DEC_PAD:(hh + 1) * DEC_PAD]
        p_grp = jnp.concatenate([p_grp, jnp.zeros((LANES - DEC_PAD, c_all), jnp.float32)], axis=0)
        c0 = lax.broadcasted_iota(jnp.int32, (n_sel_pad, c_all), 1) * CMP_STRIDE
        s0 = lax.broadcasted_iota(jnp.int32, (n_sel_pad, c_all), 0) * SEL_BLOCK
        overlap_t = ((c0 < s0 + SEL_BLOCK) & (c0 + CMP_LEN > s0)).astype(jnp.bfloat16)
        score_t = lax.dot_general(overlap_t, p_grp.astype(jnp.bfloat16), _NT, preferred_element_type=jnp.float32)
        blk = lax.broadcasted_iota(jnp.int32, score_t.shape, 0)
        cur = lax.shift_right_logical(qpos_l, SEL_BLOCK.bit_length() - 1)
        avail = (blk <= cur) & (blk < n_sel)
        forced = avail & ((blk == 0) | (blk == cur) | (blk == cur - 1))
        work = jnp.where(forced, jnp.inf, jnp.where(avail, score_t, -jnp.inf))
        blk_f = blk.astype(jnp.float32)
        idxs, vals = [], []
        for _ in range(SEL_TOPK):
            m = jnp.max(work, axis=0, keepdims=True)
            first = jnp.min(jnp.where(work == m, blk_f, float(n_sel_pad)), axis=0, keepdims=True)
            idxs.append(first)
            vals.append(jnp.where(m > -jnp.inf, 1.0, 0.0))
            work = jnp.where(blk_f == first, -jnp.inf, work)
        idx_ref[0, g] = jnp.concatenate(idxs, axis=0).astype(jnp.int32)
        val_ref[0, g] = jnp.concatenate(vals, axis=0).astype(jnp.int32)

        s = lax.dot_general(qg, window_rows(g), _NT, preferred_element_type=jnp.float32)
        kpos = past - WINDOW + lax.broadcasted_iota(jnp.int32, (1, wlen), 1)
        p = _softmax_rows(s, (kpos <= qpos) & (kpos > qpos - WINDOW))
        owin_ref[0, g] = jnp.dot(p.astype(jnp.bfloat16), window_rows(N_KV + g), preferred_element_type=jnp.float32)


def nsa_sample_dense(qd, ckv, win_rows, new_rows, *, past, n_cmp, n_sel):
    n = qd.shape[0]
    rows = HPG * DEC_PAD
    o_shape = jax.ShapeDtypeStruct((n, N_KV, rows, HEAD_DIM), jnp.float32)
    i_shape = jax.ShapeDtypeStruct((n, N_KV, SEL_TOPK, LANES), jnp.int32)
    o_spec = pl.BlockSpec((1, N_KV, rows, HEAD_DIM), lambda b: (b, 0, 0, 0))
    i_spec = pl.BlockSpec((1, N_KV, SEL_TOPK, LANES), lambda b: (b, 0, 0, 0))
    return pl.pallas_call(
        functools.partial(_nsa_sample_dense_kernel, past=past, n_cmp=n_cmp, n_sel=n_sel),
        grid=(n,),
        in_specs=[pl.BlockSpec((1, N_KV, rows, HEAD_DIM), lambda b: (b, 0, 0, 0)),
                  pl.BlockSpec((1,) + ckv.shape[1:], lambda b: (b, 0, 0, 0, 0)),
                  pl.BlockSpec((WINDOW * KV_ROWS, HEAD_DIM), lambda b: (b, 0)),
                  pl.BlockSpec((1,) + new_rows.shape[1:], lambda b: (b, 0, 0))],
        out_specs=[o_spec, o_spec, i_spec, i_spec],
        out_shape=[o_shape, o_shape, i_shape, i_shape],
        compiler_params=_params("parallel"),
        name="nsa_sample_dense",
    )(qd, ckv, win_rows, new_rows)


def _nsa_sample_sel_kernel(idx_ref, val_ref, pt_ref, q_ref, new_ref, *refs, tq, past, n_past_blk):
    blocks, o_ref = refs[:N_KV * SEL_TOPK], refs[N_KV * SEL_TOPK]
    b, t = pl.program_id(0), pl.program_id(1)
    qpos = past + t
    lane = lax.broadcasted_iota(jnp.int32, (1, SEL_TOPK * SEL_BLOCK), 1)
    slot_of_lane = lax.shift_right_logical(lane, SEL_BLOCK.bit_length() - 1)
    for g in range(N_KV):
        base = ((b * tq + t) * N_KV + g) * SEL_TOPK
        ks, vs = [], []
        tokpos = jnp.zeros(lane.shape, jnp.int32)
        ok = jnp.zeros(lane.shape, jnp.int32)
        for s in range(SEL_TOPK):
            blk = idx_ref[base + s]
            from_pool = blk < n_past_blk
            pool_blk = blocks[g * SEL_TOPK + s]
            k_rows = pl.ds(g, SEL_BLOCK, stride=KV_ROWS)
            v_rows = pl.ds(N_KV + g, SEL_BLOCK, stride=KV_ROWS)
            ks.append(jnp.where(from_pool, pool_blk[k_rows, :], new_ref[0, k_rows, :]))
            vs.append(jnp.where(from_pool, pool_blk[v_rows, :], new_ref[0, v_rows, :]))
            here = slot_of_lane == s
            tokpos = jnp.where(here, blk * SEL_BLOCK, tokpos)
            ok = jnp.where(here, val_ref[base + s], ok)
        tokpos = tokpos + lax.bitwise_and(lane, SEL_BLOCK - 1)
        mask = (ok > 0) & (tokpos <= qpos)
        k = jnp.concatenate(ks, axis=0).astype(jnp.bfloat16)
        v = jnp.concatenate(vs, axis=0).astype(jnp.bfloat16)
        s = lax.dot_general(q_ref[0, 0, g], k, _NT, preferred_element_type=jnp.float32)
        p = _softmax_rows(s, mask)
        o_ref[0, 0, g] = jnp.dot(p.astype(jnp.bfloat16), v, preferred_element_type=jnp.float32)


def nsa_sample_sel(qs, idx, valid, page_table, pool, new_blocks, *, past):
    n, tq = qs.shape[:2]
    n_pages = page_table.shape[1]
    bpp = PAGE_SIZE // SEL_BLOCK
    n_past_blk = n_pages * bpp

    def block_spec(g, s):
        def index(b, t, idx_ref, val_ref, pt_ref):
            blk = jnp.minimum(idx_ref[((b * tq + t) * N_KV + g) * SEL_TOPK + s], n_past_blk - 1)
            return (pt_ref[b * n_pages + blk // bpp] * bpp + blk % bpp, 0)
        return pl.BlockSpec((SEL_BLOCK * KV_ROWS, HEAD_DIM), index)

    grid_spec = pltpu.PrefetchScalarGridSpec(
        num_scalar_prefetch=3,
        grid=(n, tq),
        in_specs=[pl.BlockSpec((1, 1, N_KV, 16, HEAD_DIM), lambda b, t, *_: (b, t, 0, 0, 0)),
                  pl.BlockSpec((1, SEL_BLOCK * KV_ROWS, HEAD_DIM), lambda b, t, *_: (b, 0, 0))]
                 + [block_spec(g, s) for g in range(N_KV) for s in range(SEL_TOPK)],
        out_specs=pl.BlockSpec((1, 1, N_KV, 16, HEAD_DIM), lambda b, t, *_: (b, t, 0, 0, 0)),
    )
    return pl.pallas_call(
        functools.partial(_nsa_sample_sel_kernel, tq=tq, past=past, n_past_blk=n_past_blk),
        grid_spec=grid_spec,
        out_shape=jax.ShapeDtypeStruct((n, tq, N_KV, 16, HEAD_DIM), jnp.float32),
        compiler_params=_params("parallel", "arbitrary"),
        name="nsa_sample_sel",
    )(idx.reshape(-1), valid.reshape(-1), page_table.reshape(-1), qs, new_blocks,
      *([pool] * (N_KV * SEL_TOPK)))


def nsa_sample(q, gates, kvs_new, kvw_new, pool_cmp, pool_sel, win_cache, page_table, cmp_pe, cmp_w1, cmp_w2):
    n, tq, _ = q.shape
    past = page_table.shape[1] * PAGE_SIZE
    assert tq <= DEC_PAD and tq <= SEL_BLOCK and win_cache.shape == (n * WINDOW * KV_ROWS, HEAD_DIM)
    assert (past + tq) // CMP_STRIDE * CMP_STRIDE == past
    n_cmp = past // CMP_STRIDE - 1
    n_sel = -(-(past + tq) // SEL_BLOCK)
    ckv = compress_history(pool_cmp, page_table, cmp_pe, cmp_w1, cmp_w2)
    q5 = q.reshape(n, tq, N_KV, HPG, HEAD_DIM)
    qd = jnp.pad(q5.transpose(0, 2, 3, 1, 4), ((0, 0), (0, 0), (0, 0), (0, DEC_PAD - tq), (0, 0)))
    qd = qd.reshape(n, N_KV, HPG * DEC_PAD, HEAD_DIM)
    new_win = jnp.pad(kvw_new, ((0, 0), (0, DEC_PAD - tq), (0, 0))).reshape(n, DEC_PAD * KV_ROWS, HEAD_DIM)
    o_cmp, o_win, idx, valid = nsa_sample_dense(qd, ckv, win_cache, new_win, past=past, n_cmp=n_cmp, n_sel=n_sel)
    idx = idx[..., :tq].transpose(0, 3, 1, 2)
    valid = valid[..., :tq].transpose(0, 3, 1, 2)
    qs = jnp.pad(q5, ((0, 0), (0, 0), (0, 0), (0, 16 - HPG), (0, 0)))
    new_blocks = jnp.pad(kvs_new, ((0, 0), (0, SEL_BLOCK - tq), (0, 0))).reshape(n, SEL_BLOCK * KV_ROWS, HEAD_DIM)
    o_sel = nsa_sample_sel(qs, idx, valid, page_table, pool_sel, new_blocks, past=past)[:, :, :, :HPG]
    unpack = lambda o: o.reshape(n, N_KV, HPG, DEC_PAD, HEAD_DIM)[:, :, :, :tq].transpose(0, 3, 1, 2, 4)
    g = gates[..., :GATE_W].reshape(n, tq, 3, N_KV, HPG)[..., None]
    o = g[:, :, 0] * unpack(o_cmp) + g[:, :, 1] * o_sel + g[:, :, 2] * unpack(o_win)
    return o.reshape(n, tq, Q_W)


def prepare_weights(w_in, w_nsa_o, w_conv_o, w_out, w_xq, w_xkv, w_xo):
    bf = jnp.bfloat16
    w = w_in.astype(bf)
    d = w.shape[0]
    s = IN_SPLITS
    return dict(
        q=w[:, :s[0]], kv=w[:, s[0]:s[3]],
        gates=jnp.zeros((d, LANES), bf).at[:, :GATE_W].set(w[:, s[3]:s[4]]),
        glu_a=w[:, s[4]:s[4] + CONV_CH], glu_b=w[:, s[4] + CONV_CH:s[5]], merge=w[:, s[5]:],
        nsa_o=w_nsa_o.astype(bf), conv_o=w_conv_o.astype(bf), out=w_out.astype(bf),
        xq=w_xq.astype(bf), xkv=w_xkv.astype(bf), xo=w_xo.astype(bf))


def front(x2, g_mix, wts):
    q = fused_linear(x2, [wts['q']], gain=g_mix, out_dtype=jnp.bfloat16, name="front_q")
    kv = fused_linear(x2, [wts['kv']], gain=g_mix, name="front_kv")
    gates = fused_linear(x2, [wts['gates']], gain=g_mix, epilogue=_sigmoid_epilogue, name="front_gates")
    up = fused_linear(x2, [wts['glu_a'], wts['glu_b']], gain=g_mix, epilogue=_glu_epilogue, name="front_glu")
    mg = fused_linear(x2, [wts['merge']], gain=g_mix, epilogue=_sigmoid_epilogue, name="front_merge")
    return q, kv, gates, up, mg


def mixer_tail(x2, o, act, mg, wts):
    merged = merge_branches(o, act, wts['nsa_o'], wts['conv_o'], mg)
    return fused_linear(merged, [wts['out']], tiles=[x2], epilogue=_residual_epilogue, name="mixer_out_proj")


def cross_attn_block(x2, n, mkv, g_xattn, wts):
    m = x2.shape[0]
    qx = fused_linear(x2, [wts['xq']], gain=g_xattn, out_dtype=jnp.bfloat16, name="xattn_q")
    oc = cross_attn_core(qx.reshape(n, m // n, MEM_W), mkv)
    return fused_linear(oc.reshape(m, MEM_W), [wts['xo']], tiles=[x2], epilogue=_residual_epilogue, name="xattn_o")


def kernel(x_prompt, x_sample, cache_kv_cmp, cache_kv_sel, cache_kv_win, cache_conv, cache_mem_kv, page_table,
           mem_prompt, g_mix, w_in, cmp_pe, cmp_w1, cmp_w2, w_nsa_o, w_dw, b_dw, ln_conv_g, ln_conv_b, w_conv_o,
           w_out, g_xattn, g_mem, w_xq, w_xkv, w_xo, g_moe, w_router, b_router, w_gu, b_gu, w_down, b_down, g_final):
    nb, t, d = x_prompt.shape
    nd, tq = x_sample.shape[:2]
    l = 0
    bf = jnp.bfloat16
    wts = prepare_weights(w_in[l], w_nsa_o[l], w_conv_o[l], w_out[l], w_xq[l], w_xkv[l], w_xo[l])
    kv_shape = lambda n_, t_: (n_, t_, 2, N_KV, HEAD_DIM)

    xp2 = x_prompt.reshape(nb * t, d)
    q, kv, gates, up_new, mg = front(xp2, g_mix[l], wts)
    kvc, kvs, kvw = (kv[:, i * KV_W:(i + 1) * KV_W].reshape(nb, t, KV_W) for i in range(3))
    ckv = compress_prompt(kvc, cmp_pe[l], cmp_w1[l], cmp_w2[l])
    o = nsa_prompt(q.reshape(nb, t, Q_W), gates.reshape(nb, t, LANES), ckv, kvs.astype(bf), kvw.astype(bf))
    up_new = up_new.reshape(nb, t, CONV_CH)
    act = conv_act(jnp.zeros((nb, CONV_WIDTH - 1, CONV_CH), jnp.float32), up_new, w_dw[l], b_dw[l],
                   ln_conv_g[l], ln_conv_b[l])
    xp = mixer_tail(xp2, o.reshape(nb * t, Q_W), act.reshape(nb * t, CONV_CH), mg, wts)
    mkv = fused_linear(mem_prompt.reshape(nb * MEM_LEN, d), [wts['xkv']], gain=g_mem[l], name="mem_kv")
    xp = cross_attn_block(xp, nb, mkv.reshape(nb, MEM_LEN, 2 * MEM_W), g_xattn[l], wts)
    p_kv_cmp, p_kv_sel = kvc.reshape(kv_shape(nb, t))[None], kvs.reshape(kv_shape(nb, t))[None]
    p_kv_win = kvw.reshape(kv_shape(nb, t))[:, t - min(WINDOW, t):][None]
    p_conv = jnp.pad(up_new, ((0, 0), (CONV_WIDTH - 1, 0), (0, 0)))[:, t:][None]
    p_mem = mkv.reshape(nb, MEM_LEN, 2, MEM_HEADS, MEM_HEAD_DIM)[None]

    xs2 = x_sample.reshape(nd * tq, d)
    q, kv, gates, up_new, mg = front(xs2, g_mix[l], wts)
    kvc, kvs, kvw = (kv[:, i * KV_W:(i + 1) * KV_W].reshape(nd, tq, KV_W) for i in range(3))
    o = nsa_sample(q.reshape(nd, tq, Q_W), gates.reshape(nd, tq, LANES), kvs, kvw,
                   cache_kv_cmp[l].reshape(-1, HEAD_DIM), cache_kv_sel[l].reshape(-1, HEAD_DIM),
                   cache_kv_win[l].reshape(-1, HEAD_DIM), page_table, cmp_pe[l], cmp_w1[l], cmp_w2[l])
    kvc, kvs, kvw = (a.reshape(kv_shape(nd, tq)) for a in (kvc, kvs, kvw))
    win_new = jnp.concatenate([cache_kv_win[l], kvw], axis=1)[:, -WINDOW:]
    up_new = up_new.reshape(nd, tq, CONV_CH)
    act = conv_act(cache_conv[l], up_new, w_dw[l], b_dw[l], ln_conv_g[l], ln_conv_b[l])
    xs = mixer_tail(xs2, o.reshape(nd * tq, Q_W).astype(bf), act.reshape(nd * tq, CONV_CH), mg, wts)
    xs = cross_attn_block(xs, nd, cache_mem_kv[l].reshape(nd, MEM_LEN, 2 * MEM_W), g_xattn[l], wts)
    s_conv = jnp.concatenate([cache_conv[l], up_new], axis=1)[:, tq:][None]

    y_prompt, y_sample = moe_layer([xp, xs], g_moe[l], w_router[l], b_router[l], w_gu[l], b_gu[l], w_down[l],
                                   b_down[l], g_final)
    y_prompt, y_sample = y_prompt.reshape(nb, t, d), y_sample.reshape(nd, tq, d)
    return (y_prompt, y_sample, p_kv_cmp, p_kv_sel, p_kv_win, p_conv, p_mem, kvc[None], kvs[None], win_new[None],
            s_conv)
```

```python
import functools

import jax
import jax.numpy as jnp
from jax import lax
from jax.experimental import pallas as pl
from jax.experimental.pallas import tpu as pltpu

D_MODEL = 2048
PAGE_SIZE = 128

N_HEADS = 16
HEAD_DIM = 128
N_KV = 2
HPG = N_HEADS // N_KV
CMP_STRIDE = 16
CMP_LEN = 2 * CMP_STRIDE
CMP_HALF = CMP_STRIDE * HEAD_DIM
SEL_BLOCK = 64
SEL_TOPK = 16
WINDOW = 512
SCALE = HEAD_DIM ** -0.5
EXP2_SCALE = SCALE * 1.4426950408889634
CONV_CH = D_MODEL // 2
CONV_WIDTH = 31
MEM_LEN = 256
MEM_HEADS = 4
MEM_HEAD_DIM = 128
MEM_W = MEM_HEADS * MEM_HEAD_DIM
MEM_SCALE = MEM_HEAD_DIM ** -0.5
N_EXPERTS = 32
TOP_K = 4
SWIGLU_LIMIT = 7.0
SWIGLU_ALPHA = 1.702
EPS = 1e-6
Q_W = N_HEADS * HEAD_DIM
KV_W = 2 * N_KV * HEAD_DIM
KV_ROWS = KV_W // HEAD_DIM
GATE_W = 3 * N_HEADS
GLU_W = 2 * CONV_CH
MERGE_W = 2 * D_MODEL
IN_SPLITS = [Q_W, Q_W + KV_W, Q_W + 2 * KV_W, Q_W + 3 * KV_W, Q_W + 3 * KV_W + GATE_W, Q_W + 3 * KV_W + GATE_W + GLU_W]

V7X_VMEM_BYTES = 64 * 1024 * 1024
VMEM_LIMIT_BYTES = V7X_VMEM_BYTES - 8 * 1024 * 1024
LANES = 128

MOE_BLOCK_ROWS = 1152
MOE_FF_TILE = 256
NSA_Q_TILE = 128
NSA_KV_TILE = 1024
CONV_CTX = 32
CONV_ROWS = 64
CHUNKS_PER_PAGE = PAGE_SIZE // CMP_STRIDE
CMP_PAGES = 32
SLAB_PITCH = CMP_PAGES * CHUNKS_PER_PAGE + 8
DEC_PAD = 8

NEG = -1e30
_NT = (((1,), (1,)), ((), ()))


def _params(*semantics):
    return pltpu.CompilerParams(dimension_semantics=semantics, vmem_limit_bytes=VMEM_LIMIT_BYTES)


def _row_block(rows, target):
    if rows <= target:
        return rows
    for b in range(target, 15, -1):
        if rows % b == 0 and b % 16 == 0:
            return b
    return rows


def _linear_kernel(*refs, n_w, norm, n_tile, n_row, epilogue):
    x_ref = refs[0]
    k = 1
    g_ref = refs[k] if norm else None
    k += int(norm)
    w_refs = refs[k:k + n_w]
    k += n_w
    tile_refs = refs[k:k + n_tile]
    k += n_tile
    row_refs = refs[k:k + n_row]
    k += n_row
    o_ref, h_ref = refs[k], refs[k + 1]

    @pl.when(pl.program_id(1) == 0)
    def _():
        x = x_ref[...].astype(jnp.float32)
        if norm:
            x = x * lax.rsqrt(jnp.mean(x * x, axis=-1, keepdims=True) + EPS) * g_ref[...]
        h_ref[...] = x.astype(jnp.bfloat16)

    h = h_ref[...]
    zs = [jnp.dot(h, w[...], preferred_element_type=jnp.float32) for w in w_refs]
    o_ref[...] = epilogue(zs, [t[...] for t in tile_refs], [r[...] for r in row_refs]).astype(o_ref.dtype)


def fused_linear(x, ws, *, gain=None, tiles=(), rows=(), epilogue=None, out_dtype=jnp.float32,
                 block_m=1024, block_n=512, name="fused_linear"):
    m, k = x.shape
    n = ws[0].shape[1]
    bm = _row_block(m, block_m)
    bn = min(block_n, n)
    assert m % bm == 0 and n % bn == 0 and all(w.shape == (k, n) for w in ws)
    if epilogue is None:
        epilogue = _first_epilogue
    norm = gain is not None
    in_specs = [pl.BlockSpec((bm, k), lambda i, j: (i, 0))]
    args = [x]
    if norm:
        in_specs.append(pl.BlockSpec((1, k), lambda i, j: (0, 0)))
        args.append(gain.reshape(1, k).astype(jnp.float32))
    in_specs += [pl.BlockSpec((k, bn), lambda i, j: (0, j)) for _ in ws]
    args += list(ws)
    in_specs += [pl.BlockSpec((bm, bn), lambda i, j: (i, j)) for _ in tiles]
    args += list(tiles)
    in_specs += [pl.BlockSpec((1, bn), lambda i, j: (0, j)) for _ in rows]
    args += list(rows)
    return pl.pallas_call(
        functools.partial(_linear_kernel, n_w=len(ws), norm=norm, n_tile=len(tiles), n_row=len(rows),
                          epilogue=epilogue),
        grid=(m // bm, n // bn),
        in_specs=in_specs,
        out_specs=pl.BlockSpec((bm, bn), lambda i, j: (i, j)),
        out_shape=jax.ShapeDtypeStruct((m, n), out_dtype),
        scratch_shapes=[pltpu.VMEM((bm, k), jnp.bfloat16)],
        compiler_params=_params("parallel", "arbitrary"),
        name=name,
    )(*args)


def _first_epilogue(zs, ts, rs):
    return zs[0]


def _glu_epilogue(zs, ts, rs):
    return zs[0] * jax.nn.sigmoid(zs[1])


def _sigmoid_epilogue(zs, ts, rs):
    return jax.nn.sigmoid(zs[0])


def _residual_epilogue(zs, ts, rs):
    return ts[0] + zs[0]


def _merge_kernel(o_ref, a_ref, wo_ref, wc_ref, ga_ref, gc_ref, out_ref):
    y_att = jnp.dot(o_ref[...], wo_ref[...], preferred_element_type=jnp.float32)
    y_conv = jnp.dot(a_ref[...], wc_ref[...], preferred_element_type=jnp.float32)
    out_ref[...] = (ga_ref[...] * y_att + gc_ref[...] * y_conv).astype(out_ref.dtype)


def merge_branches(o, act, w_nsa_o, w_conv_o, merge_gates, *, block_m=1024, block_n=512):
    m = o.shape[0]
    d = w_nsa_o.shape[1]
    bm = _row_block(m, block_m)
    bn = min(block_n, d)
    nj = d // bn
    return pl.pallas_call(
        _merge_kernel,
        grid=(m // bm, nj),
        in_specs=[pl.BlockSpec((bm, o.shape[1]), lambda i, j: (i, 0)),
                  pl.BlockSpec((bm, act.shape[1]), lambda i, j: (i, 0)),
                  pl.BlockSpec((w_nsa_o.shape[0], bn), lambda i, j: (0, j)),
                  pl.BlockSpec((w_conv_o.shape[0], bn), lambda i, j: (0, j)),
                  pl.BlockSpec((bm, bn), lambda i, j: (i, j)),
                  pl.BlockSpec((bm, bn), lambda i, j: (i, j + nj))],
        out_specs=pl.BlockSpec((bm, bn), lambda i, j: (i, j)),
        out_shape=jax.ShapeDtypeStruct((m, d), jnp.bfloat16),
        compiler_params=_params("parallel", "arbitrary"),
        name="merge_branches",
    )(o, act, w_nsa_o, w_conv_o, merge_gates, merge_gates)


def _compress_kernel(x_ref, w1_ref, pe_ref, w2_ref, o_ref):
    c = x_ref.shape[1]
    for j in range(2):
        w_a = w1_ref[j, :, :HEAD_DIM]
        w_b = w1_ref[j, :, HEAD_DIM:]
        pe = pe_ref[j]
        bias = (jnp.dot(pe[:, :CMP_HALF], w_a, preferred_element_type=jnp.float32)
                + jnp.dot(pe[:, CMP_HALF:], w_b, preferred_element_type=jnp.float32))[0:1]
        xs = []
        for g in range(N_KV):
            base = (j * N_KV + g) * HEAD_DIM
            xs.append(jnp.concatenate(
                [x_ref[0, :, i * KV_W + base:i * KV_W + base + HEAD_DIM] for i in range(CMP_STRIDE)], axis=1))
        x = jnp.concatenate(xs, axis=0).astype(jnp.bfloat16)
        ab = jnp.dot(x, w1_ref[j], preferred_element_type=jnp.float32)
        a, b = ab[:, :HEAD_DIM], ab[:, HEAD_DIM:]
        for g in range(N_KV):
            b_next = pltpu.roll(b[g * c:(g + 1) * c], c - 1, axis=0)
            h = jax.nn.gelu(a[g * c:(g + 1) * c] + b_next + bias)
            o_ref[0, j, g] = jnp.dot(h.astype(jnp.bfloat16), w2_ref[j],
                                     preferred_element_type=jnp.float32).astype(o_ref.dtype)


def _compress_weights(cmp_pe, cmp_w1, cmp_w2):
    bf = jnp.bfloat16
    w1 = jnp.concatenate([cmp_w1[:, :CMP_STRIDE].reshape(2, CMP_HALF, HEAD_DIM),
                          cmp_w1[:, CMP_STRIDE:].reshape(2, CMP_HALF, HEAD_DIM)], axis=-1).astype(bf)
    pe = jnp.broadcast_to(cmp_pe.reshape(2, 1, CMP_LEN * HEAD_DIM), (2, 8, CMP_LEN * HEAD_DIM)).astype(bf)
    return w1, pe, cmp_w2.astype(bf)


def compress_prompt(kvc, cmp_pe, cmp_w1, cmp_w2):
    n, t, _ = kvc.shape
    c = t // CMP_STRIDE
    w1, pe, w2 = _compress_weights(cmp_pe, cmp_w1, cmp_w2)
    return pl.pallas_call(
        _compress_kernel,
        grid=(n,),
        in_specs=[pl.BlockSpec((1, c, CMP_STRIDE * KV_W), lambda b: (b, 0, 0)),
                  pl.BlockSpec(w1.shape, lambda b: (0, 0, 0)),
                  pl.BlockSpec(pe.shape, lambda b: (0, 0, 0)),
                  pl.BlockSpec(w2.shape, lambda b: (0, 0, 0))],
        out_specs=pl.BlockSpec((1, 2, N_KV, c, HEAD_DIM), lambda b: (b, 0, 0, 0, 0)),
        out_shape=jax.ShapeDtypeStruct((n, 2, N_KV, c, HEAD_DIM), jnp.bfloat16),
        compiler_params=_params("parallel"),
        name="compress_prompt",
    )(kvc.reshape(n, c, CMP_STRIDE * KV_W), w1, pe, w2)


def _softmax_rows(s, mask, every_row_valid=False):
    sm = jnp.where(mask, s, NEG)
    e = jnp.exp2((sm - jnp.max(sm, axis=-1, keepdims=True)) * EXP2_SCALE)
    if not every_row_valid:
        e = jnp.where(mask, e, 0.0)
    return e / jnp.maximum(jnp.sum(e, axis=-1, keepdims=True), 1e-30)


def _select_blocks_t(score_t, qpos_l):
    blk = lax.broadcasted_iota(jnp.int32, score_t.shape, 0)
    cur = lax.shift_right_logical(qpos_l, SEL_BLOCK.bit_length() - 1)
    avail = blk <= cur
    forced = avail & ((blk == 0) | (blk == cur) | (blk == cur - 1))
    work = jnp.where(forced, jnp.inf, jnp.where(avail, score_t, -jnp.inf))
    blk_f = blk.astype(jnp.float32)
    sel = jnp.zeros(score_t.shape, jnp.float32)
    for _ in range(SEL_TOPK):
        m = jnp.max(work, axis=0, keepdims=True)
        first = jnp.min(jnp.where(work == m, blk_f, float(score_t.shape[0])), axis=0, keepdims=True)
        hit = blk_f == first
        sel = jnp.where(hit & (m > -jnp.inf), 1.0, sel)
        work = jnp.where(hit, -jnp.inf, work)
    return sel


def _nsa_prompt_kernel(q_ref, gate_ref, ckv_ref, ks_ref, kw_ref, o_ref, m_ref, l_ref, acc_ref, *, tq, tk):
    b = pl.program_id(1)
    q0 = b * tq
    qpos = q0 + lax.broadcasted_iota(jnp.int32, (tq, 1), 0)
    qpos_l = q0 + lax.broadcasted_iota(jnp.int32, (1, tq), 1)
    n_cmp = ckv_ref.shape[3]
    gates = gate_ref[0]
    blocks_per_tile = tk // SEL_BLOCK

    for g in range(N_KV):
        kcol = slice(g * HEAD_DIM, (g + 1) * HEAD_DIM)
        vcol = slice((N_KV + g) * HEAD_DIM, (N_KV + g + 1) * HEAD_DIM)
        qg = jnp.concatenate([q_ref[0, :, (g * HPG + hh) * HEAD_DIM:(g * HPG + hh + 1) * HEAD_DIM]
                              for hh in range(HPG)], axis=0)

        s = lax.dot_general(qg, ckv_ref[0, 0, g], _NT, preferred_element_type=jnp.float32)
        cend = lax.broadcasted_iota(jnp.int32, (1, n_cmp), 1) * CMP_STRIDE + (CMP_LEN - 1)
        vis = cend <= qpos
        ps = [_softmax_rows(s[hh * tq:(hh + 1) * tq], vis) for hh in range(HPG)]
        o_cmp = jnp.dot(jnp.concatenate(ps, axis=0).astype(jnp.bfloat16), ckv_ref[0, 1, g],
                        preferred_element_type=jnp.float32)
        p_grp = ps[0]
        for p in ps[1:]:
            p_grp = p_grp + p

        c0 = lax.broadcasted_iota(jnp.int32, (LANES, n_cmp), 1) * CMP_STRIDE
        s0 = lax.broadcasted_iota(jnp.int32, (LANES, n_cmp), 0) * SEL_BLOCK
        overlap_t = ((c0 < s0 + SEL_BLOCK) & (c0 + CMP_LEN > s0)).astype(jnp.bfloat16)
        score_t = lax.dot_general(overlap_t, p_grp.astype(jnp.bfloat16), _NT, preferred_element_type=jnp.float32)
        sel = _select_blocks_t(score_t, qpos_l).T.astype(jnp.bfloat16)

        m_ref[...] = jnp.full(m_ref.shape, NEG, jnp.float32)
        l_ref[...] = jnp.zeros(l_ref.shape, jnp.float32)
        acc_ref[...] = jnp.zeros(acc_ref.shape, jnp.float32)

        def sel_tile(kt, causal):
            k0 = pl.multiple_of(kt * tk, tk)
            s = lax.dot_general(qg, ks_ref[0, pl.ds(k0, tk), kcol], _NT, preferred_element_type=jnp.float32)
            jrow = lax.broadcasted_iota(jnp.int32, (LANES, tk), 0)
            jcol = lax.shift_right_logical(lax.broadcasted_iota(jnp.int32, (LANES, tk), 1),
                                           SEL_BLOCK.bit_length() - 1)
            expand = (jrow == kt * blocks_per_tile + jcol).astype(jnp.bfloat16)
            mask = jnp.dot(sel, expand, preferred_element_type=jnp.float32) > 0.5
            if causal:
                mask = mask & (k0 + lax.broadcasted_iota(jnp.int32, (1, tk), 1) <= qpos)
            ps = []
            for hh in range(HPG):
                rows = slice(hh * tq, (hh + 1) * tq)
                sh = jnp.where(mask, s[rows], NEG)
                m_old = m_ref[rows]
                m_new = jnp.maximum(m_old, jnp.max(sh, axis=-1, keepdims=True))
                p = jnp.exp2((sh - m_new) * EXP2_SCALE)
                alpha = jnp.exp2((m_old - m_new) * EXP2_SCALE)
                l_ref[rows] = alpha * l_ref[rows] + jnp.sum(p, axis=-1, keepdims=True)
                acc_ref[rows] = alpha * acc_ref[rows]
                m_ref[rows] = m_new
                ps.append(p.astype(jnp.bfloat16))
            acc_ref[...] += jnp.dot(jnp.concatenate(ps, axis=0), ks_ref[0, pl.ds(k0, tk), vcol],
                                    preferred_element_type=jnp.float32)

        n_full = q0 // tk

        def full_tile(kt, carry):
            sel_tile(kt, False)
            return carry

        lax.fori_loop(0, n_full, full_tile, 0)
        sel_tile(n_full, True)
        o_sel = acc_ref[...] / l_ref[...]

        w0 = pl.multiple_of(jnp.maximum(q0 - WINDOW, 0), tq)
        s = lax.dot_general(qg, kw_ref[0, pl.ds(w0, WINDOW + tq), kcol], _NT, preferred_element_type=jnp.float32)
        kpos = w0 + lax.broadcasted_iota(jnp.int32, (1, WINDOW + tq), 1)
        mask = (kpos <= qpos) & (kpos > qpos - WINDOW)
        ps = [_softmax_rows(s[hh * tq:(hh + 1) * tq], mask, every_row_valid=True).astype(jnp.bfloat16)
              for hh in range(HPG)]
        o_win = jnp.dot(jnp.concatenate(ps, axis=0), kw_ref[0, pl.ds(w0, WINDOW + tq), vcol],
                        preferred_element_type=jnp.float32)

        for hh in range(HPG):
            head = g * HPG + hh
            rows = slice(hh * tq, (hh + 1) * tq)
            o = (gates[:, head:head + 1] * o_cmp[rows]
                 + gates[:, N_HEADS + head:N_HEADS + head + 1] * o_sel[rows]
                 + gates[:, 2 * N_HEADS + head:2 * N_HEADS + head + 1] * o_win[rows])
            o_ref[0, :, head * HEAD_DIM:(head + 1) * HEAD_DIM] = o.astype(o_ref.dtype)


def nsa_prompt(q, gates, ckv, kvs, kvw, *, tq=NSA_Q_TILE, tk=NSA_KV_TILE):
    n, t, _ = q.shape
    assert t % tk == 0 and tk % tq == 0 and t >= WINDOW + tq and t // SEL_BLOCK <= LANES
    rows = HPG * tq
    return pl.pallas_call(
        functools.partial(_nsa_prompt_kernel, tq=tq, tk=tk),
        grid=(n, t // tq),
        in_specs=[pl.BlockSpec((1, tq, Q_W), lambda b, i: (b, i, 0)),
                  pl.BlockSpec((1, tq, LANES), lambda b, i: (b, i, 0)),
                  pl.BlockSpec((1,) + ckv.shape[1:], lambda b, i: (b, 0, 0, 0, 0)),
                  pl.BlockSpec((1, t, KV_W), lambda b, i: (b, 0, 0)),
                  pl.BlockSpec((1, t, KV_W), lambda b, i: (b, 0, 0))],
        out_specs=pl.BlockSpec((1, tq, Q_W), lambda b, i: (b, i, 0)),
        out_shape=jax.ShapeDtypeStruct((n, t, Q_W), jnp.bfloat16),
        scratch_shapes=[pltpu.VMEM((rows, 1), jnp.float32), pltpu.VMEM((rows, 1), jnp.float32),
                        pltpu.VMEM((rows, HEAD_DIM), jnp.float32)],
        compiler_params=_params("parallel", "arbitrary"),
        name="nsa_prompt",
    )(q, gates, ckv, kvs, kvw)


def _conv_kernel(a_ref, p_ref, c_ref, w_ref, bias_ref, g_ref, beta_ref, o_ref, win_ref, y_ref, *, tt):
    win_ref[0:CONV_CTX, :] = jnp.where(pl.program_id(1) == 0, c_ref[0], p_ref[0])
    win_ref[CONV_CTX:CONV_CTX + tt, :] = a_ref[0]
    first = CONV_CTX - (CONV_WIDTH - 1)
    rc = min(CONV_ROWS, tt)
    for cc in range(CONV_CH // LANES):
        cols = slice(cc * LANES, (cc + 1) * LANES)
        for r0 in range(0, tt, rc):
            acc = jnp.broadcast_to(bias_ref[:, cols], (rc, LANES))
            for k in range(CONV_WIDTH):
                acc = acc + w_ref[k:k + 1, cols] * win_ref[r0 + first + k:r0 + first + k + rc, cols]
            y_ref[r0:r0 + rc, cols] = acc
    y = y_ref[...]
    yc = y - jnp.mean(y, axis=-1, keepdims=True)
    yn = yc * lax.rsqrt(jnp.mean(yc * yc, axis=-1, keepdims=True) + EPS) * g_ref[...] + beta_ref[...]
    o_ref[0] = (yn * jax.nn.sigmoid(yn)).astype(o_ref.dtype)


def conv_act(ctx, up, w_dw, b_dw, ln_g, ln_b, *, block_t=128):
    n, t, ch = up.shape
    tp = -(-t // CONV_CTX) * CONV_CTX
    tt = _row_block(tp, block_t)
    up_p = up if tp == t else jnp.pad(up, ((0, 0), (0, tp - t), (0, 0)))
    ctx_p = jnp.pad(ctx, ((0, 0), (CONV_CTX - ctx.shape[1], 0), (0, 0)))
    w = jnp.zeros((CONV_CTX, ch), jnp.float32).at[:CONV_WIDTH].set(w_dw)
    step = tt // CONV_CTX
    row = lambda v: v.reshape(1, ch).astype(jnp.float32)
    out = pl.pallas_call(
        functools.partial(_conv_kernel, tt=tt),
        grid=(n, tp // tt),
        in_specs=[pl.BlockSpec((1, tt, ch), lambda b, i: (b, i, 0)),
                  pl.BlockSpec((1, CONV_CTX, ch), lambda b, i: (b, jnp.maximum(i * step - 1, 0), 0)),
                  pl.BlockSpec((1, CONV_CTX, ch), lambda b, i: (b, 0, 0)),
                  pl.BlockSpec((CONV_CTX, ch), lambda b, i: (0, 0)),
                  pl.BlockSpec((1, ch), lambda b, i: (0, 0)),
                  pl.BlockSpec((1, ch), lambda b, i: (0, 0)),
                  pl.BlockSpec((1, ch), lambda b, i: (0, 0))],
        out_specs=pl.BlockSpec((1, tt, ch), lambda b, i: (b, i, 0)),
        out_shape=jax.ShapeDtypeStruct((n, tp, ch), jnp.bfloat16),
        scratch_shapes=[pltpu.VMEM((tt + CONV_CTX, ch), jnp.float32), pltpu.VMEM((tt, ch), jnp.float32)],
        compiler_params=_params("parallel", "arbitrary"),
        name="conv_act",
    )(up_p, up_p, ctx_p, w, row(b_dw), row(ln_g), row(ln_b))
    return out[:, :t]


def _cross_attn_kernel(q_ref, kv_ref, o_ref):
    outs = []
    for h in range(MEM_HEADS):
        lo = h * MEM_HEAD_DIM
        q = q_ref[0, :, lo:lo + MEM_HEAD_DIM]
        k = kv_ref[0, :, lo:lo + MEM_HEAD_DIM].astype(jnp.bfloat16)
        v = kv_ref[0, :, MEM_W + lo:MEM_W + lo + MEM_HEAD_DIM].astype(jnp.bfloat16)
        s = lax.dot_general(q, k, _NT, preferred_element_type=jnp.float32) * MEM_SCALE
        e = jnp.exp(s - jnp.max(s, axis=-1, keepdims=True))
        p = e / jnp.sum(e, axis=-1, keepdims=True)
        outs.append(jnp.dot(p.astype(jnp.bfloat16), v, preferred_element_type=jnp.float32))
    o_ref[0] = jnp.concatenate(outs, axis=-1).astype(o_ref.dtype)


def cross_attn_core(q, kv, *, block_q=512):
    n, t, _ = q.shape
    bq = _row_block(t, block_q)
    return pl.pallas_call(
        _cross_attn_kernel,
        grid=(n, t // bq),
        in_specs=[pl.BlockSpec((1, bq, MEM_W), lambda b, i: (b, i, 0)),
                  pl.BlockSpec((1, MEM_LEN, 2 * MEM_W), lambda b, i: (b, 0, 0))],
        out_specs=pl.BlockSpec((1, bq, MEM_W), lambda b, i: (b, i, 0)),
        out_shape=jax.ShapeDtypeStruct((n, t, MEM_W), jnp.bfloat16),
        compiler_params=_params("parallel", "arbitrary"),
        name="cross_attn_core",
    )(q, kv)


def _router_kernel(x_ref, g_ref, w_ref, b_ref, h_ref, e_ref, p_ref):
    x = x_ref[...]
    h = x * lax.rsqrt(jnp.mean(x * x, axis=-1, keepdims=True) + EPS) * g_ref[...]
    h_ref[...] = h
    logits = jnp.dot(h.astype(jnp.bfloat16), w_ref[...], preferred_element_type=jnp.float32) + b_ref[...]
    lt = logits.T[:N_EXPERTS]
    ids = lax.broadcasted_iota(jnp.int32, lt.shape, 0).astype(jnp.float32)
    vals, idxs = [], []
    for _ in range(TOP_K):
        m = jnp.max(lt, axis=0, keepdims=True)
        idx = jnp.min(jnp.where(lt == m, ids, float(N_EXPERTS)), axis=0, keepdims=True)
        vals.append(m)
        idxs.append(idx)
        lt = jnp.where(ids == idx, -jnp.inf, lt)
    top_v = jnp.concatenate(vals, axis=0)
    e = jnp.exp(top_v - vals[0])
    e_ref[...] = jnp.concatenate(idxs, axis=0).astype(jnp.int32)
    p_ref[...] = e / jnp.sum(e, axis=0, keepdims=True)


def moe_router(x, gain, w_router, b_router, *, block_m=512):
    m, d = x.shape
    bm = _row_block(m, block_m)
    w = jnp.zeros((d, LANES), jnp.bfloat16).at[:, :N_EXPERTS].set(w_router.astype(jnp.bfloat16))
    b = jnp.zeros((1, LANES), jnp.float32).at[0, :N_EXPERTS].set(b_router.astype(jnp.float32))
    return pl.pallas_call(
        _router_kernel,
        grid=(m // bm,),
        in_specs=[pl.BlockSpec((bm, d), lambda i: (i, 0)),
                  pl.BlockSpec((1, d), lambda i: (0, 0)),
                  pl.BlockSpec((d, LANES), lambda i: (0, 0)),
                  pl.BlockSpec((1, LANES), lambda i: (0, 0))],
        out_specs=[pl.BlockSpec((bm, d), lambda i: (i, 0)),
                   pl.BlockSpec((TOP_K, bm), lambda i: (0, i)),
                   pl.BlockSpec((TOP_K, bm), lambda i: (0, i))],
        out_shape=[jax.ShapeDtypeStruct((m, d), jnp.float32),
                   jax.ShapeDtypeStruct((TOP_K, m), jnp.int32),
                   jax.ShapeDtypeStruct((TOP_K, m), jnp.float32)],
        compiler_params=_params("parallel"),
        name="moe_router",
    )(x, gain.reshape(1, d).astype(jnp.float32), w, b)


def _expert_kernel(blk_e_ref, blk_rows_ref, tok_cur_ref, tok_nxt_ref, h_hbm, wg_ref, wu_ref, bg_ref, bu_ref,
                   wd_ref, bd_ref, o_ref, x_stage, x_cur, sem, *, nf):
    i, j = pl.program_id(0), pl.program_id(1)
    rows = blk_rows_ref[i]
    share = MOE_BLOCK_ROWS // nf

    def row_copy(tok, r):
        return pltpu.make_async_copy(h_hbm.at[pl.ds(tok, 1)], x_stage.at[pl.ds(r, 1)], sem.at[0])

    def for_all_rows(fn):
        def body(r, carry):
            fn(r)
            return carry
        lax.fori_loop(0, MOE_BLOCK_ROWS, body, 0, unroll=8)

    @pl.when(j == 0)
    def _():
        o_ref[...] = jnp.broadcast_to(bd_ref[0], o_ref.shape)

        @pl.when(i == 0)
        def _():
            for_all_rows(lambda r: row_copy(tok_cur_ref[0, 0, r], r).start())

        @pl.when((i == 0) | (blk_rows_ref[jnp.maximum(i - 1, 0)] > 0))
        def _():
            for_all_rows(lambda r: row_copy(0, r).wait())

        @pl.when(rows > 0)
        def _():
            x_cur[...] = x_stage[...].astype(jnp.bfloat16)

    @pl.when(rows > 0)
    def _():
        for r in range(share):
            row_copy(tok_nxt_ref[0, 0, j * share + r], j * share + r).start(priority=r % 2)
        x = x_cur[...]
        a = jnp.dot(x, wg_ref[0].astype(jnp.bfloat16), preferred_element_type=jnp.float32) + bg_ref[0]
        u = jnp.dot(x, wu_ref[0].astype(jnp.bfloat16), preferred_element_type=jnp.float32) + bu_ref[0]
        a = jnp.minimum(a, SWIGLU_LIMIT)
        u = jnp.clip(u, -SWIGLU_LIMIT, SWIGLU_LIMIT)
        act = a * jax.nn.sigmoid(SWIGLU_ALPHA * a) * (u + 1.0)
        o_ref[...] += jnp.dot(act.astype(jnp.bfloat16), wd_ref[0].astype(jnp.bfloat16),
                              preferred_element_type=jnp.float32)


def moe_experts(h, slot_tok, blk_e, blk_rows, w_gu, b_gu, w_down, b_down):
    s_total = slot_tok.shape[0]
    d = w_gu.shape[1]
    n_e, _, two_ff = w_gu.shape
    ff = two_ff // 2
    nb = s_total // MOE_BLOCK_ROWS
    nf = ff // MOE_FF_TILE
    tok3 = slot_tok.reshape(nb, 1, MOE_BLOCK_ROWS)

    def col(i, j, rows_ref):
        return jnp.where(rows_ref[i] > 0, j, nf - 1)

    grid_spec = pltpu.PrefetchScalarGridSpec(
        num_scalar_prefetch=2,
        grid=(nb, nf),
        in_specs=[
            pl.BlockSpec((1, 1, MOE_BLOCK_ROWS), lambda i, j, e, r: (i, 0, 0), memory_space=pltpu.SMEM),
            pl.BlockSpec((1, 1, MOE_BLOCK_ROWS), lambda i, j, e, r: (jnp.minimum(i + 1, nb - 1), 0, 0),
                         memory_space=pltpu.SMEM),
            pl.BlockSpec(memory_space=pl.ANY),
            pl.BlockSpec((1, d, MOE_FF_TILE), lambda i, j, e, r: (e[i], 0, col(i, j, r))),
            pl.BlockSpec((1, d, MOE_FF_TILE), lambda i, j, e, r: (e[i], 0, nf + col(i, j, r))),
            pl.BlockSpec((1, 1, MOE_FF_TILE), lambda i, j, e, r: (e[i], 0, col(i, j, r))),
            pl.BlockSpec((1, 1, MOE_FF_TILE), lambda i, j, e, r: (e[i], 0, nf + col(i, j, r))),
            pl.BlockSpec((1, MOE_FF_TILE, d), lambda i, j, e, r: (e[i], col(i, j, r), 0)),
            pl.BlockSpec((1, 1, d), lambda i, j, e, r: (e[i], 0, 0)),
        ],
        out_specs=pl.BlockSpec((MOE_BLOCK_ROWS, d), lambda i, j, e, r: (i, 0)),
        scratch_shapes=[pltpu.VMEM((MOE_BLOCK_ROWS, d), jnp.float32),
                        pltpu.VMEM((MOE_BLOCK_ROWS, d), jnp.bfloat16),
                        pltpu.SemaphoreType.DMA((1,))],
    )
    b_gu3 = b_gu.reshape(n_e, 1, two_ff)
    return pl.pallas_call(
        functools.partial(_expert_kernel, nf=nf),
        grid_spec=grid_spec,
        out_shape=jax.ShapeDtypeStruct((s_total, d), jnp.float32),
        compiler_params=_params("arbitrary", "arbitrary"),
        name="moe_experts",
    )(blk_e, blk_rows, tok3, tok3, h, w_gu, w_gu, b_gu3, b_gu3, w_down, b_down.reshape(n_e, 1, d))


def moe_layer(x_groups, gain, w_router, b_router, w_gu, b_gu, w_down, b_down, g_final):
    routed = [moe_router(x, gain, w_router, b_router) for x in x_groups]
    h = jnp.concatenate([r[0] for r in routed], axis=0)
    top_e = jnp.concatenate([r[1] for r in routed], axis=1)
    top_p = jnp.concatenate([r[2] for r in routed], axis=1)
    t = top_e.shape[1]
    bm = MOE_BLOCK_ROWS
    n_blocks = (t * TOP_K) // bm + N_EXPERTS + 1
    onehot = (top_e[:, :, None] == jnp.arange(N_EXPERTS)[None, None, :]).astype(jnp.int32)
    per_tok = jnp.sum(onehot, axis=0)
    before = jnp.cumsum(per_tok, axis=0) - per_tok
    counts = jnp.sum(per_tok, axis=0)
    blocks_e = (counts + bm - 1) // bm
    blk_end = jnp.cumsum(blocks_e)
    blk_start = blk_end - blocks_e
    first_slot = before + (blk_start * bm)[None, :]
    slot = jnp.sum(onehot * first_slot[None], axis=-1).astype(jnp.int32)
    tok_ids = jnp.broadcast_to(jnp.arange(t, dtype=jnp.int32)[None], slot.shape)
    slot_tok = jnp.zeros((n_blocks * bm,), jnp.int32).at[slot.reshape(-1)].set(tok_ids.reshape(-1))
    blk = jnp.arange(n_blocks)
    blk_e = jnp.minimum(jnp.searchsorted(blk_end, blk, side='right'), N_EXPERTS - 1).astype(jnp.int32)
    used = blk < blk_end[-1]
    last_e = blk_e[jnp.maximum(blk_end[-1] - 1, 0)]
    blk_e = jnp.where(used, blk_e, last_e).astype(jnp.int32)
    blk_rows = jnp.where(used, jnp.clip(counts[blk_e] - (blk - blk_start[blk_e]) * bm, 0, bm), 0).astype(jnp.int32)
    outs = moe_experts(h, slot_tok, blk_e, blk_rows, w_gu.astype(jnp.bfloat16), b_gu,
                       w_down.astype(jnp.bfloat16), b_down)
    ys, start = [], 0
    for x in x_groups:
        m = x.shape[0]
        ys.append(moe_combine_norm(x, outs, slot[:, start:start + m], top_p[:, start:start + m], g_final))
        start += m
    return ys


def _combine_kernel(idx_cur_ref, idx_nxt_ref, x_ref, p_ref, g_ref, outs_hbm, o_ref, buf, sem, *, bm):
    i = pl.program_id(0)
    cur = lax.rem(i, 2)
    n_rows = TOP_K * bm

    def row_copy(slot, dst, r):
        return pltpu.make_async_copy(outs_hbm.at[pl.ds(slot, 1)], buf.at[dst, pl.ds(r, 1)], sem.at[dst])

    def for_rows(fn):
        for r in range(n_rows):
            fn(r, r)

    @pl.when(i == 0)
    def _():
        for_rows(lambda r, k: row_copy(idx_cur_ref[0, 0, r], cur, r).start(priority=k % 2))

    for_rows(lambda r, k: row_copy(0, cur, r).wait())

    @pl.when(i + 1 < pl.num_programs(0))
    def _():
        for_rows(lambda r, k: row_copy(idx_nxt_ref[0, 0, r], 1 - cur, r).start(priority=k % 2))

    x = x_ref[...]
    for k in range(TOP_K):
        x = x + p_ref[:, k:k + 1] * buf[cur, k * bm:(k + 1) * bm, :]
    o_ref[...] = x * lax.rsqrt(jnp.mean(x * x, axis=-1, keepdims=True) + EPS) * g_ref[...]


def moe_combine_norm(x, outs, slot, gate, gain, *, block_m=128):
    m, d = x.shape
    bm = _row_block(m, block_m)
    nt = m // bm
    idx = slot.reshape(TOP_K, nt, bm).transpose(1, 0, 2).reshape(nt, 1, TOP_K * bm)
    return pl.pallas_call(
        functools.partial(_combine_kernel, bm=bm),
        grid=(nt,),
        in_specs=[pl.BlockSpec((1, 1, TOP_K * bm), lambda i: (i, 0, 0), memory_space=pltpu.SMEM),
                  pl.BlockSpec((1, 1, TOP_K * bm), lambda i: (jnp.minimum(i + 1, nt - 1), 0, 0),
                               memory_space=pltpu.SMEM),
                  pl.BlockSpec((bm, d), lambda i: (i, 0)),
                  pl.BlockSpec((bm, TOP_K), lambda i: (i, 0)),
                  pl.BlockSpec((1, d), lambda i: (0, 0)),
                  pl.BlockSpec(memory_space=pl.ANY)],
        out_specs=pl.BlockSpec((bm, d), lambda i: (i, 0)),
        out_shape=jax.ShapeDtypeStruct((m, d), jnp.float32),
        scratch_shapes=[pltpu.VMEM((2, TOP_K * bm, d), jnp.float32), pltpu.SemaphoreType.DMA((2,))],
        compiler_params=_params("arbitrary"),
        name="moe_combine_norm",
    )(idx, idx, x, gate.T, gain.reshape(1, d).astype(jnp.float32), outs)


def _hist_compress_kernel(pt_ref, *refs):
    pages, (w1_ref, pe_ref, w2_ref, o_ref, slab_ref) = refs[:CMP_PAGES + 1], refs[CMP_PAGES + 1:]
    c = CMP_PAGES * CHUNKS_PER_PAGE
    last = lax.broadcasted_iota(jnp.int32, (c, 1), 0) == c - 1

    slab_rows = CMP_STRIDE * KV_ROWS
    regs_per_chunk = slab_rows // 8
    for k in range(CMP_PAGES + 1):
        chunks = CHUNKS_PER_PAGE if k < CMP_PAGES else 1
        if k == CMP_PAGES:
            for s in range(slab_rows):
                slab_ref[s * SLAB_PITCH + c:s * SLAB_PITCH + c + 8, :] = jnp.zeros((8, HEAD_DIM), jnp.float32)
        for cc in range(chunks):
            for sg in range(regs_per_chunk):
                v = cc * regs_per_chunk + sg
                slab_ref[pl.ds(8 * sg * SLAB_PITCH + k * CHUNKS_PER_PAGE + cc, 8, stride=SLAB_PITCH), :] = (
                    pages[k][v * 8:(v + 1) * 8, :])

    def rows(sub, lo, n):
        return jnp.concatenate([slab_ref[(i * KV_ROWS + sub) * SLAB_PITCH + lo:(i * KV_ROWS + sub) * SLAB_PITCH + lo + n, :]
                                for i in range(CMP_STRIDE)], axis=1)

    for j in range(2):
        w_a = w1_ref[j, :, :HEAD_DIM]
        w_b = w1_ref[j, :, HEAD_DIM:]
        pe = pe_ref[j]
        bias = (jnp.dot(pe[:, :CMP_HALF], w_a, preferred_element_type=jnp.float32)
                + jnp.dot(pe[:, CMP_HALF:], w_b, preferred_element_type=jnp.float32))[0:1]
        bases = [j * N_KV + g for g in range(N_KV)]
        x = jnp.concatenate([rows(base, 0, c) for base in bases], axis=0).astype(jnp.bfloat16)
        x_next = jnp.concatenate([rows(base, c, CHUNKS_PER_PAGE) for base in bases], axis=0).astype(jnp.bfloat16)
        ab = jnp.dot(x, w1_ref[j], preferred_element_type=jnp.float32)
        a, b = ab[:, :HEAD_DIM], ab[:, HEAD_DIM:]
        b_tail = jnp.dot(x_next, w_b, preferred_element_type=jnp.float32)
        for g in range(N_KV):
            b_next = jnp.where(last, b_tail[g * CHUNKS_PER_PAGE:g * CHUNKS_PER_PAGE + 1],
                               pltpu.roll(b[g * c:(g + 1) * c], c - 1, axis=0))
            h = jax.nn.gelu(a[g * c:(g + 1) * c] + b_next + bias)
            o_ref[0, j, g] = jnp.dot(h.astype(jnp.bfloat16), w2_ref[j],
                                     preferred_element_type=jnp.float32).astype(o_ref.dtype)


def compress_history(pool, page_table, cmp_pe, cmp_w1, cmp_w2):
    n, n_pages = page_table.shape
    assert n_pages % CMP_PAGES == 0
    c = n_pages * CHUNKS_PER_PAGE
    w1, pe, w2 = _compress_weights(cmp_pe, cmp_w1, cmp_w2)

    def page_spec(k):
        return pl.BlockSpec((PAGE_SIZE * KV_ROWS, HEAD_DIM),
                            lambda b, s, pt: (pt[b * n_pages + jnp.minimum(s * CMP_PAGES + k, n_pages - 1)], 0))

    grid_spec = pltpu.PrefetchScalarGridSpec(
        num_scalar_prefetch=1,
        grid=(n, n_pages // CMP_PAGES),
        in_specs=[page_spec(k) for k in range(CMP_PAGES + 1)] + [
            pl.BlockSpec(w1.shape, lambda b, s, pt: (0, 0, 0)),
            pl.BlockSpec(pe.shape, lambda b, s, pt: (0, 0, 0)),
            pl.BlockSpec(w2.shape, lambda b, s, pt: (0, 0, 0))],
        out_specs=pl.BlockSpec((1, 2, N_KV, CMP_PAGES * CHUNKS_PER_PAGE, HEAD_DIM), lambda b, s, pt: (b, 0, 0, s, 0)),
        scratch_shapes=[pltpu.VMEM((CMP_STRIDE * KV_ROWS * SLAB_PITCH, HEAD_DIM), jnp.float32)],
    )
    return pl.pallas_call(
        _hist_compress_kernel,
        grid_spec=grid_spec,
        out_shape=jax.ShapeDtypeStruct((n, 2, N_KV, c, HEAD_DIM), jnp.bfloat16),
        compiler_params=_params("parallel", "arbitrary"),
        name="compress_history",
    )(page_table.reshape(-1), *([pool] * (CMP_PAGES + 1)), w1, pe, w2)


def _nsa_sample_dense_kernel(q_ref, ckv_ref, win_ref, new_ref, ocmp_ref, owin_ref, idx_ref, val_ref, *,
                             past, n_cmp, n_sel):
    rows = HPG * DEC_PAD
    tok = lax.bitwise_and(lax.broadcasted_iota(jnp.int32, (rows, 1), 0), DEC_PAD - 1)
    qpos = past + tok
    qpos_l = past + lax.broadcasted_iota(jnp.int32, (1, LANES), 1)
    c_all = ckv_ref.shape[3]
    n_sel_pad = -(-n_sel // 8) * 8
    wlen = WINDOW + LANES

    def window_rows(sub):
        return jnp.concatenate([win_ref[pl.ds(sub, WINDOW, stride=KV_ROWS), :],
                                new_ref[0, pl.ds(sub, DEC_PAD, stride=KV_ROWS), :],
                                jnp.zeros((LANES - DEC_PAD, HEAD_DIM), jnp.float32)], axis=0).astype(jnp.bfloat16)

    for g in range(N_KV):
        qg = q_ref[0, g]
        s = lax.dot_general(qg, ckv_ref[0, 0, g], _NT, preferred_element_type=jnp.float32)
        cidx = lax.broadcasted_iota(jnp.int32, (1, c_all), 1)
        vis = (cidx * CMP_STRIDE + (CMP_LEN - 1) <= qpos) & (cidx < n_cmp)
        p = _softmax_rows(s, vis)
        ocmp_ref[0, g] = jnp.dot(p.astype(jnp.bfloat16), ckv_ref[0, 1, g], preferred_element_type=jnp.float32)
        p_grp = p[0:DEC_PAD]
        for hh in range(1, HPG):
            p_grp = p_grp + p[hh * DEC_PAD:(hh + 1) * DEC_PAD]
        p_grp = jnp.concatenate([p_grp, jnp.zeros((LANES - DEC_PAD, c_all), jnp.float32)], axis=0)
        c0 = lax.broadcasted_iota(jnp.int32, (n_sel_pad, c_all), 1) * CMP_STRIDE
        s0 = lax.broadcasted_iota(jnp.int32, (n_sel_pad, c_all), 0) * SEL_BLOCK
        overlap_t = ((c0 < s0 + SEL_BLOCK) & (c0 + CMP_LEN > s0)).astype(jnp.bfloat16)
        score_t = lax.dot_general(overlap_t, p_grp.astype(jnp.bfloat16), _NT, preferred_element_type=jnp.float32)
        blk = lax.broadcasted_iota(jnp.int32, score_t.shape, 0)
        cur = lax.shift_right_logical(qpos_l, SEL_BLOCK.bit_length() - 1)
        avail = (blk <= cur) & (blk < n_sel)
        forced = avail & ((blk == 0) | (blk == cur) | (blk == cur - 1))
        work = jnp.where(forced, jnp.inf, jnp.where(avail, score_t, -jnp.inf))
        blk_f = blk.astype(jnp.float32)
        idxs, vals = [], []
        for _ in range(SEL_TOPK):
            m = jnp.max(work, axis=0, keepdims=True)
            first = jnp.min(jnp.where(work == m, blk_f, float(n_sel_pad)), axis=0, keepdims=True)
            idxs.append(first)
            vals.append(jnp.where(m > -jnp.inf, 1.0, 0.0))
            work = jnp.where(blk_f == first, -jnp.inf, work)
        idx_ref[0, g] = jnp.concatenate(idxs, axis=0).astype(jnp.int32)
        val_ref[0, g] = jnp.concatenate(vals, axis=0).astype(jnp.int32)

        s = lax.dot_general(qg, window_rows(g), _NT, preferred_element_type=jnp.float32)
        kpos = past - WINDOW + lax.broadcasted_iota(jnp.int32, (1, wlen), 1)
        p = _softmax_rows(s, (kpos <= qpos) & (kpos > qpos - WINDOW))
        owin_ref[0, g] = jnp.dot(p.astype(jnp.bfloat16), window_rows(N_KV + g), preferred_element_type=jnp.float32)


def nsa_sample_dense(qd, ckv, win_rows, new_rows, *, past, n_cmp, n_sel):
    n = qd.shape[0]
    rows = HPG * DEC_PAD
    o_shape = jax.ShapeDtypeStruct((n, N_KV, rows, HEAD_DIM), jnp.float32)
    i_shape = jax.ShapeDtypeStruct((n, N_KV, SEL_TOPK, LANES), jnp.int32)
    o_spec = pl.BlockSpec((1, N_KV, rows, HEAD_DIM), lambda b: (b, 0, 0, 0))
    i_spec = pl.BlockSpec((1, N_KV, SEL_TOPK, LANES), lambda b: (b, 0, 0, 0))
    return pl.pallas_call(
        functools.partial(_nsa_sample_dense_kernel, past=past, n_cmp=n_cmp, n_sel=n_sel),
        grid=(n,),
        in_specs=[pl.BlockSpec((1, N_KV, rows, HEAD_DIM), lambda b: (b, 0, 0, 0)),
                  pl.BlockSpec((1,) + ckv.shape[1:], lambda b: (b, 0, 0, 0, 0)),
                  pl.BlockSpec((WINDOW * KV_ROWS, HEAD_DIM), lambda b: (b, 0)),
                  pl.BlockSpec((1,) + new_rows.shape[1:], lambda b: (b, 0, 0))],
        out_specs=[o_spec, o_spec, i_spec, i_spec],
        out_shape=[o_shape, o_shape, i_shape, i_shape],
        compiler_params=_params("parallel"),
        name="nsa_sample_dense",
    )(qd, ckv, win_rows, new_rows)


def _nsa_sample_sel_kernel(idx_ref, val_ref, pt_ref, q_ref, new_ref, *refs, tq, past, n_past_blk):
    blocks, o_ref = refs[:N_KV * SEL_TOPK], refs[N_KV * SEL_TOPK]
    b, t = pl.program_id(0), pl.program_id(1)
    qpos = past + t
    lane = lax.broadcasted_iota(jnp.int32, (1, SEL_TOPK * SEL_BLOCK), 1)
    slot_of_lane = lax.shift_right_logical(lane, SEL_BLOCK.bit_length() - 1)
    for g in range(N_KV):
        base = ((b * tq + t) * N_KV + g) * SEL_TOPK
        ks, vs = [], []
        tokpos = jnp.zeros(lane.shape, jnp.int32)
        ok = jnp.zeros(lane.shape, jnp.int32)
        for s in range(SEL_TOPK):
            blk = idx_ref[base + s]
            from_pool = blk < n_past_blk
            pool_blk = blocks[g * SEL_TOPK + s]
            k_rows = pl.ds(g, SEL_BLOCK, stride=KV_ROWS)
            v_rows = pl.ds(N_KV + g, SEL_BLOCK, stride=KV_ROWS)
            ks.append(jnp.where(from_pool, pool_blk[k_rows, :], new_ref[0, k_rows, :]))
            vs.append(jnp.where(from_pool, pool_blk[v_rows, :], new_ref[0, v_rows, :]))
            here = slot_of_lane == s
            tokpos = jnp.where(here, blk * SEL_BLOCK, tokpos)
            ok = jnp.where(here, val_ref[base + s], ok)
        tokpos = tokpos + lax.bitwise_and(lane, SEL_BLOCK - 1)
        mask = (ok > 0) & (tokpos <= qpos)
        k = jnp.concatenate(ks, axis=0).astype(jnp.bfloat16)
        v = jnp.concatenate(vs, axis=0).astype(jnp.bfloat16)
        s = lax.dot_general(q_ref[0, 0, g], k, _NT, preferred_element_type=jnp.float32)
        p = _softmax_rows(s, mask)
        o_ref[0, 0, g] = jnp.dot(p.astype(jnp.bfloat16), v, preferred_element_type=jnp.float32)


def nsa_sample_sel(qs, idx, valid, page_table, pool, new_blocks, *, past):
    n, tq = qs.shape[:2]
    n_pages = page_table.shape[1]
    bpp = PAGE_SIZE // SEL_BLOCK
    n_past_blk = n_pages * bpp

    def block_spec(g, s):
        def index(b, t, idx_ref, val_ref, pt_ref):
            blk = jnp.minimum(idx_ref[((b * tq + t) * N_KV + g) * SEL_TOPK + s], n_past_blk - 1)
            return (pt_ref[b * n_pages + blk // bpp] * bpp + blk % bpp, 0)
        return pl.BlockSpec((SEL_BLOCK * KV_ROWS, HEAD_DIM), index)

    grid_spec = pltpu.PrefetchScalarGridSpec(
        num_scalar_prefetch=3,
        grid=(n, tq),
        in_specs=[pl.BlockSpec((1, 1, N_KV, 16, HEAD_DIM), lambda b, t, *_: (b, t, 0, 0, 0)),
                  pl.BlockSpec((1, SEL_BLOCK * KV_ROWS, HEAD_DIM), lambda b, t, *_: (b, 0, 0))]
                 + [block_spec(g, s) for g in range(N_KV) for s in range(SEL_TOPK)],
        out_specs=pl.BlockSpec((1, 1, N_KV, 16, HEAD_DIM), lambda b, t, *_: (b, t, 0, 0, 0)),
    )
    return pl.pallas_call(
        functools.partial(_nsa_sample_sel_kernel, tq=tq, past=past, n_past_blk=n_past_blk),
        grid_spec=grid_spec,
        out_shape=jax.ShapeDtypeStruct((n, tq, N_KV, 16, HEAD_DIM), jnp.float32),
        compiler_params=_params("parallel", "arbitrary"),
        name="nsa_sample_sel",
    )(idx.reshape(-1), valid.reshape(-1), page_table.reshape(-1), qs, new_blocks,
      *([pool] * (N_KV * SEL_TOPK)))


def nsa_sample(q, gates, kvs_new, kvw_new, pool_cmp, pool_sel, win_cache, page_table, cmp_pe, cmp_w1, cmp_w2):
    n, tq, _ = q.shape
    past = page_table.shape[1] * PAGE_SIZE
    assert tq <= DEC_PAD and tq <= SEL_BLOCK and win_cache.shape == (n * WINDOW * KV_ROWS, HEAD_DIM)
    assert (past + tq) // CMP_STRIDE * CMP_STRIDE == past
    n_cmp = past // CMP_STRIDE - 1
    n_sel = -(-(past + tq) // SEL_BLOCK)
    ckv = compress_history(pool_cmp, page_table, cmp_pe, cmp_w1, cmp_w2)
    q5 = q.reshape(n, tq, N_KV, HPG, HEAD_DIM)
    qd = jnp.pad(q5.transpose(0, 2, 3, 1, 4), ((0, 0), (0, 0), (0, 0), (0, DEC_PAD - tq), (0, 0)))
    qd = qd.reshape(n, N_KV, HPG * DEC_PAD, HEAD_DIM)
    new_win = jnp.pad(kvw_new, ((0, 0), (0, DEC_PAD - tq), (0, 0))).reshape(n, DEC_PAD * KV_ROWS, HEAD_DIM)
    o_cmp, o_win, idx, valid = nsa_sample_dense(qd, ckv, win_cache, new_win, past=past, n_cmp=n_cmp, n_sel=n_sel)
    idx = idx[..., :tq].transpose(0, 3, 1, 2)
    valid = valid[..., :tq].transpose(0, 3, 1, 2)
    qs = jnp.pad(q5, ((0, 0), (0, 0), (0, 0), (0, 16 - HPG), (0, 0)))
    new_blocks = jnp.pad(kvs_new, ((0, 0), (0, SEL_BLOCK - tq), (0, 0))).reshape(n, SEL_BLOCK * KV_ROWS, HEAD_DIM)
    o_sel = nsa_sample_sel(qs, idx, valid, page_table, pool_sel, new_blocks, past=past)[:, :, :, :HPG]
    unpack = lambda o: o.reshape(n, N_KV, HPG, DEC_PAD, HEAD_DIM)[:, :, :, :tq].transpose(0, 3, 1, 2, 4)
    g = gates[..., :GATE_W].reshape(n, tq, 3, N_KV, HPG)[..., None]
    o = g[:, :, 0] * unpack(o_cmp) + g[:, :, 1] * o_sel + g[:, :, 2] * unpack(o_win)
    return o.reshape(n, tq, Q_W)


def prepare_weights(w_in, w_nsa_o, w_conv_o, w_out, w_xq, w_xkv, w_xo):
    bf = jnp.bfloat16
    w = w_in.astype(bf)
    d = w.shape[0]
    s = IN_SPLITS
    return dict(
        q=w[:, :s[0]], kv=w[:, s[0]:s[3]],
        gates=jnp.zeros((d, LANES), bf).at[:, :GATE_W].set(w[:, s[3]:s[4]]),
        glu_a=w[:, s[4]:s[4] + CONV_CH], glu_b=w[:, s[4] + CONV_CH:s[5]], merge=w[:, s[5]:],
        nsa_o=w_nsa_o.astype(bf), conv_o=w_conv_o.astype(bf), out=w_out.astype(bf),
        xq=w_xq.astype(bf), xkv=w_xkv.astype(bf), xo=w_xo.astype(bf))


def front(x2, g_mix, wts):
    q = fused_linear(x2, [wts['q']], gain=g_mix, out_dtype=jnp.bfloat16, name="front_q")
    kv = fused_linear(x2, [wts['kv']], gain=g_mix, name="front_kv")
    gates = fused_linear(x2, [wts['gates']], gain=g_mix, epilogue=_sigmoid_epilogue, name="front_gates")
    up = fused_linear(x2, [wts['glu_a'], wts['glu_b']], gain=g_mix, epilogue=_glu_epilogue, name="front_glu")
    mg = fused_linear(x2, [wts['merge']], gain=g_mix, epilogue=_sigmoid_epilogue, name="front_merge")
    return q, kv, gates, up, mg


def mixer_tail(x2, o, act, mg, wts):
    merged = merge_branches(o, act, wts['nsa_o'], wts['conv_o'], mg)
    return fused_linear(merged, [wts['out']], tiles=[x2], epilogue=_residual_epilogue, name="mixer_out_proj")


def cross_attn_block(x2, n, mkv, g_xattn, wts):
    m = x2.shape[0]
    qx = fused_linear(x2, [wts['xq']], gain=g_xattn, out_dtype=jnp.bfloat16, name="xattn_q")
    oc = cross_attn_core(qx.reshape(n, m // n, MEM_W), mkv)
    return fused_linear(oc.reshape(m, MEM_W), [wts['xo']], tiles=[x2], epilogue=_residual_epilogue, name="xattn_o")


def kernel(x_prompt, x_sample, cache_kv_cmp, cache_kv_sel, cache_kv_win, cache_conv, cache_mem_kv, page_table,
           mem_prompt, g_mix, w_in, cmp_pe, cmp_w1, cmp_w2, w_nsa_o, w_dw, b_dw, ln_conv_g, ln_conv_b, w_conv_o,
           w_out, g_xattn, g_mem, w_xq, w_xkv, w_xo, g_moe, w_router, b_router, w_gu, b_gu, w_down, b_down, g_final):
    nb, t, d = x_prompt.shape
    nd, tq = x_sample.shape[:2]
    l = 0
    bf = jnp.bfloat16
    wts = prepare_weights(w_in[l], w_nsa_o[l], w_conv_o[l], w_out[l], w_xq[l], w_xkv[l], w_xo[l])
    kv_shape = lambda n_, t_: (n_, t_, 2, N_KV, HEAD_DIM)

    xp2 = x_prompt.reshape(nb * t, d)
    q, kv, gates, up_new, mg = front(xp2, g_mix[l], wts)
    kvc, kvs, kvw = (kv[:, i * KV_W:(i + 1) * KV_W].reshape(nb, t, KV_W) for i in range(3))
    ckv = compress_prompt(kvc, cmp_pe[l], cmp_w1[l], cmp_w2[l])
    o = nsa_prompt(q.reshape(nb, t, Q_W), gates.reshape(nb, t, LANES), ckv, kvs.astype(bf), kvw.astype(bf))
    up_new = up_new.reshape(nb, t, CONV_CH)
    act = conv_act(jnp.zeros((nb, CONV_WIDTH - 1, CONV_CH), jnp.float32), up_new, w_dw[l], b_dw[l],
                   ln_conv_g[l], ln_conv_b[l])
    xp = mixer_tail(xp2, o.reshape(nb * t, Q_W), act.reshape(nb * t, CONV_CH), mg, wts)
    mkv = fused_linear(mem_prompt.reshape(nb * MEM_LEN, d), [wts['xkv']], gain=g_mem[l], name="mem_kv")
    xp = cross_attn_block(xp, nb, mkv.reshape(nb, MEM_LEN, 2 * MEM_W), g_xattn[l], wts)
    p_kv_cmp, p_kv_sel = kvc.reshape(kv_shape(nb, t))[None], kvs.reshape(kv_shape(nb, t))[None]
    p_kv_win = kvw.reshape(kv_shape(nb, t))[:, t - min(WINDOW, t):][None]
    p_conv = jnp.pad(up_new, ((0, 0), (CONV_WIDTH - 1, 0), (0, 0)))[:, t:][None]
    p_mem = mkv.reshape(nb, MEM_LEN, 2, MEM_HEADS, MEM_HEAD_DIM)[None]

    xs2 = x_sample.reshape(nd * tq, d)
    q, kv, gates, up_new, mg = front(xs2, g_mix[l], wts)
    kvc, kvs, kvw = (kv[:, i * KV_W:(i + 1) * KV_W].reshape(nd, tq, KV_W) for i in range(3))
    o = nsa_sample(q.reshape(nd, tq, Q_W), gates.reshape(nd, tq, LANES), kvs, kvw,
                   cache_kv_cmp[l].reshape(-1, HEAD_DIM), cache_kv_sel[l].reshape(-1, HEAD_DIM),
                   cache_kv_win[l].reshape(-1, HEAD_DIM), page_table, cmp_pe[l], cmp_w1[l], cmp_w2[l])
    kvc, kvs, kvw = (a.reshape(kv_shape(nd, tq)) for a in (kvc, kvs, kvw))
    win_new = jnp.concatenate([cache_kv_win[l], kvw], axis=1)[:, -WINDOW:]
    up_new = up_new.reshape(nd, tq, CONV_CH)
    act = conv_act(cache_conv[l], up_new, w_dw[l], b_dw[l], ln_conv_g[l], ln_conv_b[l])
    xs = mixer_tail(xs2, o.reshape(nd * tq, Q_W).astype(bf), act.reshape(nd * tq, CONV_CH), mg, wts)
    xs = cross_attn_block(xs, nd, cache_mem_kv[l].reshape(nd, MEM_LEN, 2 * MEM_W), g_xattn[l], wts)
    s_conv = jnp.concatenate([cache_conv[l], up_new], axis=1)[:, tq:][None]

    y_prompt, y_sample = moe_layer([xp, xs], g_moe[l], w_router[l], b_router[l], w_gu[l], b_gu[l], w_down[l],
                                   b_down[l], g_final)
    y_prompt, y_sample = y_prompt.reshape(nb, t, d), y_sample.reshape(nd, tq, d)
    return (y_prompt, y_sample, p_kv_cmp, p_kv_sel, p_kv_win, p_conv, p_mem, kvc[None], kvs[None], win_new[None],
            s_conv)
```
